```python
import jax, jax.numpy as jnp
from jax import lax
import numpy as np

D_MODEL = 1024
BATCH = 16
SEQ = 4096
DEPTH = 1
DEC_BATCH = 32
DEC_SEQ = 64
PAST_LEN = 1024

CHUNK = 64
SUB = 16
N_SUB = CHUNK // SUB
GLA_HEADS = 4
GLA_V = D_MODEL // 2
GLA_K = GLA_V // 2
GLA_DK = GLA_K // GLA_HEADS
GLA_DV = GLA_V // GLA_HEADS
GATE_RANK = 16
GATE_TAU = 16.0
HGRN_HEADS = 4
HGRN_W = D_MODEL - GLA_V
HGRN_D = HGRN_W // HGRN_HEADS
D_MIX = GLA_V + HGRN_W
IN_SPLITS = (GLA_K, GLA_K, GLA_V, GLA_V, GATE_RANK, HGRN_W, HGRN_W, HGRN_W, HGRN_W)
IN_COLS = sum(IN_SPLITS)
EPS = 1e-6

kernel_name = 'hymba_gla_hgrn2_stream_step'


def rms_norm(x, g):
    xf = x.astype(jnp.float32)
    y = xf * lax.rsqrt(jnp.mean(xf * xf, axis=-1, keepdims=True) + EPS)
    return (y * g.astype(jnp.float32)).astype(x.dtype)


def chunked_gla(q, k, v, g, s0):
    B, T, H, K = q.shape
    V = v.shape[-1]
    out_dtype = v.dtype
    pad = (-T) % CHUNK
    n = (T + pad) // CHUNK

    def to_chunks(a):
        a = jnp.pad(a.astype(jnp.float32), ((0, 0), (0, pad), (0, 0), (0, 0)))
        return a.reshape(B, n, CHUNK, H, a.shape[-1]).transpose(1, 0, 3, 2, 4)

    pos = jnp.arange(CHUNK)
    off_mask = pos[None, None, :] < (jnp.arange(N_SUB) * SUB)[:, None, None]
    diag_mask = jnp.tril(jnp.ones((SUB, SUB), dtype=bool))[:, :, None]

    def step(S, inp):
        qc, kc, vc, gc = inp
        b = jnp.cumsum(gc, axis=2)
        o_inter = jnp.einsum('bhck,bhkv->bhcv', qc * jnp.exp(b), S)
        qs = qc.reshape(B, H, N_SUB, SUB, K)
        ks = kc.reshape(B, H, N_SUB, SUB, K)
        vs = vc.reshape(B, H, N_SUB, SUB, V)
        bs = b.reshape(B, H, N_SUB, SUB, K)
        r = bs[:, :, :, 0] - gc.reshape(B, H, N_SUB, SUB, K)[:, :, :, 0]
        q_off = qs * jnp.exp(bs - r[:, :, :, None])
        k_off = kc[:, :, None] * jnp.exp(jnp.minimum(r[:, :, :, None] - b[:, :, None], 0.0))
        a_off = jnp.where(off_mask, jnp.einsum('bhnlk,bhnsk->bhnls', q_off, k_off), 0.0)
        diff = bs[:, :, :, :, None] - bs[:, :, :, None]
        decay = jnp.exp(jnp.where(diag_mask, diff, -jnp.inf))
        a_diag = jnp.einsum('bhntsk,bhnsk->bhnts', qs[:, :, :, :, None] * decay, ks)
        o_intra = (jnp.einsum('bhnls,bhsv->bhnlv', a_off, vc)
                   + jnp.einsum('bhnts,bhnsv->bhntv', a_diag, vs))
        b_last = b[:, :, -1]
        S_new = (S * jnp.exp(b_last)[..., None]
                 + jnp.einsum('bhck,bhcv->bhkv', kc * jnp.exp(b_last[:, :, None] - b), vc))
        return S_new, o_inter + o_intra.reshape(B, H, CHUNK, V)

    S, o = lax.scan(step, s0.astype(jnp.float32),
                    (to_chunks(q), to_chunks(k), to_chunks(v), to_chunks(g)))
    o = o.transpose(1, 0, 3, 2, 4).reshape(B, n * CHUNK, H, V)[:, :T]
    return o.astype(out_dtype), S


def mixer_layer(x, c, s_gla, s_hgrn, w_ada, b_ada, g_pre, w_in, w_alpha, b_alpha,
                g_on_gla, lb, g_on_hgrn, w_out, g_post):
    B, T, _ = x.shape
    mod = jnp.einsum('bd,de->be', c, w_ada) + b_ada
    shift, scale, gate = jnp.split(mod, 3, axis=-1)
    h = rms_norm(x, g_pre) * (1 + scale[:, None]) + shift[:, None]
    p = jnp.einsum('btd,de->bte', h, w_in)
    q_a, k_a, v_a, z_a, a_lr, q_h, f_h, i_h, z_h = jnp.split(
        p, np.cumsum(IN_SPLITS)[:-1].tolist(), axis=-1)

    log_alpha = jax.nn.log_sigmoid(
        (jnp.einsum('btr,rk->btk', a_lr, w_alpha) + b_alpha).astype(jnp.float32)) / GATE_TAU
    o_a, s_a = chunked_gla(
        q_a.reshape(B, T, GLA_HEADS, GLA_DK) * GLA_DK ** -0.5,
        k_a.reshape(B, T, GLA_HEADS, GLA_DK),
        v_a.reshape(B, T, GLA_HEADS, GLA_DV),
        log_alpha.reshape(B, T, GLA_HEADS, GLA_DK), s_gla)

    f = lb + (1.0 - lb) * jax.nn.sigmoid(f_h.astype(jnp.float32))
    o_h, s_h = chunked_gla(
        jax.nn.silu(q_h).reshape(B, T, HGRN_HEADS, HGRN_D),
        (1.0 - f).reshape(B, T, HGRN_HEADS, HGRN_D),
        i_h.reshape(B, T, HGRN_HEADS, HGRN_D),
        jnp.log(f).reshape(B, T, HGRN_HEADS, HGRN_D), s_hgrn)

    o_a = rms_norm(o_a, g_on_gla).reshape(B, T, GLA_V) * jax.nn.silu(z_a)
    o_h = rms_norm(o_h, g_on_hgrn).reshape(B, T, HGRN_W) * jax.nn.silu(z_h)
    o = jnp.einsum('bte,ed->btd', jnp.concatenate([o_a, o_h], axis=-1), w_out)
    y = x + gate[:, None] * rms_norm(o, g_post)
    return y, s_a, s_h


def setup_inputs(seed: int = 0) -> dict:
    key = jax.random.key(seed)
    ks = jax.random.split(key, 20)
    nrm = jax.random.normal
    f32 = jnp.float32
    return {
        'x_prompt': nrm(ks[0], (BATCH, SEQ, D_MODEL), f32),
        'x_sample': nrm(ks[1], (DEC_BATCH, DEC_SEQ, D_MODEL), f32),
        'c_prompt': nrm(ks[2], (BATCH, D_MODEL), f32),
        'c_sample': nrm(ks[3], (DEC_BATCH, D_MODEL), f32),
        'state_gla': 0.3 * nrm(ks[4], (DEPTH, DEC_BATCH, GLA_HEADS, GLA_DK, GLA_DV), f32),
        'state_hgrn': 0.3 * nrm(ks[5], (DEPTH, DEC_BATCH, HGRN_HEADS, HGRN_D, HGRN_D), f32),
        'w_ada': 0.5 * D_MODEL ** -0.5 * nrm(ks[6], (DEPTH, D_MODEL, 3 * D_MODEL), f32),
        'b_ada': 0.02 * nrm(ks[7], (DEPTH, 3 * D_MODEL), f32),
        'g_pre': 1.0 + 0.05 * nrm(ks[8], (DEPTH, D_MODEL), f32),
        'w_in': D_MODEL ** -0.5 * nrm(ks[9], (DEPTH, D_MODEL, IN_COLS), f32),
        'w_alpha': GATE_RANK ** -0.5 * nrm(ks[10], (DEPTH, GATE_RANK, GLA_K), f32),
        'b_alpha': 0.1 * nrm(ks[11], (DEPTH, GLA_K), f32),
        'g_onorm_gla': 1.0 + 0.05 * nrm(ks[12], (DEPTH, GLA_DV), f32),
        'hgrn_lb_logits': 0.3 * nrm(ks[13], (DEPTH + 1, HGRN_W), f32),
        'g_onorm_hgrn': 1.0 + 0.05 * nrm(ks[14], (DEPTH, HGRN_D), f32),
        'w_out': D_MIX ** -0.5 * nrm(ks[15], (DEPTH, D_MIX, D_MODEL), f32),
        'g_post': 1.0 + 0.05 * nrm(ks[16], (DEPTH, D_MODEL), f32),
    }


def reference(x_prompt, x_sample, c_prompt, c_sample, state_gla, state_hgrn, w_ada, b_ada, g_pre,
              w_in, w_alpha, b_alpha, g_onorm_gla, hgrn_lb_logits, g_onorm_hgrn, w_out, g_post):
    lower_bounds = jnp.cumsum(jax.nn.softmax(hgrn_lb_logits.astype(jnp.float32), axis=0), axis=0)
    bp = x_prompt.shape[0]
    zero_gla = jnp.zeros((bp, GLA_HEADS, GLA_DK, GLA_DV), jnp.float32)
    zero_hgrn = jnp.zeros((bp, HGRN_HEADS, HGRN_D, HGRN_D), jnp.float32)
    yp, ys = x_prompt, x_sample
    gla_p, hgrn_p, gla_s, hgrn_s = [], [], [], []
    for l in range(DEPTH):
        wl = (w_ada[l], b_ada[l], g_pre[l], w_in[l], w_alpha[l], b_alpha[l], g_onorm_gla[l],
              lower_bounds[l], g_onorm_hgrn[l], w_out[l], g_post[l])
        yp, sa, sh = mixer_layer(yp, c_prompt, zero_gla, zero_hgrn, *wl)
        gla_p.append(sa)
        hgrn_p.append(sh)
        ys, sa, sh = mixer_layer(ys, c_sample, state_gla[l], state_hgrn[l], *wl)
        gla_s.append(sa)
        hgrn_s.append(sh)
    new_gla_prompt = jnp.stack(gla_p).astype(x_prompt.dtype)
    new_hgrn_prompt = jnp.stack(hgrn_p).astype(x_prompt.dtype)
    new_gla_sample = jnp.stack(gla_s).astype(state_gla.dtype)
    new_hgrn_sample = jnp.stack(hgrn_s).astype(state_hgrn.dtype)
    return (yp, ys, new_gla_prompt, new_hgrn_prompt, new_gla_sample, new_hgrn_sample)
```

```python
import functools

import numpy as np
import jax
import jax.numpy as jnp
from jax import lax
from jax.experimental import pallas as pl
from jax.experimental.pallas import tpu as pltpu

D_MODEL = 1024
CHUNK = 64
GLA_HEADS = 4
GLA_V = D_MODEL // 2
GLA_K = GLA_V // 2
GLA_DK = GLA_K // GLA_HEADS
GLA_DV = GLA_V // GLA_HEADS
GATE_RANK = 16
GATE_TAU = 16.0
HGRN_HEADS = 4
HGRN_W = D_MODEL - GLA_V
HGRN_D = HGRN_W // HGRN_HEADS
EPS = 1e-6

LANES = 128
SUBLANES = 8
N_HEAD_TILES = GLA_HEADS + HGRN_HEADS
K_LANES = GLA_K + HGRN_W
N_K_TILES = K_LANES // LANES
RANK_PAD = LANES
HALF_SIZES = (32, 16, 8, 4, 2, 1)
N_LEVELS = len(HALF_SIZES) + 1
ROWS_PER_STEP = 256
VMEM_LIMIT_BYTES = 56 * 1024 * 1024

F32 = jnp.float32
BF16 = jnp.bfloat16


def _k_tile_of_head(ht):
    return ht // 2 if ht < GLA_HEADS else ht - GLA_HEADS // 2


def _heads_of_k_tile(lt):
    return (2 * lt, 2 * lt + 1) if lt < GLA_K // LANES else (lt + GLA_HEADS // 2,)


def _level_masks():
    t = np.arange(CHUNK)[:, None]
    s = np.arange(CHUNK)[None, :]
    masks = [(t // (2 * m)) == (s // (2 * m)) for m in HALF_SIZES]
    masks.append(t == s)
    return np.stack(masks).astype(np.float32)


def _rms(x):
    return x * lax.rsqrt(jnp.mean(x * x, axis=-1, keepdims=True) + EPS)


def _sigmoid(x):
    return 1.0 / (1.0 + jnp.exp(-x))


def _log_sigmoid(x):
    return jnp.minimum(x, 0.0) - jnp.log(1.0 + jnp.exp(-jnp.abs(x)))


def _ada_kernel(c_ref, w_ref, b_ref, gpre_ref, out_ref):
    n = pl.program_id(0)
    val = jnp.dot(c_ref[...].astype(BF16), w_ref[...].astype(BF16),
                  preferred_element_type=F32) + b_ref[...]
    out_ref[...] = jnp.where(n == 1, gpre_ref[...] * (1.0 + val), val)


def _ada_call(c_all, w_ada, b_ada, g_pre):
    nb = c_all.shape[0]
    return pl.pallas_call(
        _ada_kernel,
        grid=(3,),
        in_specs=[
            pl.BlockSpec((nb, D_MODEL), lambda n: (0, 0)),
            pl.BlockSpec((D_MODEL, D_MODEL), lambda n: (0, n)),
            pl.BlockSpec((1, D_MODEL), lambda n: (0, n)),
            pl.BlockSpec((1, D_MODEL), lambda n: (0, 0)),
        ],
        out_specs=pl.BlockSpec((nb, D_MODEL), lambda n: (0, n)),
        out_shape=jax.ShapeDtypeStruct((nb, 3 * D_MODEL), F32),
        name="ada_mod",
    )(c_all, w_ada, b_ada, g_pre)


def _ref_rows(b_ref, block, off):
    pieces = []
    if block >= SUBLANES:
        for j in range(CHUNK // block):
            row = b_ref[j * block + off:j * block + off + 1, :]
            pieces.append(jnp.broadcast_to(row, (block, LANES)))
    else:
        assert block == SUBLANES // 2
        sub = lax.broadcasted_iota(jnp.int32, (SUBLANES, LANES), 0)
        for i in range(CHUNK // SUBLANES):
            lo = b_ref[SUBLANES * i + off:SUBLANES * i + off + 1, :]
            hi = b_ref[SUBLANES * i + block + off:SUBLANES * i + block + off + 1, :]
            pieces.append(jnp.where(sub < block,
                                    jnp.broadcast_to(lo, (SUBLANES, LANES)),
                                    jnp.broadcast_to(hi, (SUBLANES, LANES))))
    return jnp.concatenate(pieces, axis=0)


def _mixer_kernel(x_ref, mods_ref, st0_ref, win_ref, walpha_ref, balpha_ref, lbl_ref, gon_ref,
                  wout_ref, gpost_ref, masks_ref,
                  y_ref, st_ref,
                  hb_s, q_s, k_s, g_s, v_s, zs_s, o_s, b_s,
                  *, bb, tt):
    rows = bb * tt
    j = pl.program_id(1)

    @pl.when(j == 0)
    def _():
        st_ref[...] = st0_ref[...]

    def mod_rows(i):
        if bb == 1:
            return mods_ref[0, i:i + 1, :]
        return jnp.concatenate(
            [jnp.broadcast_to(mods_ref[bi, i:i + 1, :], (tt, D_MODEL)) for bi in range(bb)], axis=0)

    x = x_ref[...].reshape(rows, D_MODEL)
    hb_s[...] = (_rms(x) * mod_rows(1) + mod_rows(0)).astype(BF16)

    def proj(c0, c1):
        return jnp.dot(hb_s[...], win_ref[:, c0:c1], preferred_element_type=F32)

    c = 0
    pq = proj(c, c + K_LANES)
    c += K_LANES
    q_s[:, :GLA_K] = pq[:, :GLA_K] * (GLA_DK ** -0.5)
    qh = pq[:, GLA_K:]
    q_s[:, GLA_K:] = qh * _sigmoid(qh)

    pk = proj(c, c + K_LANES)
    c += K_LANES
    k_s[:, :GLA_K] = pk[:, :GLA_K]
    lbl = lbl_ref[...]
    lmax = jnp.max(lbl, axis=0, keepdims=True)
    lexp = jnp.exp(lbl - lmax)
    lb = lexp[0:1, :] / jnp.sum(lexp, axis=0, keepdims=True)
    f = lb + (1.0 - lb) * _sigmoid(pk[:, GLA_K:])
    k_s[:, GLA_K:] = 1.0 - f
    g_s[:, GLA_K:] = jnp.log(f)

    v_s[...] = proj(c, c + D_MODEL).astype(BF16)
    c += D_MODEL
    pz = proj(c, c + D_MODEL)
    c += D_MODEL
    zs_s[...] = pz * _sigmoid(pz)
    a_lr = proj(c, c + RANK_PAD)
    ga = jnp.dot(a_lr.astype(BF16), walpha_ref[...], preferred_element_type=F32) + balpha_ref[...]
    g_s[:, :GLA_K] = _log_sigmoid(ga) * (1.0 / GATE_TAU)

    row_i = lax.broadcasted_iota(jnp.int32, (CHUNK, LANES), 0)
    lane_i = lax.broadcasted_iota(jnp.int32, (CHUNK, LANES), 1)
    nt = (((1,), (1,)), ((), ()))
    tn = (((0,), (0,)), ((), ()))

    def chunk_body(ci, carry):
        base = pl.multiple_of(ci * CHUNK, CHUNK)
        bi = ci // (tt // CHUNK)
        for lt in range(N_K_TILES):
            lanes = slice(lt * LANES, (lt + 1) * LANES)
            g = g_s[pl.ds(base, CHUNK), lanes]
            q = q_s[pl.ds(base, CHUNK), lanes]
            k = k_s[pl.ds(base, CHUNK), lanes]
            b = g
            for sh in (1, 2, 4):
                b = b + jnp.where(row_i >= sh, pltpu.roll(b, sh, 0), 0.0)
            for sh in (8, 16, 32):
                b = b + jnp.concatenate([jnp.zeros((sh, LANES), F32), b[:CHUNK - sh]], axis=0)
            b_s[...] = b
            b_last = b_s[CHUNK - 1:CHUNK, :]
            q_state = q * jnp.exp(b)
            k_state = k * jnp.exp(b_last - b)
            lhs, rhs = [], []
            for m in HALF_SIZES:
                upper = (row_i & (2 * m - 1)) >= m
                if m == 1:
                    e = jnp.exp(jnp.where(upper, g, 0.0))
                else:
                    d = b - _ref_rows(b_s, 2 * m, m - 1)
                    e = jnp.exp(jnp.where(upper, d, -d))
                lhs.append(jnp.where(upper, q * e, 0.0))
                rhs.append(jnp.where(upper, 0.0, k * e).astype(BF16))
            lhs.append(q)
            rhs.append(k.astype(BF16))

            heads = _heads_of_k_tile(lt)
            st = st_ref[bi, lt]
            st_b = st.astype(BF16)
            st_new = st * jnp.exp(b_last)
            for hi, ht in enumerate(heads):
                if len(heads) == 1:
                    hmask = None
                else:
                    hmask = (lane_i >= hi * GLA_DK) & (lane_i < (hi + 1) * GLA_DK)

                def head(a):
                    return (a if hmask is None else jnp.where(hmask, a, 0.0)).astype(BF16)

                att = jnp.zeros((CHUNK, CHUNK), F32)
                for lv in range(N_LEVELS):
                    p = lax.dot_general(head(lhs[lv]), rhs[lv], nt, preferred_element_type=F32)
                    att = att + p * masks_ref[lv]
                v = v_s[pl.ds(base, CHUNK), ht * LANES:(ht + 1) * LANES]
                o = jnp.dot(att.astype(BF16), v, preferred_element_type=F32)
                o = o + lax.dot_general(head(q_state), st_b, nt, preferred_element_type=F32)
                o_s[pl.ds(base, CHUNK), ht * LANES:(ht + 1) * LANES] = o
                st_new = st_new + lax.dot_general(v, head(k_state), tn, preferred_element_type=F32)
            st_ref[bi, lt] = st_new
        return carry

    lax.fori_loop(0, rows // CHUNK, chunk_body, 0)

    gated = []
    for ht in range(N_HEAD_TILES):
        lanes = slice(ht * LANES, (ht + 1) * LANES)
        gated.append((_rms(o_s[:, lanes]) * gon_ref[:, lanes] * zs_s[:, lanes]).astype(BF16))
    proj_out = jnp.dot(jnp.concatenate(gated, axis=1), wout_ref[...], preferred_element_type=F32)
    y = x_ref[...].reshape(rows, D_MODEL) + mod_rows(2) * (_rms(proj_out) * gpost_ref[...])
    y_ref[...] = y.reshape(bb, tt, D_MODEL)


def _mixer_call(x, mods, st0, w_in, w_alpha, b_alpha, lb_logits, g_on, w_out, g_post, masks, *, bb, tt):
    nb, seq, _ = x.shape
    rows = bb * tt
    assert nb % bb == 0 and seq % tt == 0 and tt % CHUNK == 0
    in_cols = w_in.shape[1]
    const2 = lambda b, j: (0, 0)
    kernel = functools.partial(_mixer_kernel, bb=bb, tt=tt)
    return pl.pallas_call(
        kernel,
        grid=(nb // bb, seq // tt),
        in_specs=[
            pl.BlockSpec((bb, tt, D_MODEL), lambda b, j: (b, j, 0)),
            pl.BlockSpec((bb, 3, D_MODEL), lambda b, j: (b, 0, 0)),
            pl.BlockSpec((bb, N_K_TILES, LANES, LANES), lambda b, j: (b, 0, 0, 0)),
            pl.BlockSpec((D_MODEL, in_cols), const2),
            pl.BlockSpec((RANK_PAD, GLA_K), const2),
            pl.BlockSpec((1, GLA_K), const2),
            pl.BlockSpec(lb_logits.shape, const2),
            pl.BlockSpec((1, D_MODEL), const2),
            pl.BlockSpec((D_MODEL, D_MODEL), const2),
            pl.BlockSpec((1, D_MODEL), const2),
            pl.BlockSpec((N_LEVELS, CHUNK, CHUNK), lambda b, j: (0, 0, 0)),
        ],
        out_specs=[
            pl.BlockSpec((bb, tt, D_MODEL), lambda b, j: (b, j, 0)),
            pl.BlockSpec((bb, N_K_TILES, LANES, LANES), lambda b, j: (b, 0, 0, 0)),
        ],
        out_shape=[
            jax.ShapeDtypeStruct(x.shape, F32),
            jax.ShapeDtypeStruct((nb, N_K_TILES, LANES, LANES), F32),
        ],
        scratch_shapes=[
            pltpu.VMEM((rows, D_MODEL), BF16),
            pltpu.VMEM((rows, K_LANES), F32),
            pltpu.VMEM((rows, K_LANES), F32),
            pltpu.VMEM((rows, K_LANES), F32),
            pltpu.VMEM((rows, D_MODEL), BF16),
            pltpu.VMEM((rows, D_MODEL), F32),
            pltpu.VMEM((rows, D_MODEL), F32),
            pltpu.VMEM((CHUNK, LANES), F32),
        ],
        compiler_params=pltpu.CompilerParams(
            dimension_semantics=("arbitrary", "arbitrary"),
            vmem_limit_bytes=VMEM_LIMIT_BYTES,
        ),
        name="mixer_bb%d_tt%d" % (bb, tt),
    )(x, mods, st0, w_in, w_alpha, b_alpha, lb_logits, g_on, w_out, g_post, masks)


def _pack_states(s_gla, s_hgrn):
    nb = s_gla.shape[0]
    a = s_gla.reshape(nb, GLA_HEADS // 2, 2, GLA_DK, GLA_DV).transpose(0, 1, 4, 2, 3)
    a = a.reshape(nb, GLA_HEADS // 2, GLA_DV, LANES)
    return jnp.concatenate([a, jnp.swapaxes(s_hgrn, -1, -2)], axis=1)


def _unpack_states(st):
    nb = st.shape[0]
    n_gla = GLA_HEADS // 2
    a = st[:, :n_gla].reshape(nb, n_gla, GLA_DV, 2, GLA_DK).transpose(0, 1, 3, 4, 2)
    return a.reshape(nb, GLA_HEADS, GLA_DK, GLA_DV), jnp.swapaxes(st[:, n_gla:], -1, -2)


def kernel(x_prompt, x_sample, c_prompt, c_sample, state_gla, state_hgrn, w_ada, b_ada, g_pre, w_in,
           w_alpha, b_alpha, g_onorm_gla, hgrn_lb_logits, g_onorm_hgrn, w_out, g_post):
    assert w_ada.shape[0] == 1, "single-layer problem"
    bp = x_prompt.shape[0]

    o_qa, o_ka, o_va, o_za = 0, GLA_K, 2 * GLA_K, 2 * GLA_K + GLA_V
    o_al = o_za + GLA_V
    o_qh = o_al + GATE_RANK
    o_fh, o_ih, o_zh = o_qh + HGRN_W, o_qh + 2 * HGRN_W, o_qh + 3 * HGRN_W
    w = w_in[0]
    cols = lambda o, n: w[:, o:o + n]
    w_in_r = jnp.concatenate([
        cols(o_qa, GLA_K), cols(o_qh, HGRN_W), cols(o_ka, GLA_K), cols(o_fh, HGRN_W),
        cols(o_va, GLA_V), cols(o_ih, HGRN_W), cols(o_za, GLA_V), cols(o_zh, HGRN_W),
        cols(o_al, GATE_RANK), jnp.zeros((D_MODEL, RANK_PAD - GATE_RANK), w.dtype)], axis=1).astype(BF16)
    w_alpha_p = jnp.concatenate(
        [w_alpha[0], jnp.zeros((RANK_PAD - GATE_RANK, GLA_K), w_alpha.dtype)], axis=0).astype(BF16)
    g_on = jnp.concatenate([jnp.tile(g_onorm_gla[0], GLA_HEADS),
                            jnp.tile(g_onorm_hgrn[0], HGRN_HEADS)])[None, :]
    masks = jnp.asarray(_level_masks())

    c_all = jnp.concatenate([c_prompt, c_sample], axis=0)
    mods = _ada_call(c_all, w_ada[0], b_ada, g_pre).reshape(c_all.shape[0], 3, D_MODEL)

    shared = (w_in_r, w_alpha_p, b_alpha, hgrn_lb_logits, g_on, w_out[0].astype(BF16), g_post, masks)
    st0_p = jnp.zeros((bp, N_K_TILES, LANES, LANES), F32)
    yp, st_p = _mixer_call(x_prompt, mods[:bp], st0_p, *shared, bb=1, tt=ROWS_PER_STEP)
    st0_s = _pack_states(state_gla[0], state_hgrn[0])
    ys, st_s = _mixer_call(x_sample, mods[bp:], st0_s, *shared,
                           bb=ROWS_PER_STEP // x_sample.shape[1], tt=x_sample.shape[1])

    gla_p, hgrn_p = _unpack_states(st_p)
    gla_s, hgrn_s = _unpack_states(st_s)
    return (yp, ys, gla_p[None], hgrn_p[None], gla_s[None].astype(state_gla.dtype),
            hgrn_s[None].astype(state_hgrn.dtype))
```

```python
import functools
import math

import numpy as np
import jax
import jax.numpy as jnp
from jax import lax
from jax.experimental import pallas as pl
from jax.experimental.pallas import tpu as pltpu

D_MODEL = 1024
CHUNK = 64
GLA_HEADS = 4
GLA_V = D_MODEL // 2
GLA_K = GLA_V // 2
GLA_DK = GLA_K // GLA_HEADS
GLA_DV = GLA_V // GLA_HEADS
GATE_RANK = 16
GATE_TAU = 16.0
HGRN_HEADS = 4
HGRN_W = D_MODEL - GLA_V
HGRN_D = HGRN_W // HGRN_HEADS
EPS = 1e-6
LOG2E = math.log2(math.e)

LANES = 128
SUBLANES = 8
K_LANES = GLA_K + HGRN_W
N_K_TILES = K_LANES // LANES
N_GLA_PAIRS = GLA_HEADS // 2
N_HGRN_PAIRS = HGRN_HEADS // 2
N_PAIRS = N_GLA_PAIRS + N_HGRN_PAIRS
PAIR_V = 2 * LANES
RANK_PAD = LANES
HALF_SIZES = (32, 16, 8, 4, 2, 1)
N_LEVELS = len(HALF_SIZES) + 1
SLOT_STATE = N_LEVELS
N_SLOTS = N_LEVELS + 1
ROWS_PER_STEP = 256
VMEM_LIMIT_BYTES = 56 * 1024 * 1024

F32 = jnp.float32
BF16 = jnp.bfloat16
NT = (((1,), (1,)), ((), ()))
TN = (((0,), (0,)), ((), ()))


def _level_masks():
    t = np.arange(CHUNK)[:, None]
    s = np.arange(CHUNK)[None, :]
    masks = [(t // (2 * m)) == (s // (2 * m)) for m in HALF_SIZES]
    masks.append(t == s)
    masks = np.stack(masks).astype(np.float32)
    return np.concatenate([masks, masks], axis=2)


def _rms(x):
    return x * lax.rsqrt(jnp.mean(x * x, axis=-1, keepdims=True) + EPS)


def _sigmoid(x):
    return 1.0 / (1.0 + jnp.exp(-x))


def _log2_sigmoid(x):
    return jnp.minimum(x, 0.0) * LOG2E - jnp.log2(1.0 + jnp.exp(-jnp.abs(x)))


def _ada_kernel(c_ref, w_ref, b_ref, gpre_ref, out_ref):
    n = pl.program_id(0)
    val = jnp.dot(c_ref[...].astype(BF16), w_ref[...].astype(BF16),
                  preferred_element_type=F32) + b_ref[...]
    out_ref[...] = jnp.where(n == 1, gpre_ref[...] * (1.0 + val), val)


def _ada_call(c_all, w_ada, b_ada, g_pre):
    nb = c_all.shape[0]
    return pl.pallas_call(
        _ada_kernel,
        grid=(3,),
        in_specs=[
            pl.BlockSpec((nb, D_MODEL), lambda n: (0, 0)),
            pl.BlockSpec((D_MODEL, D_MODEL), lambda n: (0, n)),
            pl.BlockSpec((1, D_MODEL), lambda n: (0, n)),
            pl.BlockSpec((1, D_MODEL), lambda n: (0, 0)),
        ],
        out_specs=pl.BlockSpec((nb, D_MODEL), lambda n: (0, n)),
        out_shape=jax.ShapeDtypeStruct((nb, 3 * D_MODEL), F32),
        name="ada_mod",
    )(c_all, w_ada, b_ada, g_pre)


def _ref_rows(b_ref, block, off):
    pieces = []
    if block >= SUBLANES:
        for j in range(CHUNK // block):
            row = b_ref[j * block + off:j * block + off + 1, :]
            pieces.append(jnp.broadcast_to(row, (block, LANES)))
    else:
        assert block == SUBLANES // 2
        sub = lax.broadcasted_iota(jnp.int32, (SUBLANES, LANES), 0)
        for i in range(CHUNK // SUBLANES):
            lo = b_ref[SUBLANES * i + off:SUBLANES * i + off + 1, :]
            hi = b_ref[SUBLANES * i + block + off:SUBLANES * i + block + off + 1, :]
            pieces.append(jnp.where(sub < block,
                                    jnp.broadcast_to(lo, (SUBLANES, LANES)),
                                    jnp.broadcast_to(hi, (SUBLANES, LANES))))
    return jnp.concatenate(pieces, axis=0)


def _mixer_kernel(x_ref, mods_ref, stg0_ref, sth0_ref, win_ref, walpha_ref, balpha_ref, lbl_ref,
                  gon_ref, wout_ref, gpost_ref, masks_ref,
                  y_ref, stg_ref, sth_ref,
                  hb_s, q_s, k_s, g_s, vbd_s, zs_s, o_s, b_s, lhs_s, ktg_s, kth_s, ksg_s, ksh_s, dec_s,
                  *, bb, tt):
    rows = bb * tt
    n_chunks = rows // CHUNK
    chunks_per_seq = tt // CHUNK

    @pl.when((pl.program_id(0) == 0) & (pl.program_id(1) == 0))
    def _():
        vbd_s[...] = jnp.zeros(vbd_s.shape, BF16)
        kth_s[...] = jnp.zeros(kth_s.shape, BF16)

    @pl.when(pl.program_id(1) == 0)
    def _():
        stg_ref[...] = stg0_ref[...]
        sth_ref[...] = sth0_ref[...]

    def mod_rows(i):
        if bb == 1:
            return mods_ref[0, i:i + 1, :]
        return jnp.concatenate(
            [jnp.broadcast_to(mods_ref[bi, i:i + 1, :], (tt, D_MODEL)) for bi in range(bb)], axis=0)

    x = x_ref[...].reshape(rows, D_MODEL)
    hb_s[...] = (_rms(x) * mod_rows(1) + mod_rows(0)).astype(BF16)

    def proj(c0, c1):
        return jnp.dot(hb_s[...], win_ref[:, c0:c1], preferred_element_type=F32)

    c = 0
    pq = proj(c, c + K_LANES)
    c += K_LANES
    q_s[:, :GLA_K] = pq[:, :GLA_K] * (GLA_DK ** -0.5)
    qh = pq[:, GLA_K:]
    q_s[:, GLA_K:] = qh * _sigmoid(qh)

    pk = proj(c, c + K_LANES)
    c += K_LANES
    k_s[:, :GLA_K] = pk[:, :GLA_K]
    lbl = lbl_ref[...]
    lmax = jnp.max(lbl, axis=0, keepdims=True)
    lexp = jnp.exp(lbl - lmax)
    lb = lexp[0:1, :] / jnp.sum(lexp, axis=0, keepdims=True)
    f = lb + (1.0 - lb) * _sigmoid(pk[:, GLA_K:])
    k_s[:, GLA_K:] = 1.0 - f
    g_s[:, GLA_K:] = jnp.log2(f)

    pv = proj(c, c + D_MODEL).astype(BF16)
    c += D_MODEL
    for ci in range(n_chunks):
        r = slice(ci * CHUNK, (ci + 1) * CHUNK)
        for p in range(N_PAIRS):
            vbd_s[p, ci, :CHUNK, :LANES] = pv[r, p * PAIR_V:p * PAIR_V + LANES]
            vbd_s[p, ci, CHUNK:, LANES:] = pv[r, p * PAIR_V + LANES:(p + 1) * PAIR_V]
    pz = proj(c, c + D_MODEL)
    c += D_MODEL
    zs_s[...] = pz * _sigmoid(pz)
    a_lr = proj(c, c + RANK_PAD)
    ga = jnp.dot(a_lr.astype(BF16), walpha_ref[...], preferred_element_type=F32) + balpha_ref[...]
    g_s[:, :GLA_K] = _log2_sigmoid(ga) * (1.0 / GATE_TAU)

    row_i = lax.broadcasted_iota(jnp.int32, (CHUNK, LANES), 0)
    lane_i = lax.broadcasted_iota(jnp.int32, (CHUNK, LANES), 1)
    head0 = lane_i < GLA_DK

    def tile_base(ci, lt):
        r = slice(ci * CHUNK, (ci + 1) * CHUNK)
        lanes = slice(lt * LANES, (lt + 1) * LANES)
        b_t = b_s.at[ci, lt]
        g = g_s[r, lanes]
        q = q_s[r, lanes]
        k = k_s[r, lanes]
        b = g
        for sh in (1, 2, 4):
            b = b + jnp.where(row_i >= sh, pltpu.roll(b, sh, 0), 0.0)
        for sh in (8, 16, 32):
            b = b + jnp.concatenate([jnp.zeros((sh, LANES), F32), b[:CHUNK - sh]], axis=0)
        b_t[...] = b
        b_last = b_t[CHUNK - 1:CHUNK, :]
        dec_s[ci * SUBLANES:ci * SUBLANES + 1, lanes] = jnp.exp2(b_last)
        lhs_s[SLOT_STATE, r, lanes] = (q * jnp.exp2(b)).astype(BF16)
        k_state = k * jnp.exp2(b_last - b)
        if lt < N_GLA_PAIRS:
            ksg_s[lt, ci, :CHUNK, :] = jnp.where(head0, k_state, 0.0).astype(BF16)
            ksg_s[lt, ci, CHUNK:, :] = jnp.where(head0, 0.0, k_state).astype(BF16)
        else:
            ksh_s[lt - N_GLA_PAIRS, ci] = k_state.astype(BF16)
        return q, k, g, b, b_t

    def tile_level(base, lv):
        q, k, g, b, b_t = base
        if lv == N_LEVELS - 1:
            return q, k
        m = HALF_SIZES[lv]
        upper = (row_i & (2 * m - 1)) >= m
        if m == 1:
            lhs, rhs = q * jnp.exp2(g), k
        else:
            e = jnp.exp2(-jnp.abs(b - _ref_rows(b_t, 2 * m, m - 1)))
            lhs, rhs = q * e, k * e
        return jnp.where(upper, lhs, 0.0), jnp.where(upper, 0.0, rhs)

    def prep_pair(ci, p):
        r = slice(ci * CHUNK, (ci + 1) * CHUNK)
        tiles = (p,) if p < N_GLA_PAIRS else tuple(
            N_GLA_PAIRS + 2 * (p - N_GLA_PAIRS) + pos for pos in range(2))
        bases = [tile_base(ci, lt) for lt in tiles]
        for lv in range(N_LEVELS):
            rhs_tiles = []
            for lt, base in zip(tiles, bases):
                lhs, rhs = tile_level(base, lv)
                lhs_s[lv, r, lt * LANES:(lt + 1) * LANES] = lhs.astype(BF16)
                rhs_tiles.append(rhs)
            if p < N_GLA_PAIRS:
                rhs = rhs_tiles[0]
                stacked = jnp.concatenate([jnp.where(head0, rhs, 0.0), jnp.where(head0, 0.0, rhs)], axis=0)
                ktg_s[lv, p, ci] = stacked.astype(BF16).T
            else:
                kt = jnp.concatenate(rhs_tiles, axis=0).astype(BF16).T
                ph = p - N_GLA_PAIRS
                kth_s[lv, ph, ci, :LANES, :CHUNK] = kt[:, :CHUNK]
                kth_s[lv, ph, ci, LANES:, CHUNK:] = kt[:, CHUNK:]

    def attention(ci, lhs_lanes, key_t):
        r = slice(ci * CHUNK, (ci + 1) * CHUNK)
        att = None
        for lv in range(N_LEVELS):
            p = jnp.dot(lhs_s[lv, r, lhs_lanes], key_t(lv), preferred_element_type=F32)
            if lv > 0:
                p = p * masks_ref[lv]
            att = p if att is None else att + p
        return att.astype(BF16)

    states = {}

    def load_state(key, ref, idx):
        if key not in states:
            states[key] = ref[idx]
        return states[key]

    for ci in range(n_chunks):
        bi, cseq = divmod(ci, chunks_per_seq)
        r = slice(ci * CHUNK, (ci + 1) * CHUNK)
        last_of_seq = cseq == chunks_per_seq - 1
        for p in range(N_PAIRS):
            prep_pair(ci, p)
        dec = dec_s[ci * SUBLANES:ci * SUBLANES + 1, :]

        for p in range(N_GLA_PAIRS):
            lanes = slice(p * LANES, (p + 1) * LANES)
            vbd = vbd_s[p, ci]
            att = attention(ci, lanes, lambda lv: ktg_s[lv, p, ci])
            st = load_state(("g", bi, p), stg_ref, (bi, p))
            o = jnp.dot(att, vbd, preferred_element_type=F32)
            o = o + lax.dot_general(lhs_s[SLOT_STATE, r, lanes], st.astype(BF16), NT,
                                    preferred_element_type=F32)
            o_s[r, p * PAIR_V:(p + 1) * PAIR_V] = o
            st = st * dec[:, lanes] + lax.dot_general(vbd, ksg_s[p, ci], TN, preferred_element_type=F32)
            states[("g", bi, p)] = st
            if last_of_seq:
                stg_ref[bi, p] = st

        for p in range(N_HGRN_PAIRS):
            pg = N_GLA_PAIRS + p
            lanes = slice(GLA_K + p * PAIR_V, GLA_K + (p + 1) * PAIR_V)
            att = attention(ci, lanes, lambda lv: kth_s[lv, p, ci])
            o = jnp.dot(att, vbd_s[pg, ci], preferred_element_type=F32)
            for pos in range(2):
                hh = 2 * p + pos
                hl = slice(GLA_K + hh * LANES, GLA_K + (hh + 1) * LANES)
                st = load_state(("h", bi, hh), sth_ref, (bi, hh))
                o_s[r, pg * PAIR_V + pos * LANES:pg * PAIR_V + (pos + 1) * LANES] = (
                    o[:, pos * LANES:(pos + 1) * LANES]
                    + lax.dot_general(lhs_s[SLOT_STATE, r, hl], st.astype(BF16), NT,
                                      preferred_element_type=F32))
                v_head = vbd_s[pg, ci, pos * CHUNK:(pos + 1) * CHUNK, pos * LANES:(pos + 1) * LANES]
                st = st * dec[:, hl] + lax.dot_general(v_head, ksh_s[hh, ci], TN, preferred_element_type=F32)
                states[("h", bi, hh)] = st
                if last_of_seq:
                    sth_ref[bi, hh] = st

    gated = []
    for ht in range(GLA_HEADS + HGRN_HEADS):
        lanes = slice(ht * LANES, (ht + 1) * LANES)
        gated.append((_rms(o_s[:, lanes]) * gon_ref[:, lanes] * zs_s[:, lanes]).astype(BF16))
    proj_out = jnp.dot(jnp.concatenate(gated, axis=1), wout_ref[...], preferred_element_type=F32)
    y = x_ref[...].reshape(rows, D_MODEL) + mod_rows(2) * (_rms(proj_out) * gpost_ref[...])
    y_ref[...] = y.reshape(bb, tt, D_MODEL)


def _mixer_call(x, mods, stg0, sth0, w_in, w_alpha, b_alpha, lb_logits, g_on, w_out, g_post, masks, *, bb, tt):
    nb, seq, _ = x.shape
    rows = bb * tt
    assert nb % bb == 0 and seq % tt == 0 and tt % CHUNK == 0
    n_chunks = rows // CHUNK
    in_cols = w_in.shape[1]
    const2 = lambda b, j: (0, 0)
    per_seq4 = lambda b, j: (b, 0, 0, 0)
    kernel = functools.partial(_mixer_kernel, bb=bb, tt=tt)
    stg_shape = (nb, N_GLA_PAIRS, PAIR_V, LANES)
    sth_shape = (nb, HGRN_HEADS, LANES, LANES)
    return pl.pallas_call(
        kernel,
        grid=(nb // bb, seq // tt),
        in_specs=[
            pl.BlockSpec((bb, tt, D_MODEL), lambda b, j: (b, j, 0)),
            pl.BlockSpec((bb, 3, D_MODEL), lambda b, j: (b, 0, 0)),
            pl.BlockSpec((bb,) + stg_shape[1:], per_seq4),
            pl.BlockSpec((bb,) + sth_shape[1:], per_seq4),
            pl.BlockSpec((D_MODEL, in_cols), const2),
            pl.BlockSpec((RANK_PAD, GLA_K), const2),
            pl.BlockSpec((1, GLA_K), const2),
            pl.BlockSpec(lb_logits.shape, const2),
            pl.BlockSpec((1, D_MODEL), const2),
            pl.BlockSpec((D_MODEL, D_MODEL), const2),
            pl.BlockSpec((1, D_MODEL), const2),
            pl.BlockSpec(masks.shape, lambda b, j: (0, 0, 0)),
        ],
        out_specs=[
            pl.BlockSpec((bb, tt, D_MODEL), lambda b, j: (b, j, 0)),
            pl.BlockSpec((bb,) + stg_shape[1:], per_seq4),
            pl.BlockSpec((bb,) + sth_shape[1:], per_seq4),
        ],
        out_shape=[
            jax.ShapeDtypeStruct(x.shape, F32),
            jax.ShapeDtypeStruct(stg_shape, F32),
            jax.ShapeDtypeStruct(sth_shape, F32),
        ],
        scratch_shapes=[
            pltpu.VMEM((rows, D_MODEL), BF16),
            pltpu.VMEM((rows, K_LANES), F32),
            pltpu.VMEM((rows, K_LANES), F32),
            pltpu.VMEM((rows, K_LANES), F32),
            pltpu.VMEM((N_PAIRS, n_chunks, 2 * CHUNK, PAIR_V), BF16),
            pltpu.VMEM((rows, D_MODEL), F32),
            pltpu.VMEM((rows, D_MODEL), F32),
            pltpu.VMEM((n_chunks, N_K_TILES, CHUNK, LANES), F32),
            pltpu.VMEM((N_SLOTS, rows, K_LANES), BF16),
            pltpu.VMEM((N_LEVELS, N_GLA_PAIRS, n_chunks, LANES, 2 * CHUNK), BF16),
            pltpu.VMEM((N_LEVELS, N_HGRN_PAIRS, n_chunks, PAIR_V, 2 * CHUNK), BF16),
            pltpu.VMEM((N_GLA_PAIRS, n_chunks, 2 * CHUNK, LANES), BF16),
            pltpu.VMEM((HGRN_HEADS, n_chunks, CHUNK, LANES), BF16),
            pltpu.VMEM((n_chunks * SUBLANES, K_LANES), F32),
        ],
        compiler_params=pltpu.CompilerParams(
            dimension_semantics=("arbitrary", "arbitrary"),
            vmem_limit_bytes=VMEM_LIMIT_BYTES,
        ),
        name="mixer_bb%d_tt%d" % (bb, tt),
    )(x, mods, stg0, sth0, w_in, w_alpha, b_alpha, lb_logits, g_on, w_out, g_post, masks)


def _pack_states(s_gla, s_hgrn):
    nb = s_gla.shape[0]
    a = jnp.swapaxes(s_gla, -1, -2).reshape(nb, N_GLA_PAIRS, 2, GLA_DV, GLA_DK)
    z = jnp.zeros_like(a[:, :, 0])
    a = jnp.concatenate([jnp.concatenate([a[:, :, 0], z], axis=-1),
                         jnp.concatenate([z, a[:, :, 1]], axis=-1)], axis=2)
    return a, jnp.swapaxes(s_hgrn, -1, -2)


def _unpack_states(stg, sth):
    nb = stg.shape[0]
    heads = [stg[:, :, h * GLA_DV:(h + 1) * GLA_DV, h * GLA_DK:(h + 1) * GLA_DK] for h in range(2)]
    a = jnp.swapaxes(jnp.stack(heads, axis=2), -1, -2)
    return a.reshape(nb, GLA_HEADS, GLA_DK, GLA_DV), jnp.swapaxes(sth, -1, -2)


def kernel(x_prompt, x_sample, c_prompt, c_sample, state_gla, state_hgrn, w_ada, b_ada, g_pre, w_in,
           w_alpha, b_alpha, g_onorm_gla, hgrn_lb_logits, g_onorm_hgrn, w_out, g_post):
    assert w_ada.shape[0] == 1, "single-layer problem"
    bp = x_prompt.shape[0]

    o_qa, o_ka, o_va, o_za = 0, GLA_K, 2 * GLA_K, 2 * GLA_K + GLA_V
    o_al = o_za + GLA_V
    o_qh = o_al + GATE_RANK
    o_fh, o_ih, o_zh = o_qh + HGRN_W, o_qh + 2 * HGRN_W, o_qh + 3 * HGRN_W
    w = w_in[0]
    cols = lambda o, n: w[:, o:o + n]
    w_in_r = jnp.concatenate([
        cols(o_qa, GLA_K), cols(o_qh, HGRN_W), cols(o_ka, GLA_K), cols(o_fh, HGRN_W),
        cols(o_va, GLA_V), cols(o_ih, HGRN_W), cols(o_za, GLA_V), cols(o_zh, HGRN_W),
        cols(o_al, GATE_RANK), jnp.zeros((D_MODEL, RANK_PAD - GATE_RANK), w.dtype)], axis=1).astype(BF16)
    w_alpha_p = jnp.concatenate(
        [w_alpha[0], jnp.zeros((RANK_PAD - GATE_RANK, GLA_K), w_alpha.dtype)], axis=0).astype(BF16)
    g_on = jnp.concatenate([jnp.tile(g_onorm_gla[0], GLA_HEADS),
                            jnp.tile(g_onorm_hgrn[0], HGRN_HEADS)])[None, :]
    masks = jnp.asarray(_level_masks())

    c_all = jnp.concatenate([c_prompt, c_sample], axis=0)
    mods = _ada_call(c_all, w_ada[0], b_ada, g_pre).reshape(c_all.shape[0], 3, D_MODEL)

    shared = (w_in_r, w_alpha_p, b_alpha, hgrn_lb_logits, g_on, w_out[0].astype(BF16), g_post, masks)
    stg0_p = jnp.zeros((bp, N_GLA_PAIRS, PAIR_V, LANES), F32)
    sth0_p = jnp.zeros((bp, HGRN_HEADS, LANES, LANES), F32)
    yp, stg_p, sth_p = _mixer_call(x_prompt, mods[:bp], stg0_p, sth0_p, *shared, bb=1, tt=ROWS_PER_STEP)
    stg0_s, sth0_s = _pack_states(state_gla[0], state_hgrn[0])
    ys, stg_s, sth_s = _mixer_call(x_sample, mods[bp:], stg0_s, sth0_s, *shared,
                                   bb=ROWS_PER_STEP // x_sample.shape[1], tt=x_sample.shape[1])

    gla_p, hgrn_p = _unpack_states(stg_p, sth_p)
    gla_s, hgrn_s = _unpack_states(stg_s, sth_s)
    return (yp, ys, gla_p[None], hgrn_p[None], gla_s[None].astype(state_gla.dtype),
            hgrn_s[None].astype(state_hgrn.dtype))
```

```python
import functools
import math

import numpy as np
import jax
import jax.numpy as jnp
from jax import lax
from jax.experimental import pallas as pl
from jax.experimental.pallas import tpu as pltpu

D_MODEL = 1024
CHUNK = 64
GLA_HEADS = 4
GLA_V = D_MODEL // 2
GLA_K = GLA_V // 2
GLA_DK = GLA_K // GLA_HEADS
GLA_DV = GLA_V // GLA_HEADS
GATE_RANK = 16
GATE_TAU = 16.0
HGRN_HEADS = 4
HGRN_W = D_MODEL - GLA_V
HGRN_D = HGRN_W // HGRN_HEADS
EPS = 1e-6
LOG2E = math.log2(math.e)

LANES = 128
SUBLANES = 8
K_LANES = GLA_K + HGRN_W
N_K_TILES = K_LANES // LANES
N_GLA_PAIRS = GLA_HEADS // 2
N_HGRN_PAIRS = HGRN_HEADS // 2
N_PAIRS = N_GLA_PAIRS + N_HGRN_PAIRS
PAIR_V = 2 * LANES
RANK_PAD = LANES
HALF_SIZES = (32, 16, 8, 4, 2, 1)
N_LEVELS = len(HALF_SIZES) + 1
SLOT_STATE = N_LEVELS
N_SLOTS = N_LEVELS + 1
MASK_CAUSAL = N_LEVELS
SINGLE_ANCHOR_MAX_LOG2 = 90.0
ROWS_PER_STEP = 256
VMEM_LIMIT_BYTES = 56 * 1024 * 1024

F32 = jnp.float32
BF16 = jnp.bfloat16
NT = (((1,), (1,)), ((), ()))
TN = (((0,), (0,)), ((), ()))


def _level_masks():
    t = np.arange(CHUNK)[:, None]
    s = np.arange(CHUNK)[None, :]
    masks = [(t // (2 * m)) == (s // (2 * m)) for m in HALF_SIZES]
    masks.append(t == s)
    masks.append(t >= s)
    masks = np.stack(masks).astype(np.float32)
    return np.concatenate([masks, masks], axis=2)


def _rms(x):
    return x * lax.rsqrt(jnp.mean(x * x, axis=-1, keepdims=True) + EPS)


def _sigmoid(x):
    return 1.0 / (1.0 + jnp.exp(-x))


def _log2_sigmoid(x):
    return jnp.minimum(x, 0.0) * LOG2E - jnp.log2(1.0 + jnp.exp(-jnp.abs(x)))


def _ada_kernel(c_ref, w_ref, b_ref, gpre_ref, out_ref):
    n = pl.program_id(0)
    val = jnp.dot(c_ref[...].astype(BF16), w_ref[...].astype(BF16),
                  preferred_element_type=F32) + b_ref[...]
    out_ref[...] = jnp.where(n == 1, gpre_ref[...] * (1.0 + val), val)


def _ada_call(c_all, w_ada, b_ada, g_pre):
    nb = c_all.shape[0]
    return pl.pallas_call(
        _ada_kernel,
        grid=(3,),
        in_specs=[
            pl.BlockSpec((nb, D_MODEL), lambda n: (0, 0)),
            pl.BlockSpec((D_MODEL, D_MODEL), lambda n: (0, n)),
            pl.BlockSpec((1, D_MODEL), lambda n: (0, n)),
            pl.BlockSpec((1, D_MODEL), lambda n: (0, 0)),
        ],
        out_specs=pl.BlockSpec((nb, D_MODEL), lambda n: (0, n)),
        out_shape=jax.ShapeDtypeStruct((nb, 3 * D_MODEL), F32),
        name="ada_mod",
    )(c_all, w_ada, b_ada, g_pre)


def _ref_rows(b_ref, block, off):
    pieces = []
    if block >= SUBLANES:
        for j in range(CHUNK // block):
            row = b_ref[j * block + off:j * block + off + 1, :]
            pieces.append(jnp.broadcast_to(row, (block, LANES)))
    else:
        assert block == SUBLANES // 2
        sub = lax.broadcasted_iota(jnp.int32, (SUBLANES, LANES), 0)
        for i in range(CHUNK // SUBLANES):
            lo = b_ref[SUBLANES * i + off:SUBLANES * i + off + 1, :]
            hi = b_ref[SUBLANES * i + block + off:SUBLANES * i + block + off + 1, :]
            pieces.append(jnp.where(sub < block,
                                    jnp.broadcast_to(lo, (SUBLANES, LANES)),
                                    jnp.broadcast_to(hi, (SUBLANES, LANES))))
    return jnp.concatenate(pieces, axis=0)


def _mixer_kernel(x_ref, mods_ref, stg0_ref, sth0_ref, win_ref, walpha_ref, balpha_ref, lbl_ref,
                  gon_ref, wout_ref, gpost_ref, masks_ref,
                  y_ref, stg_ref, sth_ref,
                  hb_s, q_s, k_s, g_s, vbd_s, zs_s, o_s, b_s, lhs_s, ktg_s, kth_s, ksg_s, ksh_s, dec_s,
                  *, bb, tt):
    rows = bb * tt
    n_chunks = rows // CHUNK
    chunks_per_seq = tt // CHUNK

    @pl.when((pl.program_id(0) == 0) & (pl.program_id(1) == 0))
    def _():
        vbd_s[...] = jnp.zeros(vbd_s.shape, BF16)
        kth_s[...] = jnp.zeros(kth_s.shape, BF16)

    @pl.when(pl.program_id(1) == 0)
    def _():
        stg_ref[...] = stg0_ref[...]
        sth_ref[...] = sth0_ref[...]

    def mod_rows(i):
        if bb == 1:
            return mods_ref[0, i:i + 1, :]
        return jnp.concatenate(
            [jnp.broadcast_to(mods_ref[bi, i:i + 1, :], (tt, D_MODEL)) for bi in range(bb)], axis=0)

    x = x_ref[...].reshape(rows, D_MODEL)
    hb_s[...] = (_rms(x) * mod_rows(1) + mod_rows(0)).astype(BF16)

    def proj(c0, c1):
        return jnp.dot(hb_s[...], win_ref[:, c0:c1], preferred_element_type=F32)

    c = 0
    pq = proj(c, c + K_LANES)
    c += K_LANES
    q_s[:, :GLA_K] = pq[:, :GLA_K] * (GLA_DK ** -0.5)
    qh = pq[:, GLA_K:]
    q_s[:, GLA_K:] = qh * _sigmoid(qh)

    pk = proj(c, c + K_LANES)
    c += K_LANES
    k_s[:, :GLA_K] = pk[:, :GLA_K]
    lbl = lbl_ref[...]
    lmax = jnp.max(lbl, axis=0, keepdims=True)
    lexp = jnp.exp(lbl - lmax)
    lb = lexp[0:1, :] / jnp.sum(lexp, axis=0, keepdims=True)
    f = lb + (1.0 - lb) * _sigmoid(pk[:, GLA_K:])
    k_s[:, GLA_K:] = 1.0 - f
    g_s[:, GLA_K:] = jnp.log2(f)

    pv = proj(c, c + D_MODEL).astype(BF16)
    c += D_MODEL
    for ci in range(n_chunks):
        r = slice(ci * CHUNK, (ci + 1) * CHUNK)
        for p in range(N_PAIRS):
            vbd_s[p, ci, :CHUNK, :LANES] = pv[r, p * PAIR_V:p * PAIR_V + LANES]
            vbd_s[p, ci, CHUNK:, LANES:] = pv[r, p * PAIR_V + LANES:(p + 1) * PAIR_V]
    pz = proj(c, c + D_MODEL)
    c += D_MODEL
    zs_s[...] = pz * _sigmoid(pz)
    a_lr = proj(c, c + RANK_PAD)
    ga = jnp.dot(a_lr.astype(BF16), walpha_ref[...], preferred_element_type=F32) + balpha_ref[...]
    g_s[:, :GLA_K] = _log2_sigmoid(ga) * (1.0 / GATE_TAU)

    row_i = lax.broadcasted_iota(jnp.int32, (CHUNK, LANES), 0)
    lane_i = lax.broadcasted_iota(jnp.int32, (CHUNK, LANES), 1)
    head0 = lane_i < GLA_DK

    def tile_base(ci, lt):
        r = slice(ci * CHUNK, (ci + 1) * CHUNK)
        lanes = slice(lt * LANES, (lt + 1) * LANES)
        b_t = b_s.at[ci, lt]
        g = g_s[r, lanes]
        q = q_s[r, lanes]
        k = k_s[r, lanes]
        b = g
        for sh in (1, 2, 4):
            b = b + jnp.where(row_i >= sh, pltpu.roll(b, sh, 0), 0.0)
        for sh in (8, 16, 32):
            b = b + jnp.concatenate([jnp.zeros((sh, LANES), F32), b[:CHUNK - sh]], axis=0)
        b_t[...] = b
        b_last = b_t[CHUNK - 1:CHUNK, :]
        dec_s[ci * SUBLANES:ci * SUBLANES + 1, lanes] = jnp.exp2(b_last)
        lhs_s[SLOT_STATE, r, lanes] = (q * jnp.exp2(b)).astype(BF16)
        k_state = k * jnp.exp2(b_last - b)
        if lt < N_GLA_PAIRS:
            ksg_s[lt, ci, :CHUNK, :] = jnp.where(head0, k_state, 0.0).astype(BF16)
            ksg_s[lt, ci, CHUNK:, :] = jnp.where(head0, 0.0, k_state).astype(BF16)
        else:
            ksh_s[lt - N_GLA_PAIRS, ci] = k_state.astype(BF16)
        return q, k, g, b, b_t

    def tile_level(base, lv):
        q, k, g, b, b_t = base
        if lv == N_LEVELS - 1:
            return q, k
        m = HALF_SIZES[lv]
        upper = (row_i & (2 * m - 1)) >= m
        if m == 1:
            lhs, rhs = q * jnp.exp2(g), k
        else:
            e = jnp.exp2(-jnp.abs(b - _ref_rows(b_t, 2 * m, m - 1)))
            lhs, rhs = q * e, k * e
        return jnp.where(upper, lhs, 0.0), jnp.where(upper, 0.0, rhs)

    def store_keys_t(lv, ci, p, rhs_tiles):
        if p < N_GLA_PAIRS:
            rhs = rhs_tiles[0]
            stacked = jnp.concatenate([jnp.where(head0, rhs, 0.0), jnp.where(head0, 0.0, rhs)], axis=0)
            ktg_s[lv, p, ci] = stacked.astype(BF16).T
        else:
            kt = jnp.concatenate(rhs_tiles, axis=0).astype(BF16).T
            ph = p - N_GLA_PAIRS
            kth_s[lv, ph, ci, :LANES, :CHUNK] = kt[:, :CHUNK]
            kth_s[lv, ph, ci, LANES:, CHUNK:] = kt[:, CHUNK:]

    def prep_pair(ci, p, single_anchor):
        r = slice(ci * CHUNK, (ci + 1) * CHUNK)
        tiles = (p,) if p < N_GLA_PAIRS else tuple(
            N_GLA_PAIRS + 2 * (p - N_GLA_PAIRS) + pos for pos in range(2))
        bases = [tile_base(ci, lt) for lt in tiles]
        if single_anchor:
            store_keys_t(0, ci, p, [k * jnp.exp2(-b) for (_, k, _, b, _) in bases])
            return
        for lv in range(N_LEVELS):
            rhs_tiles = []
            for lt, base in zip(tiles, bases):
                lhs, rhs = tile_level(base, lv)
                lhs_s[lv, r, lt * LANES:(lt + 1) * LANES] = lhs.astype(BF16)
                rhs_tiles.append(rhs)
            store_keys_t(lv, ci, p, rhs_tiles)

    def attention(ci, lhs_lanes, key_t, single_anchor):
        r = slice(ci * CHUNK, (ci + 1) * CHUNK)
        if single_anchor:
            p = jnp.dot(lhs_s[SLOT_STATE, r, lhs_lanes], key_t(0), preferred_element_type=F32)
            return jnp.where(masks_ref[MASK_CAUSAL] > 0.0, p, 0.0).astype(BF16)
        att = None
        for lv in range(N_LEVELS):
            p = jnp.dot(lhs_s[lv, r, lhs_lanes], key_t(lv), preferred_element_type=F32)
            if lv > 0:
                p = p * masks_ref[lv]
            att = p if att is None else att + p
        return att.astype(BF16)

    def recurrences(single_anchor):
        states = {}

        def load_state(key, ref, idx):
            if key not in states:
                states[key] = ref[idx]
            return states[key]

        for ci in range(n_chunks):
            bi, cseq = divmod(ci, chunks_per_seq)
            r = slice(ci * CHUNK, (ci + 1) * CHUNK)
            last_of_seq = cseq == chunks_per_seq - 1
            for p in range(N_PAIRS):
                prep_pair(ci, p, single_anchor)
            dec = dec_s[ci * SUBLANES:ci * SUBLANES + 1, :]

            for p in range(N_GLA_PAIRS):
                lanes = slice(p * LANES, (p + 1) * LANES)
                vbd = vbd_s[p, ci]
                att = attention(ci, lanes, lambda lv: ktg_s[lv, p, ci], single_anchor)
                st = load_state(("g", bi, p), stg_ref, (bi, p))
                o = jnp.dot(att, vbd, preferred_element_type=F32)
                o = o + lax.dot_general(lhs_s[SLOT_STATE, r, lanes], st.astype(BF16), NT,
                                        preferred_element_type=F32)
                o_s[r, p * PAIR_V:(p + 1) * PAIR_V] = o
                st = st * dec[:, lanes] + lax.dot_general(vbd, ksg_s[p, ci], TN, preferred_element_type=F32)
                states[("g", bi, p)] = st
                if last_of_seq:
                    stg_ref[bi, p] = st

            for p in range(N_HGRN_PAIRS):
                pg = N_GLA_PAIRS + p
                lanes = slice(GLA_K + p * PAIR_V, GLA_K + (p + 1) * PAIR_V)
                att = attention(ci, lanes, lambda lv: kth_s[lv, p, ci], single_anchor)
                o = jnp.dot(att, vbd_s[pg, ci], preferred_element_type=F32)
                for pos in range(2):
                    hh = 2 * p + pos
                    hl = slice(GLA_K + hh * LANES, GLA_K + (hh + 1) * LANES)
                    st = load_state(("h", bi, hh), sth_ref, (bi, hh))
                    o_s[r, pg * PAIR_V + pos * LANES:pg * PAIR_V + (pos + 1) * LANES] = (
                        o[:, pos * LANES:(pos + 1) * LANES]
                        + lax.dot_general(lhs_s[SLOT_STATE, r, hl], st.astype(BF16), NT,
                                          preferred_element_type=F32))
                    v_head = vbd_s[pg, ci, pos * CHUNK:(pos + 1) * CHUNK, pos * LANES:(pos + 1) * LANES]
                    st = st * dec[:, hl] + lax.dot_general(v_head, ksh_s[hh, ci], TN,
                                                           preferred_element_type=F32)
                    states[("h", bi, hh)] = st
                    if last_of_seq:
                        sth_ref[bi, hh] = st

    total = None
    for ci in range(n_chunks):
        t = jnp.sum(g_s[ci * CHUNK:(ci + 1) * CHUNK, :], axis=0, keepdims=True)
        total = t if total is None else jnp.minimum(total, t)
    mild = jnp.min(total) >= -SINGLE_ANCHOR_MAX_LOG2

    @pl.when(mild)
    def _():
        recurrences(True)

    @pl.when(jnp.logical_not(mild))
    def _():
        recurrences(False)

    gated = []
    for ht in range(GLA_HEADS + HGRN_HEADS):
        lanes = slice(ht * LANES, (ht + 1) * LANES)
        gated.append((_rms(o_s[:, lanes]) * gon_ref[:, lanes] * zs_s[:, lanes]).astype(BF16))
    proj_out = jnp.dot(jnp.concatenate(gated, axis=1), wout_ref[...], preferred_element_type=F32)
    y = x_ref[...].reshape(rows, D_MODEL) + mod_rows(2) * (_rms(proj_out) * gpost_ref[...])
    y_ref[...] = y.reshape(bb, tt, D_MODEL)


def _mixer_call(x, mods, stg0, sth0, w_in, w_alpha, b_alpha, lb_logits, g_on, w_out, g_post, masks, *, bb, tt):
    nb, seq, _ = x.shape
    rows = bb * tt
    assert nb % bb == 0 and seq % tt == 0 and tt % CHUNK == 0
    n_chunks = rows // CHUNK
    in_cols = w_in.shape[1]
    const2 = lambda b, j: (0, 0)
    per_seq4 = lambda b, j: (b, 0, 0, 0)
    kernel = functools.partial(_mixer_kernel, bb=bb, tt=tt)
    stg_shape = (nb, N_GLA_PAIRS, PAIR_V, LANES)
    sth_shape = (nb, HGRN_HEADS, LANES, LANES)
    return pl.pallas_call(
        kernel,
        grid=(nb // bb, seq // tt),
        in_specs=[
            pl.BlockSpec((bb, tt, D_MODEL), lambda b, j: (b, j, 0)),
            pl.BlockSpec((bb, 3, D_MODEL), lambda b, j: (b, 0, 0)),
            pl.BlockSpec((bb,) + stg_shape[1:], per_seq4),
            pl.BlockSpec((bb,) + sth_shape[1:], per_seq4),
            pl.BlockSpec((D_MODEL, in_cols), const2),
            pl.BlockSpec((RANK_PAD, GLA_K), const2),
            pl.BlockSpec((1, GLA_K), const2),
            pl.BlockSpec(lb_logits.shape, const2),
            pl.BlockSpec((1, D_MODEL), const2),
            pl.BlockSpec((D_MODEL, D_MODEL), const2),
            pl.BlockSpec((1, D_MODEL), const2),
            pl.BlockSpec(masks.shape, lambda b, j: (0, 0, 0)),
        ],
        out_specs=[
            pl.BlockSpec((bb, tt, D_MODEL), lambda b, j: (b, j, 0)),
            pl.BlockSpec((bb,) + stg_shape[1:], per_seq4),
            pl.BlockSpec((bb,) + sth_shape[1:], per_seq4),
        ],
        out_shape=[
            jax.ShapeDtypeStruct(x.shape, F32),
            jax.ShapeDtypeStruct(stg_shape, F32),
            jax.ShapeDtypeStruct(sth_shape, F32),
        ],
        scratch_shapes=[
            pltpu.VMEM((rows, D_MODEL), BF16),
            pltpu.VMEM((rows, K_LANES), F32),
            pltpu.VMEM((rows, K_LANES), F32),
            pltpu.VMEM((rows, K_LANES), F32),
            pltpu.VMEM((N_PAIRS, n_chunks, 2 * CHUNK, PAIR_V), BF16),
            pltpu.VMEM((rows, D_MODEL), F32),
            pltpu.VMEM((rows, D_MODEL), F32),
            pltpu.VMEM((n_chunks, N_K_TILES, CHUNK, LANES), F32),
            pltpu.VMEM((N_SLOTS, rows, K_LANES), BF16),
            pltpu.VMEM((N_LEVELS, N_GLA_PAIRS, n_chunks, LANES, 2 * CHUNK), BF16),
            pltpu.VMEM((N_LEVELS, N_HGRN_PAIRS, n_chunks, PAIR_V, 2 * CHUNK), BF16),
            pltpu.VMEM((N_GLA_PAIRS, n_chunks, 2 * CHUNK, LANES), BF16),
            pltpu.VMEM((HGRN_HEADS, n_chunks, CHUNK, LANES), BF16),
            pltpu.VMEM((n_chunks * SUBLANES, K_LANES), F32),
        ],
        compiler_params=pltpu.CompilerParams(
            dimension_semantics=("arbitrary", "arbitrary"),
            vmem_limit_bytes=VMEM_LIMIT_BYTES,
        ),
        name="mixer_bb%d_tt%d" % (bb, tt),
    )(x, mods, stg0, sth0, w_in, w_alpha, b_alpha, lb_logits, g_on, w_out, g_post, masks)


def _pack_states(s_gla, s_hgrn):
    nb = s_gla.shape[0]
    a = jnp.swapaxes(s_gla, -1, -2).reshape(nb, N_GLA_PAIRS, 2, GLA_DV, GLA_DK)
    z = jnp.zeros_like(a[:, :, 0])
    a = jnp.concatenate([jnp.concatenate([a[:, :, 0], z], axis=-1),
                         jnp.concatenate([z, a[:, :, 1]], axis=-1)], axis=2)
    return a, jnp.swapaxes(s_hgrn, -1, -2)


def _unpack_states(stg, sth):
    nb = stg.shape[0]
    heads = [stg[:, :, h * GLA_DV:(h + 1) * GLA_DV, h * GLA_DK:(h + 1) * GLA_DK] for h in range(2)]
    a = jnp.swapaxes(jnp.stack(heads, axis=2), -1, -2)
    return a.reshape(nb, GLA_HEADS, GLA_DK, GLA_DV), jnp.swapaxes(sth, -1, -2)


def kernel(x_prompt, x_sample, c_prompt, c_sample, state_gla, state_hgrn, w_ada, b_ada, g_pre, w_in,
           w_alpha, b_alpha, g_onorm_gla, hgrn_lb_logits, g_onorm_hgrn, w_out, g_post):
    assert w_ada.shape[0] == 1, "single-layer problem"
    bp = x_prompt.shape[0]

    o_qa, o_ka, o_va, o_za = 0, GLA_K, 2 * GLA_K, 2 * GLA_K + GLA_V
    o_al = o_za + GLA_V
    o_qh = o_al + GATE_RANK
    o_fh, o_ih, o_zh = o_qh + HGRN_W, o_qh + 2 * HGRN_W, o_qh + 3 * HGRN_W
    w = w_in[0]
    cols = lambda o, n: w[:, o:o + n]
    w_in_r = jnp.concatenate([
        cols(o_qa, GLA_K), cols(o_qh, HGRN_W), cols(o_ka, GLA_K), cols(o_fh, HGRN_W),
        cols(o_va, GLA_V), cols(o_ih, HGRN_W), cols(o_za, GLA_V), cols(o_zh, HGRN_W),
        cols(o_al, GATE_RANK), jnp.zeros((D_MODEL, RANK_PAD - GATE_RANK), w.dtype)], axis=1).astype(BF16)
    w_alpha_p = jnp.concatenate(
        [w_alpha[0], jnp.zeros((RANK_PAD - GATE_RANK, GLA_K), w_alpha.dtype)], axis=0).astype(BF16)
    g_on = jnp.concatenate([jnp.tile(g_onorm_gla[0], GLA_HEADS),
                            jnp.tile(g_onorm_hgrn[0], HGRN_HEADS)])[None, :]
    masks = jnp.asarray(_level_masks())

    c_all = jnp.concatenate([c_prompt, c_sample], axis=0)
    mods = _ada_call(c_all, w_ada[0], b_ada, g_pre).reshape(c_all.shape[0], 3, D_MODEL)

    shared = (w_in_r, w_alpha_p, b_alpha, hgrn_lb_logits, g_on, w_out[0].astype(BF16), g_post, masks)
    stg0_p = jnp.zeros((bp, N_GLA_PAIRS, PAIR_V, LANES), F32)
    sth0_p = jnp.zeros((bp, HGRN_HEADS, LANES, LANES), F32)
    yp, stg_p, sth_p = _mixer_call(x_prompt, mods[:bp], stg0_p, sth0_p, *shared, bb=1, tt=ROWS_PER_STEP)
    stg0_s, sth0_s = _pack_states(state_gla[0], state_hgrn[0])
    ys, stg_s, sth_s = _mixer_call(x_sample, mods[bp:], stg0_s, sth0_s, *shared,
                                   bb=ROWS_PER_STEP // x_sample.shape[1], tt=x_sample.shape[1])

    gla_p, hgrn_p = _unpack_states(stg_p, sth_p)
    gla_s, hgrn_s = _unpack_states(stg_s, sth_s)
    return (yp, ys, gla_p[None], hgrn_p[None], gla_s[None].astype(state_gla.dtype),
            hgrn_s[None].astype(state_hgrn.dtype))
```

```python
import functools
import math

import numpy as np
import jax
import jax.numpy as jnp
from jax import lax
from jax.experimental import pallas as pl
from jax.experimental.pallas import tpu as pltpu

D_MODEL = 1024
CHUNK = 64
GLA_HEADS = 4
GLA_V = D_MODEL // 2
GLA_K = GLA_V // 2
GLA_DK = GLA_K // GLA_HEADS
GLA_DV = GLA_V // GLA_HEADS
GATE_RANK = 16
GATE_TAU = 16.0
HGRN_HEADS = 4
HGRN_W = D_MODEL - GLA_V
HGRN_D = HGRN_W // HGRN_HEADS
EPS = 1e-6
LOG2E = math.log2(math.e)

LANES = 128
SUBLANES = 8
K_LANES = GLA_K + HGRN_W
N_K_TILES = K_LANES // LANES
N_GLA_PAIRS = GLA_HEADS // 2
N_HGRN_PAIRS = HGRN_HEADS // 2
N_PAIRS = N_GLA_PAIRS + N_HGRN_PAIRS
PAIR_V = 2 * LANES
RANK_PAD = LANES
HALF_SIZES = (32, 16, 8, 4, 2, 1)
N_LEVELS = len(HALF_SIZES) + 1
SLOT_STATE = N_LEVELS
N_SLOTS = N_LEVELS + 1
MASK_CAUSAL = N_LEVELS
SINGLE_ANCHOR_MAX_LOG2 = 90.0
ROWS_PER_STEP = 256
VMEM_LIMIT_BYTES = 56 * 1024 * 1024

F32 = jnp.float32
BF16 = jnp.bfloat16
NT = (((1,), (1,)), ((), ()))
TN = (((0,), (0,)), ((), ()))


def _level_masks():
    t = np.arange(CHUNK)[:, None]
    s = np.arange(CHUNK)[None, :]
    masks = [(t // (2 * m)) == (s // (2 * m)) for m in HALF_SIZES]
    masks.append(t == s)
    masks.append(t >= s)
    masks = np.stack(masks).astype(np.float32)
    return np.concatenate([masks, masks], axis=2)


def _rms(x):
    return x * lax.rsqrt(jnp.mean(x * x, axis=-1, keepdims=True) + EPS)


def _sigmoid(x):
    return 1.0 / (1.0 + jnp.exp(-x))


def _log2_sigmoid(x):
    return jnp.minimum(x, 0.0) * LOG2E - jnp.log2(1.0 + jnp.exp(-jnp.abs(x)))


def _ada_kernel(c_ref, w_ref, b_ref, gpre_ref, out_ref):
    n = pl.program_id(0)
    val = jnp.dot(c_ref[...].astype(BF16), w_ref[...].astype(BF16),
                  preferred_element_type=F32) + b_ref[...]
    out_ref[...] = jnp.where(n == 1, gpre_ref[...] * (1.0 + val), val)


def _ada_call(c_all, w_ada, b_ada, g_pre):
    nb = c_all.shape[0]
    return pl.pallas_call(
        _ada_kernel,
        grid=(3,),
        in_specs=[
            pl.BlockSpec((nb, D_MODEL), lambda n: (0, 0)),
            pl.BlockSpec((D_MODEL, D_MODEL), lambda n: (0, n)),
            pl.BlockSpec((1, D_MODEL), lambda n: (0, n)),
            pl.BlockSpec((1, D_MODEL), lambda n: (0, 0)),
        ],
        out_specs=pl.BlockSpec((nb, D_MODEL), lambda n: (0, n)),
        out_shape=jax.ShapeDtypeStruct((nb, 3 * D_MODEL), F32),
        name="ada_mod",
    )(c_all, w_ada, b_ada, g_pre)


def _ref_rows(b_ref, block, off):
    pieces = []
    if block >= SUBLANES:
        for j in range(CHUNK // block):
            row = b_ref[j * block + off:j * block + off + 1, :]
            pieces.append(jnp.broadcast_to(row, (block, LANES)))
    else:
        assert block == SUBLANES // 2
        sub = lax.broadcasted_iota(jnp.int32, (SUBLANES, LANES), 0)
        for i in range(CHUNK // SUBLANES):
            lo = b_ref[SUBLANES * i + off:SUBLANES * i + off + 1, :]
            hi = b_ref[SUBLANES * i + block + off:SUBLANES * i + block + off + 1, :]
            pieces.append(jnp.where(sub < block,
                                    jnp.broadcast_to(lo, (SUBLANES, LANES)),
                                    jnp.broadcast_to(hi, (SUBLANES, LANES))))
    return jnp.concatenate(pieces, axis=0)


def _mixer_kernel(x_ref, mods_ref, stg0_ref, sth0_ref, win_ref, walpha_ref, balpha_ref, lbl_ref,
                  gon_ref, wout_ref, gpost_ref, masks_ref,
                  y_ref, stg_ref, sth_ref,
                  hb_s, q_s, k_s, g_s, vbd_s, zs_s, o_s, b_s, lhs_s, ktg_s, kth_s, ksg_s, ksh_s, dcol_s, sbg_s, sbh_s,
                  *, bb, tt):
    rows = bb * tt
    n_chunks = rows // CHUNK
    chunks_per_seq = tt // CHUNK

    @pl.when((pl.program_id(0) == 0) & (pl.program_id(1) == 0))
    def _():
        vbd_s[...] = jnp.zeros(vbd_s.shape, BF16)
        kth_s[...] = jnp.zeros(kth_s.shape, BF16)

    @pl.when(pl.program_id(1) == 0)
    def _():
        stg_ref[...] = stg0_ref[...]
        sth_ref[...] = sth0_ref[...]

    def mod_rows(i):
        if bb == 1:
            return mods_ref[0, i:i + 1, :]
        return jnp.concatenate(
            [jnp.broadcast_to(mods_ref[bi, i:i + 1, :], (tt, D_MODEL)) for bi in range(bb)], axis=0)

    x = x_ref[...].reshape(rows, D_MODEL)
    hb_s[...] = (_rms(x) * mod_rows(1) + mod_rows(0)).astype(BF16)

    def proj(c0, c1):
        return jnp.dot(hb_s[...], win_ref[:, c0:c1], preferred_element_type=F32)

    col_q, col_k, col_v, col_z, col_a = 0, K_LANES, 2 * K_LANES, 2 * K_LANES + D_MODEL, 2 * K_LANES + 2 * D_MODEL

    a_lr = proj(col_a, col_a + RANK_PAD)
    ga = jnp.dot(a_lr.astype(BF16), walpha_ref[...], preferred_element_type=F32) + balpha_ref[...]
    g_s[:, :GLA_K] = _log2_sigmoid(ga) * (1.0 / GATE_TAU)

    pk = proj(col_k, col_k + K_LANES)
    k_s[:, :GLA_K] = pk[:, :GLA_K]
    lbl = lbl_ref[...]
    lmax = jnp.max(lbl, axis=0, keepdims=True)
    lexp = jnp.exp(lbl - lmax)
    lb = lexp[0:1, :] / jnp.sum(lexp, axis=0, keepdims=True)
    f = lb + (1.0 - lb) * _sigmoid(pk[:, GLA_K:])
    k_s[:, GLA_K:] = 1.0 - f
    g_s[:, GLA_K:] = jnp.log2(f)

    pq = proj(col_q, col_q + K_LANES)
    q_s[:, :GLA_K] = pq[:, :GLA_K] * (GLA_DK ** -0.5)
    qh = pq[:, GLA_K:]
    q_s[:, GLA_K:] = qh * _sigmoid(qh)

    def project_values():
        pv = proj(col_v, col_v + D_MODEL).astype(BF16)
        for ci in range(n_chunks):
            r = slice(ci * CHUNK, (ci + 1) * CHUNK)
            for p in range(N_PAIRS):
                vbd_s[p, ci, :CHUNK, :LANES] = pv[r, p * PAIR_V:p * PAIR_V + LANES]
                vbd_s[p, ci, CHUNK:, LANES:] = pv[r, p * PAIR_V + LANES:(p + 1) * PAIR_V]

    def project_gates():
        pz = proj(col_z, col_z + D_MODEL)
        zs_s[...] = pz * _sigmoid(pz)

    row_i = lax.broadcasted_iota(jnp.int32, (CHUNK, LANES), 0)
    lane_i = lax.broadcasted_iota(jnp.int32, (CHUNK, LANES), 1)
    head0 = lane_i < GLA_DK

    def tile_base(ci, lt):
        r = slice(ci * CHUNK, (ci + 1) * CHUNK)
        lanes = slice(lt * LANES, (lt + 1) * LANES)
        b_t = b_s.at[ci, lt]
        g = g_s[r, lanes]
        q = q_s[r, lanes]
        k = k_s[r, lanes]
        b = g
        for sh in (1, 2, 4):
            b = b + jnp.where(row_i >= sh, pltpu.roll(b, sh, 0), 0.0)
        for sh in (8, 16, 32):
            b = b + jnp.concatenate([jnp.zeros((sh, LANES), F32), b[:CHUNK - sh]], axis=0)
        b_t[...] = b
        b_last = b_t[CHUNK - 1:CHUNK, :]
        dcol_s[ci, lt] = jnp.exp2(jnp.broadcast_to(b_last, (LANES, LANES)).T)
        lhs_s[SLOT_STATE, r, lanes] = (q * jnp.exp2(b)).astype(BF16)
        k_state = k * jnp.exp2(b_last - b)
        if lt < N_GLA_PAIRS:
            ksg_s[lt, ci, :CHUNK, :] = jnp.where(head0, k_state, 0.0).astype(BF16)
            ksg_s[lt, ci, CHUNK:, :] = jnp.where(head0, 0.0, k_state).astype(BF16)
        else:
            ksh_s[lt - N_GLA_PAIRS, ci] = k_state.astype(BF16)
        return k * jnp.exp2(-b)

    def tile_reload(ci, lt):
        r = slice(ci * CHUNK, (ci + 1) * CHUNK)
        lanes = slice(lt * LANES, (lt + 1) * LANES)
        b_t = b_s.at[ci, lt]
        return q_s[r, lanes], k_s[r, lanes], g_s[r, lanes], b_t[...], b_t

    def tile_level(base, lv):
        q, k, g, b, b_t = base
        if lv == N_LEVELS - 1:
            return q, k
        m = HALF_SIZES[lv]
        upper = (row_i & (2 * m - 1)) >= m
        if m == 1:
            lhs, rhs = q * jnp.exp2(g), k
        else:
            e = jnp.exp2(-jnp.abs(b - _ref_rows(b_t, 2 * m, m - 1)))
            lhs, rhs = q * e, k * e
        return jnp.where(upper, lhs, 0.0), jnp.where(upper, 0.0, rhs)

    def store_keys_t(lv, ci, p, rhs_tiles):
        if p < N_GLA_PAIRS:
            rhs = rhs_tiles[0]
            stacked = jnp.concatenate([jnp.where(head0, rhs, 0.0), jnp.where(head0, 0.0, rhs)], axis=0)
            ktg_s[lv, p, ci] = stacked.astype(BF16).T
        else:
            kt = jnp.concatenate(rhs_tiles, axis=0).astype(BF16).T
            ph = p - N_GLA_PAIRS
            kth_s[lv, ph, ci, :LANES, :CHUNK] = kt[:, :CHUNK]
            kth_s[lv, ph, ci, LANES:, CHUNK:] = kt[:, CHUNK:]

    def tiles_of_pair(p):
        if p < N_GLA_PAIRS:
            return (p,)
        return tuple(N_GLA_PAIRS + 2 * (p - N_GLA_PAIRS) + pos for pos in range(2))

    def prep_pair(ci, p):
        store_keys_t(0, ci, p, [tile_base(ci, lt) for lt in tiles_of_pair(p)])

    def prep_pair_levels(ci, p):
        r = slice(ci * CHUNK, (ci + 1) * CHUNK)
        tiles = tiles_of_pair(p)
        bases = [tile_reload(ci, lt) for lt in tiles]
        for lv in range(N_LEVELS):
            rhs_tiles = []
            for lt, base in zip(tiles, bases):
                lhs, rhs = tile_level(base, lv)
                lhs_s[lv, r, lt * LANES:(lt + 1) * LANES] = lhs.astype(BF16)
                rhs_tiles.append(rhs)
            store_keys_t(lv, ci, p, rhs_tiles)

    def attention(ci, lhs_lanes, key_t, single_anchor):
        r = slice(ci * CHUNK, (ci + 1) * CHUNK)
        if single_anchor:
            p = jnp.dot(lhs_s[SLOT_STATE, r, lhs_lanes], key_t(0), preferred_element_type=F32)
            return jnp.where(masks_ref[MASK_CAUSAL] > 0.0, p, 0.0).astype(BF16)
        att = None
        for lv in range(N_LEVELS):
            p = jnp.dot(lhs_s[lv, r, lhs_lanes], key_t(lv), preferred_element_type=F32)
            if lv > 0:
                p = p * masks_ref[lv]
            att = p if att is None else att + p
        return att.astype(BF16)

    def advance_states(ci):
        bi, cseq = divmod(ci, chunks_per_seq)
        for p in range(N_GLA_PAIRS):
            st = stg_ref[bi, p]
            sbg_s[ci, p] = st.astype(BF16)
            dcol = dcol_s[ci, p]
            stg_ref[bi, p] = (st * jnp.concatenate([dcol, dcol], axis=1)
                              + lax.dot_general(ksg_s[p, ci], vbd_s[p, ci], TN, preferred_element_type=F32))
        for hh in range(HGRN_HEADS):
            pg, pos = N_GLA_PAIRS + hh // 2, hh % 2
            st = sth_ref[bi, hh]
            sbh_s[ci, hh] = st.astype(BF16)
            v_head = vbd_s[pg, ci, pos * CHUNK:(pos + 1) * CHUNK, pos * LANES:(pos + 1) * LANES]
            sth_ref[bi, hh] = st * dcol_s[ci, N_GLA_PAIRS + hh] + lax.dot_general(
                ksh_s[hh, ci], v_head, TN, preferred_element_type=F32)

    def pair_attention(ci, p, single_anchor):
        if p < N_GLA_PAIRS:
            return attention(ci, slice(p * LANES, (p + 1) * LANES), lambda lv: ktg_s[lv, p, ci], single_anchor)
        ph = p - N_GLA_PAIRS
        lanes = slice(GLA_K + ph * PAIR_V, GLA_K + (ph + 1) * PAIR_V)
        return attention(ci, lanes, lambda lv: kth_s[lv, ph, ci], single_anchor)

    def pair_output(ci, p, att):
        r = slice(ci * CHUNK, (ci + 1) * CHUNK)
        o = jnp.dot(att, vbd_s[p, ci], preferred_element_type=F32)
        if p < N_GLA_PAIRS:
            o_s[r, p * PAIR_V:(p + 1) * PAIR_V] = o + jnp.dot(
                lhs_s[SLOT_STATE, r, p * LANES:(p + 1) * LANES], sbg_s[ci, p], preferred_element_type=F32)
            return
        for pos in range(2):
            hh = 2 * (p - N_GLA_PAIRS) + pos
            hl = slice(GLA_K + hh * LANES, GLA_K + (hh + 1) * LANES)
            o_s[r, p * PAIR_V + pos * LANES:p * PAIR_V + (pos + 1) * LANES] = (
                o[:, pos * LANES:(pos + 1) * LANES]
                + jnp.dot(lhs_s[SLOT_STATE, r, hl], sbh_s[ci, hh], preferred_element_type=F32))

    atts = {}
    for step in range(n_chunks + 2):
        if step < n_chunks:
            for p in range(N_PAIRS):
                prep_pair(step, p)
        if step == 0:
            project_values()
        if step == 1:
            project_gates()
        if 1 <= step <= n_chunks:
            advance_states(step - 1)
            atts[step - 1] = [pair_attention(step - 1, p, True) for p in range(N_PAIRS)]
        if step >= 2:
            for p in range(N_PAIRS):
                pair_output(step - 2, p, atts[step - 2][p])

    total = None
    for ci in range(n_chunks):
        t = jnp.sum(g_s[ci * CHUNK:(ci + 1) * CHUNK, :], axis=0, keepdims=True)
        total = t if total is None else jnp.minimum(total, t)
    mild = jnp.min(total) >= -SINGLE_ANCHOR_MAX_LOG2

    @pl.when(jnp.logical_not(mild))
    def _():
        for ci in range(n_chunks):
            for p in range(N_PAIRS):
                prep_pair_levels(ci, p)
            for p in range(N_PAIRS):
                pair_output(ci, p, pair_attention(ci, p, False))

    half = rows // 2
    gate_rows = mod_rows(2)
    for h in range(2):
        rr = slice(h * half, (h + 1) * half)
        gated = []
        for ht in range(GLA_HEADS + HGRN_HEADS):
            lanes = slice(ht * LANES, (ht + 1) * LANES)
            gated.append((_rms(o_s[rr, lanes]) * gon_ref[:, lanes] * zs_s[rr, lanes]).astype(BF16))
        proj_out = jnp.dot(jnp.concatenate(gated, axis=1), wout_ref[...], preferred_element_type=F32)
        gate_h = gate_rows if bb == 1 else gate_rows[rr]
        res = gate_h * (_rms(proj_out) * gpost_ref[...])
        if bb == 1:
            y_ref[0, rr, :] = x_ref[0, rr, :] + res
        else:
            seqs = slice(h * (bb // 2), (h + 1) * (bb // 2))
            y_ref[seqs] = x_ref[seqs] + res.reshape(bb // 2, tt, D_MODEL)


def _mixer_call(x, mods, stg0, sth0, w_in, w_alpha, b_alpha, lb_logits, g_on, w_out, g_post, masks, *, bb, tt):
    nb, seq, _ = x.shape
    rows = bb * tt
    assert nb % bb == 0 and seq % tt == 0 and tt % CHUNK == 0
    n_chunks = rows // CHUNK
    in_cols = w_in.shape[1]
    const2 = lambda b, j: (0, 0)
    per_seq4 = lambda b, j: (b, 0, 0, 0)
    kernel = functools.partial(_mixer_kernel, bb=bb, tt=tt)
    stg_shape = (nb, N_GLA_PAIRS, LANES, PAIR_V)
    sth_shape = (nb, HGRN_HEADS, HGRN_D, HGRN_D)
    return pl.pallas_call(
        kernel,
        grid=(nb // bb, seq // tt),
        in_specs=[
            pl.BlockSpec((bb, tt, D_MODEL), lambda b, j: (b, j, 0)),
            pl.BlockSpec((bb, 3, D_MODEL), lambda b, j: (b, 0, 0)),
            pl.BlockSpec((bb,) + stg_shape[1:], per_seq4),
            pl.BlockSpec((bb,) + sth_shape[1:], per_seq4),
            pl.BlockSpec((D_MODEL, in_cols), const2),
            pl.BlockSpec((RANK_PAD, GLA_K), const2),
            pl.BlockSpec((1, GLA_K), const2),
            pl.BlockSpec(lb_logits.shape, const2),
            pl.BlockSpec((1, D_MODEL), const2),
            pl.BlockSpec((D_MODEL, D_MODEL), const2),
            pl.BlockSpec((1, D_MODEL), const2),
            pl.BlockSpec(masks.shape, lambda b, j: (0, 0, 0)),
        ],
        out_specs=[
            pl.BlockSpec((bb, tt, D_MODEL), lambda b, j: (b, j, 0)),
            pl.BlockSpec((bb,) + stg_shape[1:], per_seq4),
            pl.BlockSpec((bb,) + sth_shape[1:], per_seq4),
        ],
        out_shape=[
            jax.ShapeDtypeStruct(x.shape, F32),
            jax.ShapeDtypeStruct(stg_shape, F32),
            jax.ShapeDtypeStruct(sth_shape, F32),
        ],
        scratch_shapes=[
            pltpu.VMEM((rows, D_MODEL), BF16),
            pltpu.VMEM((rows, K_LANES), F32),
            pltpu.VMEM((rows, K_LANES), F32),
            pltpu.VMEM((rows, K_LANES), F32),
            pltpu.VMEM((N_PAIRS, n_chunks, 2 * CHUNK, PAIR_V), BF16),
            pltpu.VMEM((rows, D_MODEL), F32),
            pltpu.VMEM((rows, D_MODEL), F32),
            pltpu.VMEM((n_chunks, N_K_TILES, CHUNK, LANES), F32),
            pltpu.VMEM((N_SLOTS, rows, K_LANES), BF16),
            pltpu.VMEM((N_LEVELS, N_GLA_PAIRS, n_chunks, LANES, 2 * CHUNK), BF16),
            pltpu.VMEM((N_LEVELS, N_HGRN_PAIRS, n_chunks, PAIR_V, 2 * CHUNK), BF16),
            pltpu.VMEM((N_GLA_PAIRS, n_chunks, 2 * CHUNK, LANES), BF16),
            pltpu.VMEM((HGRN_HEADS, n_chunks, CHUNK, LANES), BF16),
            pltpu.VMEM((n_chunks, N_K_TILES, LANES, LANES), F32),
            pltpu.VMEM((n_chunks, N_GLA_PAIRS, LANES, PAIR_V), BF16),
            pltpu.VMEM((n_chunks, HGRN_HEADS, HGRN_D, HGRN_D), BF16),
        ],
        compiler_params=pltpu.CompilerParams(
            dimension_semantics=("arbitrary", "arbitrary"),
            vmem_limit_bytes=VMEM_LIMIT_BYTES,
        ),
        name="mixer_bb%d_tt%d" % (bb, tt),
    )(x, mods, stg0, sth0, w_in, w_alpha, b_alpha, lb_logits, g_on, w_out, g_post, masks)


def _pair_gla_states(s_gla):
    nb = s_gla.shape[0]
    a = s_gla.reshape(nb, N_GLA_PAIRS, 2, GLA_DK, GLA_DV)
    z = jnp.zeros_like(a[:, :, 0])
    return jnp.concatenate([jnp.concatenate([a[:, :, 0], z], axis=-1),
                            jnp.concatenate([z, a[:, :, 1]], axis=-1)], axis=2)


def _unpair_gla_states(stg):
    nb = stg.shape[0]
    heads = [stg[:, :, h * GLA_DK:(h + 1) * GLA_DK, h * GLA_DV:(h + 1) * GLA_DV] for h in range(2)]
    return jnp.stack(heads, axis=2).reshape(nb, GLA_HEADS, GLA_DK, GLA_DV)


def kernel(x_prompt, x_sample, c_prompt, c_sample, state_gla, state_hgrn, w_ada, b_ada, g_pre, w_in,
           w_alpha, b_alpha, g_onorm_gla, hgrn_lb_logits, g_onorm_hgrn, w_out, g_post):
    assert w_ada.shape[0] == 1, "single-layer problem"
    bp = x_prompt.shape[0]

    o_qa, o_ka, o_va, o_za = 0, GLA_K, 2 * GLA_K, 2 * GLA_K + GLA_V
    o_al = o_za + GLA_V
    o_qh = o_al + GATE_RANK
    o_fh, o_ih, o_zh = o_qh + HGRN_W, o_qh + 2 * HGRN_W, o_qh + 3 * HGRN_W
    w = w_in[0]
    cols = lambda o, n: w[:, o:o + n]
    w_in_r = jnp.concatenate([
        cols(o_qa, GLA_K), cols(o_qh, HGRN_W), cols(o_ka, GLA_K), cols(o_fh, HGRN_W),
        cols(o_va, GLA_V), cols(o_ih, HGRN_W), cols(o_za, GLA_V), cols(o_zh, HGRN_W),
        cols(o_al, GATE_RANK), jnp.zeros((D_MODEL, RANK_PAD - GATE_RANK), w.dtype)], axis=1).astype(BF16)
    w_alpha_p = jnp.concatenate(
        [w_alpha[0], jnp.zeros((RANK_PAD - GATE_RANK, GLA_K), w_alpha.dtype)], axis=0).astype(BF16)
    g_on = jnp.concatenate([jnp.tile(g_onorm_gla[0], GLA_HEADS),
                            jnp.tile(g_onorm_hgrn[0], HGRN_HEADS)])[None, :]
    masks = jnp.asarray(_level_masks())

    c_all = jnp.concatenate([c_prompt, c_sample], axis=0)
    mods = _ada_call(c_all, w_ada[0], b_ada, g_pre).reshape(c_all.shape[0], 3, D_MODEL)

    shared = (w_in_r, w_alpha_p, b_alpha, hgrn_lb_logits, g_on, w_out[0].astype(BF16), g_post, masks)
    stg0_p = jnp.zeros((bp, N_GLA_PAIRS, LANES, PAIR_V), F32)
    sth0_p = jnp.zeros((bp, HGRN_HEADS, HGRN_D, HGRN_D), F32)
    yp, stg_p, hgrn_p = _mixer_call(x_prompt, mods[:bp], stg0_p, sth0_p, *shared, bb=1, tt=ROWS_PER_STEP)
    ys, stg_s, hgrn_s = _mixer_call(x_sample, mods[bp:], _pair_gla_states(state_gla[0]), state_hgrn[0],
                                    *shared, bb=ROWS_PER_STEP // x_sample.shape[1], tt=x_sample.shape[1])
    return (yp, ys, _unpair_gla_states(stg_p)[None], hgrn_p[None],
            _unpair_gla_states(stg_s)[None].astype(state_gla.dtype), hgrn_s[None].astype(state_hgrn.dtype))
```

```python
import functools
import math

import numpy as np
import jax
import jax.numpy as jnp
from jax import lax
from jax.experimental import pallas as pl
from jax.experimental.pallas import tpu as pltpu

D_MODEL = 1024
CHUNK = 64
GLA_HEADS = 4
GLA_V = D_MODEL // 2
GLA_K = GLA_V // 2
GLA_DK = GLA_K // GLA_HEADS
GLA_DV = GLA_V // GLA_HEADS
GATE_RANK = 16
GATE_TAU = 16.0
HGRN_HEADS = 4
HGRN_W = D_MODEL - GLA_V
HGRN_D = HGRN_W // HGRN_HEADS
EPS = 1e-6
LOG2E = math.log2(math.e)

LANES = 128
SUBLANES = 8
K_LANES = GLA_K + HGRN_W
N_K_TILES = K_LANES // LANES
N_GLA_PAIRS = GLA_HEADS // 2
N_HGRN_PAIRS = HGRN_HEADS // 2
N_PAIRS = N_GLA_PAIRS + N_HGRN_PAIRS
PAIR_V = 2 * LANES
RANK_PAD = LANES
HALF_SIZES = (32, 16, 8, 4, 2, 1)
N_LEVELS = len(HALF_SIZES) + 1
MASK_CAUSAL = N_LEVELS
SINGLE_ANCHOR_MAX_LOG2 = 90.0
ROWS_PER_STEP = 512
ROWS_PER_STEP_BATCHED = 256
VMEM_LIMIT_BYTES = 56 * 1024 * 1024

F32 = jnp.float32
BF16 = jnp.bfloat16
NT = (((1,), (1,)), ((), ()))
TN = (((0,), (0,)), ((), ()))


def _level_masks():
    t = np.arange(CHUNK)[:, None]
    s = np.arange(CHUNK)[None, :]
    masks = [(t // (2 * m)) == (s // (2 * m)) for m in HALF_SIZES]
    masks.append(t == s)
    masks.append(t >= s)
    masks = np.stack(masks).astype(np.float32)
    return np.concatenate([masks, masks], axis=2)


def _rms(x):
    return x * lax.rsqrt(jnp.mean(x * x, axis=-1, keepdims=True) + EPS)


def _sigmoid(x):
    return 1.0 / (1.0 + jnp.exp(-x))


def _log2_sigmoid(x):
    return jnp.minimum(x, 0.0) * LOG2E - jnp.log2(1.0 + jnp.exp(-jnp.abs(x)))


def _ada_kernel(c_ref, w_ref, b_ref, gpre_ref, out_ref):
    n = pl.program_id(0)
    val = jnp.dot(c_ref[...].astype(BF16), w_ref[...].astype(BF16),
                  preferred_element_type=F32) + b_ref[...]
    out_ref[...] = jnp.where(n == 1, gpre_ref[...] * (1.0 + val), val)


def _ada_call(c_all, w_ada, b_ada, g_pre):
    nb = c_all.shape[0]
    return pl.pallas_call(
        _ada_kernel,
        grid=(3,),
        in_specs=[
            pl.BlockSpec((nb, D_MODEL), lambda n: (0, 0)),
            pl.BlockSpec((D_MODEL, D_MODEL), lambda n: (0, n)),
            pl.BlockSpec((1, D_MODEL), lambda n: (0, n)),
            pl.BlockSpec((1, D_MODEL), lambda n: (0, 0)),
        ],
        out_specs=pl.BlockSpec((nb, D_MODEL), lambda n: (0, n)),
        out_shape=jax.ShapeDtypeStruct((nb, 3 * D_MODEL), F32),
        name="ada_mod",
    )(c_all, w_ada, b_ada, g_pre)


def _ref_rows(b_ref, block, off):
    pieces = []
    if block >= SUBLANES:
        for j in range(CHUNK // block):
            row = b_ref[j * block + off:j * block + off + 1, :]
            pieces.append(jnp.broadcast_to(row, (block, LANES)))
    else:
        assert block == SUBLANES // 2
        sub = lax.broadcasted_iota(jnp.int32, (SUBLANES, LANES), 0)
        for i in range(CHUNK // SUBLANES):
            lo = b_ref[SUBLANES * i + off:SUBLANES * i + off + 1, :]
            hi = b_ref[SUBLANES * i + block + off:SUBLANES * i + block + off + 1, :]
            pieces.append(jnp.where(sub < block,
                                    jnp.broadcast_to(lo, (SUBLANES, LANES)),
                                    jnp.broadcast_to(hi, (SUBLANES, LANES))))
    return jnp.concatenate(pieces, axis=0)


def _mixer_kernel(x_ref, mods_ref, stg0_ref, sth0_ref, win_ref, walpha_ref, balpha_ref, lbl_ref,
                  gon_ref, wout_ref, gpost_ref, masks_ref,
                  y_ref, stg_ref, sth_ref,
                  hb_s, q_s, k_s, g_s, vbd_s, zs_s, o_s, b_s, qst_s, kag_s, kah_s, ksg_s, ksh_s, dcol_s, sbg_s, sbh_s,
                  lql_s, klg_s, klh_s,
                  *, bb, tt):
    rows = bb * tt
    n_chunks = rows // CHUNK
    chunks_per_seq = tt // CHUNK

    @pl.when((pl.program_id(0) == 0) & (pl.program_id(1) == 0))
    def _():
        for ref in (vbd_s, kah_s, klh_s, sbh_s):
            ref[...] = jnp.zeros(ref.shape, BF16)

    @pl.when(pl.program_id(1) == 0)
    def _():
        stg_ref[...] = stg0_ref[...]
        sth_ref[...] = sth0_ref[...]

    def mod_rows(i):
        if bb == 1:
            return mods_ref[0, i:i + 1, :]
        return jnp.concatenate(
            [jnp.broadcast_to(mods_ref[bi, i:i + 1, :], (tt, D_MODEL)) for bi in range(bb)], axis=0)

    x = x_ref[...].reshape(rows, D_MODEL)
    hb_s[...] = (_rms(x) * mod_rows(1) + mod_rows(0)).astype(BF16)

    def proj(c0, c1):
        return jnp.dot(hb_s[...], win_ref[:, c0:c1], preferred_element_type=F32)

    col_q, col_k, col_v, col_z, col_a = 0, K_LANES, 2 * K_LANES, 2 * K_LANES + D_MODEL, 2 * K_LANES + 2 * D_MODEL

    a_lr = proj(col_a, col_a + RANK_PAD)
    ga = jnp.dot(a_lr.astype(BF16), walpha_ref[...], preferred_element_type=F32) + balpha_ref[...]
    g_s[:, :GLA_K] = _log2_sigmoid(ga) * (1.0 / GATE_TAU)

    pk = proj(col_k, col_k + K_LANES)
    k_s[:, :GLA_K] = pk[:, :GLA_K]
    lbl = lbl_ref[...]
    lmax = jnp.max(lbl, axis=0, keepdims=True)
    lexp = jnp.exp(lbl - lmax)
    lb = lexp[0:1, :] / jnp.sum(lexp, axis=0, keepdims=True)
    f = lb + (1.0 - lb) * _sigmoid(pk[:, GLA_K:])
    k_s[:, GLA_K:] = 1.0 - f
    g_s[:, GLA_K:] = jnp.log2(f)

    pq = proj(col_q, col_q + K_LANES)
    q_s[:, :GLA_K] = pq[:, :GLA_K] * (GLA_DK ** -0.5)
    qh = pq[:, GLA_K:]
    q_s[:, GLA_K:] = qh * _sigmoid(qh)

    def project_values():
        pv = proj(col_v, col_v + D_MODEL).astype(BF16)
        for ci in range(n_chunks):
            r = slice(ci * CHUNK, (ci + 1) * CHUNK)
            for p in range(N_PAIRS):
                vbd_s[p, ci, :CHUNK, :LANES] = pv[r, p * PAIR_V:p * PAIR_V + LANES]
                vbd_s[p, ci, CHUNK:, LANES:] = pv[r, p * PAIR_V + LANES:(p + 1) * PAIR_V]

    def project_gates():
        pz = proj(col_z, col_z + D_MODEL)
        zs_s[...] = pz * _sigmoid(pz)

    row_i = lax.broadcasted_iota(jnp.int32, (CHUNK, LANES), 0)
    lane_i = lax.broadcasted_iota(jnp.int32, (CHUNK, LANES), 1)
    head0 = lane_i < GLA_DK

    def tile_base(ci, lt):
        r = slice(ci * CHUNK, (ci + 1) * CHUNK)
        lanes = slice(lt * LANES, (lt + 1) * LANES)
        b_t = b_s.at[ci, lt]
        g = g_s[r, lanes]
        q = q_s[r, lanes]
        k = k_s[r, lanes]
        b = g
        for sh in (1, 2, 4):
            b = b + jnp.where(row_i >= sh, pltpu.roll(b, sh, 0), 0.0)
        for sh in (8, 16, 32):
            b = b + jnp.concatenate([jnp.zeros((sh, LANES), F32), b[:CHUNK - sh]], axis=0)
        b_t[...] = b
        b_last = b_t[CHUNK - 1:CHUNK, :]
        dcol_s[ci, lt] = jnp.exp2(jnp.broadcast_to(b_last, (LANES, LANES)).T)
        qst_s[r, lanes] = (q * jnp.exp2(b)).astype(BF16)
        k_state = k * jnp.exp2(b_last - b)
        if lt < N_GLA_PAIRS:
            ksg_s[lt, ci, :CHUNK, :] = jnp.where(head0, k_state, 0.0).astype(BF16)
            ksg_s[lt, ci, CHUNK:, :] = jnp.where(head0, 0.0, k_state).astype(BF16)
        else:
            ksh_s[lt - N_GLA_PAIRS, ci] = k_state.astype(BF16)
        return k * jnp.exp2(-b)

    def tile_reload(ci, lt):
        r = slice(ci * CHUNK, (ci + 1) * CHUNK)
        lanes = slice(lt * LANES, (lt + 1) * LANES)
        b_t = b_s.at[ci, lt]
        return q_s[r, lanes], k_s[r, lanes], g_s[r, lanes], b_t[...], b_t

    def tile_level(base, lv):
        q, k, g, b, b_t = base
        if lv == N_LEVELS - 1:
            return q, k
        m = HALF_SIZES[lv]
        upper = (row_i & (2 * m - 1)) >= m
        if m == 1:
            lhs, rhs = q * jnp.exp2(g), k
        else:
            e = jnp.exp2(-jnp.abs(b - _ref_rows(b_t, 2 * m, m - 1)))
            lhs, rhs = q * e, k * e
        return jnp.where(upper, lhs, 0.0), jnp.where(upper, 0.0, rhs)

    def store_keys_t(dst, p, rhs_tiles):
        if p < N_GLA_PAIRS:
            rhs = rhs_tiles[0]
            stacked = jnp.concatenate([jnp.where(head0, rhs, 0.0), jnp.where(head0, 0.0, rhs)], axis=0)
            dst[...] = stacked.astype(BF16).T
        else:
            kt = jnp.concatenate(rhs_tiles, axis=0).astype(BF16).T
            dst[:LANES, :CHUNK] = kt[:, :CHUNK]
            dst[LANES:, CHUNK:] = kt[:, CHUNK:]

    def tiles_of_pair(p):
        if p < N_GLA_PAIRS:
            return (p,)
        return tuple(N_GLA_PAIRS + 2 * (p - N_GLA_PAIRS) + pos for pos in range(2))

    def pair_lanes(p):
        if p < N_GLA_PAIRS:
            return slice(p * LANES, (p + 1) * LANES)
        return slice(GLA_K + (p - N_GLA_PAIRS) * PAIR_V, GLA_K + (p - N_GLA_PAIRS + 1) * PAIR_V)

    def anchor_keys(ci, p):
        return kag_s.at[p, ci] if p < N_GLA_PAIRS else kah_s.at[p - N_GLA_PAIRS, ci]

    def level_keys(lv, p):
        return klg_s.at[lv, p] if p < N_GLA_PAIRS else klh_s.at[lv, p - N_GLA_PAIRS]

    def prep_pair(ci, p):
        store_keys_t(anchor_keys(ci, p), p, [tile_base(ci, lt) for lt in tiles_of_pair(p)])

    def prep_pair_levels(ci, p):
        tiles = tiles_of_pair(p)
        bases = [tile_reload(ci, lt) for lt in tiles]
        for lv in range(N_LEVELS):
            rhs_tiles = []
            for lt, base in zip(tiles, bases):
                lhs, rhs = tile_level(base, lv)
                lql_s[lv, :, lt * LANES:(lt + 1) * LANES] = lhs.astype(BF16)
                rhs_tiles.append(rhs)
            store_keys_t(level_keys(lv, p), p, rhs_tiles)

    def pair_attention(ci, p, single_anchor):
        lanes = pair_lanes(p)
        if single_anchor:
            prod = jnp.dot(qst_s[ci * CHUNK:(ci + 1) * CHUNK, lanes], anchor_keys(ci, p)[...],
                           preferred_element_type=F32)
            return jnp.where(masks_ref[MASK_CAUSAL] > 0.0, prod, 0.0).astype(BF16)
        att = None
        for lv in range(N_LEVELS):
            prod = jnp.dot(lql_s[lv, :, lanes], level_keys(lv, p)[...], preferred_element_type=F32)
            if lv > 0:
                prod = prod * masks_ref[lv]
            att = prod if att is None else att + prod
        return att.astype(BF16)

    def advance_states(ci):
        bi, cseq = divmod(ci, chunks_per_seq)
        for p in range(N_GLA_PAIRS):
            st = stg_ref[bi, p]
            sbg_s[ci, p] = st.astype(BF16)
            dcol = dcol_s[ci, p]
            stg_ref[bi, p] = (st * jnp.concatenate([dcol, dcol], axis=1)
                              + lax.dot_general(ksg_s[p, ci], vbd_s[p, ci], TN, preferred_element_type=F32))
        for hh in range(HGRN_HEADS):
            pg, pos = N_GLA_PAIRS + hh // 2, hh % 2
            st = sth_ref[bi, hh]
            sbh_s[ci, hh // 2, pos * LANES:(pos + 1) * LANES, pos * LANES:(pos + 1) * LANES] = st.astype(BF16)
            v_head = vbd_s[pg, ci, pos * CHUNK:(pos + 1) * CHUNK, pos * LANES:(pos + 1) * LANES]
            sth_ref[bi, hh] = st * dcol_s[ci, N_GLA_PAIRS + hh] + lax.dot_general(
                ksh_s[hh, ci], v_head, TN, preferred_element_type=F32)

    def pair_output(ci, p, att):
        r = slice(ci * CHUNK, (ci + 1) * CHUNK)
        state = sbg_s[ci, p] if p < N_GLA_PAIRS else sbh_s[ci, p - N_GLA_PAIRS]
        o_s[r, p * PAIR_V:(p + 1) * PAIR_V] = (
            jnp.dot(att, vbd_s[p, ci], preferred_element_type=F32)
            + jnp.dot(qst_s[r, pair_lanes(p)], state, preferred_element_type=F32))

    half = rows // 2
    gate_rows = mod_rows(2)

    def finish_half(h):
        rr = slice(h * half, (h + 1) * half)
        gated = []
        for ht in range(GLA_HEADS + HGRN_HEADS):
            lanes = slice(ht * LANES, (ht + 1) * LANES)
            gated.append((_rms(o_s[rr, lanes]) * gon_ref[:, lanes] * zs_s[rr, lanes]).astype(BF16))
        proj_out = jnp.dot(jnp.concatenate(gated, axis=1), wout_ref[...], preferred_element_type=F32)
        gate_h = gate_rows if bb == 1 else gate_rows[rr]
        res = gate_h * (_rms(proj_out) * gpost_ref[...])
        if bb == 1:
            y_ref[0, rr, :] = x_ref[0, rr, :] + res
        else:
            seqs = slice(h * (bb // 2), (h + 1) * (bb // 2))
            y_ref[seqs] = x_ref[seqs] + res.reshape(bb // 2, tt, D_MODEL)

    atts = {}
    for step in range(n_chunks + 2):
        if step < n_chunks:
            for p in range(N_PAIRS):
                prep_pair(step, p)
        if step == 0:
            project_values()
        if step == 1:
            project_gates()
        if 1 <= step <= n_chunks:
            advance_states(step - 1)
            atts[step - 1] = [pair_attention(step - 1, p, True) for p in range(N_PAIRS)]
        if step >= 2:
            for p in range(N_PAIRS):
                pair_output(step - 2, p, atts[step - 2][p])
            if (step - 1) * CHUNK == half:
                finish_half(0)
    finish_half(1)

    total = None
    for ci in range(n_chunks):
        t = jnp.sum(g_s[ci * CHUNK:(ci + 1) * CHUNK, :], axis=0, keepdims=True)
        total = t if total is None else jnp.minimum(total, t)
    mild = jnp.min(total) >= -SINGLE_ANCHOR_MAX_LOG2

    @pl.when(jnp.logical_not(mild))
    def _():
        for ci in range(n_chunks):
            for p in range(N_PAIRS):
                prep_pair_levels(ci, p)
            for p in range(N_PAIRS):
                pair_output(ci, p, pair_attention(ci, p, False))
        finish_half(0)
        finish_half(1)


def _mixer_call(x, mods, stg0, sth0, w_in, w_alpha, b_alpha, lb_logits, g_on, w_out, g_post, masks, *, bb, tt):
    nb, seq, _ = x.shape
    rows = bb * tt
    assert nb % bb == 0 and seq % tt == 0 and tt % CHUNK == 0
    n_chunks = rows // CHUNK
    in_cols = w_in.shape[1]
    per_seq4 = lambda b, j: (b, 0, 0, 0)

    def resident(shape):
        return pl.BlockSpec(shape, lambda b, j: (0,) * len(shape), pipeline_mode=pl.Buffered(1))

    kernel = functools.partial(_mixer_kernel, bb=bb, tt=tt)
    stg_shape = (nb, N_GLA_PAIRS, LANES, PAIR_V)
    sth_shape = (nb, HGRN_HEADS, HGRN_D, HGRN_D)
    return pl.pallas_call(
        kernel,
        grid=(nb // bb, seq // tt),
        in_specs=[
            pl.BlockSpec((bb, tt, D_MODEL), lambda b, j: (b, j, 0)),
            pl.BlockSpec((bb, 3, D_MODEL), lambda b, j: (b, 0, 0)),
            pl.BlockSpec((bb,) + stg_shape[1:], per_seq4),
            pl.BlockSpec((bb,) + sth_shape[1:], per_seq4),
            resident((D_MODEL, in_cols)),
            resident((RANK_PAD, GLA_K)),
            resident((1, GLA_K)),
            resident(lb_logits.shape),
            resident((1, D_MODEL)),
            resident((D_MODEL, D_MODEL)),
            resident((1, D_MODEL)),
            resident(masks.shape),
        ],
        out_specs=[
            pl.BlockSpec((bb, tt, D_MODEL), lambda b, j: (b, j, 0)),
            pl.BlockSpec((bb,) + stg_shape[1:], per_seq4),
            pl.BlockSpec((bb,) + sth_shape[1:], per_seq4),
        ],
        out_shape=[
            jax.ShapeDtypeStruct(x.shape, F32),
            jax.ShapeDtypeStruct(stg_shape, F32),
            jax.ShapeDtypeStruct(sth_shape, F32),
        ],
        scratch_shapes=[
            pltpu.VMEM((rows, D_MODEL), BF16),
            pltpu.VMEM((rows, K_LANES), F32),
            pltpu.VMEM((rows, K_LANES), F32),
            pltpu.VMEM((rows, K_LANES), F32),
            pltpu.VMEM((N_PAIRS, n_chunks, 2 * CHUNK, PAIR_V), BF16),
            pltpu.VMEM((rows, D_MODEL), F32),
            pltpu.VMEM((rows, D_MODEL), F32),
            pltpu.VMEM((n_chunks, N_K_TILES, CHUNK, LANES), F32),
            pltpu.VMEM((rows, K_LANES), BF16),
            pltpu.VMEM((N_GLA_PAIRS, n_chunks, LANES, 2 * CHUNK), BF16),
            pltpu.VMEM((N_HGRN_PAIRS, n_chunks, PAIR_V, 2 * CHUNK), BF16),
            pltpu.VMEM((N_GLA_PAIRS, n_chunks, 2 * CHUNK, LANES), BF16),
            pltpu.VMEM((HGRN_HEADS, n_chunks, CHUNK, LANES), BF16),
            pltpu.VMEM((n_chunks, N_K_TILES, LANES, LANES), F32),
            pltpu.VMEM((n_chunks, N_GLA_PAIRS, LANES, PAIR_V), BF16),
            pltpu.VMEM((n_chunks, N_HGRN_PAIRS, PAIR_V, PAIR_V), BF16),
            pltpu.VMEM((N_LEVELS, CHUNK, K_LANES), BF16),
            pltpu.VMEM((N_LEVELS, N_GLA_PAIRS, LANES, 2 * CHUNK), BF16),
            pltpu.VMEM((N_LEVELS, N_HGRN_PAIRS, PAIR_V, 2 * CHUNK), BF16),
        ],
        compiler_params=pltpu.CompilerParams(
            dimension_semantics=("arbitrary", "arbitrary"),
            vmem_limit_bytes=VMEM_LIMIT_BYTES,
        ),
        name="mixer_bb%d_tt%d" % (bb, tt),
    )(x, mods, stg0, sth0, w_in, w_alpha, b_alpha, lb_logits, g_on, w_out, g_post, masks)


def _pair_gla_states(s_gla):
    nb = s_gla.shape[0]
    a = s_gla.reshape(nb, N_GLA_PAIRS, 2, GLA_DK, GLA_DV)
    z = jnp.zeros_like(a[:, :, 0])
    return jnp.concatenate([jnp.concatenate([a[:, :, 0], z], axis=-1),
                            jnp.concatenate([z, a[:, :, 1]], axis=-1)], axis=2)


def _unpair_gla_states(stg):
    nb = stg.shape[0]
    heads = [stg[:, :, h * GLA_DK:(h + 1) * GLA_DK, h * GLA_DV:(h + 1) * GLA_DV] for h in range(2)]
    return jnp.stack(heads, axis=2).reshape(nb, GLA_HEADS, GLA_DK, GLA_DV)


def kernel(x_prompt, x_sample, c_prompt, c_sample, state_gla, state_hgrn, w_ada, b_ada, g_pre, w_in,
           w_alpha, b_alpha, g_onorm_gla, hgrn_lb_logits, g_onorm_hgrn, w_out, g_post):
    assert w_ada.shape[0] == 1, "single-layer problem"
    bp = x_prompt.shape[0]

    o_qa, o_ka, o_va, o_za = 0, GLA_K, 2 * GLA_K, 2 * GLA_K + GLA_V
    o_al = o_za + GLA_V
    o_qh = o_al + GATE_RANK
    o_fh, o_ih, o_zh = o_qh + HGRN_W, o_qh + 2 * HGRN_W, o_qh + 3 * HGRN_W
    w = w_in[0]
    cols = lambda o, n: w[:, o:o + n]
    w_in_r = jnp.concatenate([
        cols(o_qa, GLA_K), cols(o_qh, HGRN_W), cols(o_ka, GLA_K), cols(o_fh, HGRN_W),
        cols(o_va, GLA_V), cols(o_ih, HGRN_W), cols(o_za, GLA_V), cols(o_zh, HGRN_W),
        cols(o_al, GATE_RANK), jnp.zeros((D_MODEL, RANK_PAD - GATE_RANK), w.dtype)], axis=1).astype(BF16)
    w_alpha_p = jnp.concatenate(
        [w_alpha[0], jnp.zeros((RANK_PAD - GATE_RANK, GLA_K), w_alpha.dtype)], axis=0).astype(BF16)
    g_on = jnp.concatenate([jnp.tile(g_onorm_gla[0], GLA_HEADS),
                            jnp.tile(g_onorm_hgrn[0], HGRN_HEADS)])[None, :]
    masks = jnp.asarray(_level_masks())

    c_all = jnp.concatenate([c_prompt, c_sample], axis=0)
    mods = _ada_call(c_all, w_ada[0], b_ada, g_pre).reshape(c_all.shape[0], 3, D_MODEL)

    shared = (w_in_r, w_alpha_p, b_alpha, hgrn_lb_logits, g_on, w_out[0].astype(BF16), g_post, masks)
    stg0_p = jnp.zeros((bp, N_GLA_PAIRS, LANES, PAIR_V), F32)
    sth0_p = jnp.zeros((bp, HGRN_HEADS, HGRN_D, HGRN_D), F32)
    yp, stg_p, hgrn_p = _mixer_call(x_prompt, mods[:bp], stg0_p, sth0_p, *shared, bb=1, tt=ROWS_PER_STEP)
    ys, stg_s, hgrn_s = _mixer_call(x_sample, mods[bp:], _pair_gla_states(state_gla[0]), state_hgrn[0],
                                    *shared, bb=ROWS_PER_STEP_BATCHED // x_sample.shape[1], tt=x_sample.shape[1])
    return (yp, ys, _unpair_gla_states(stg_p)[None], hgrn_p[None],
            _unpair_gla_states(stg_s)[None].astype(state_gla.dtype), hgrn_s[None].astype(state_hgrn.dtype))
```

```python
import functools
import math

import numpy as np
import jax
import jax.numpy as jnp
from jax import lax
from jax.experimental import pallas as pl
from jax.experimental.pallas import tpu as pltpu

D_MODEL = 1024
CHUNK = 64
GLA_HEADS = 4
GLA_V = D_MODEL // 2
GLA_K = GLA_V // 2
GLA_DK = GLA_K // GLA_HEADS
GLA_DV = GLA_V // GLA_HEADS
GATE_RANK = 16
GATE_TAU = 16.0
HGRN_HEADS = 4
HGRN_W = D_MODEL - GLA_V
HGRN_D = HGRN_W // HGRN_HEADS
EPS = 1e-6
LOG2E = math.log2(math.e)

LANES = 128
SUBLANES = 8
K_LANES = GLA_K + HGRN_W
N_K_TILES = K_LANES // LANES
N_GLA_PAIRS = GLA_HEADS // 2
N_HGRN_PAIRS = HGRN_HEADS // 2
N_PAIRS = N_GLA_PAIRS + N_HGRN_PAIRS
PAIR_V = 2 * LANES
N_COL_BLOCKS = D_MODEL // PAIR_V
RANK_PAD = LANES
HALF_SIZES = (32, 16, 8, 4, 2, 1)
N_LEVELS = len(HALF_SIZES) + 1
MASK_CAUSAL = N_LEVELS
SINGLE_ANCHOR_MAX_LOG2 = 90.0
ROWS_PER_STEP = 256
VMEM_LIMIT_BYTES = 56 * 1024 * 1024

F32 = jnp.float32
BF16 = jnp.bfloat16
NT = (((1,), (1,)), ((), ()))
TN = (((0,), (0,)), ((), ()))


def _level_masks():
    t = np.arange(CHUNK)[:, None]
    s = np.arange(CHUNK)[None, :]
    masks = [(t // (2 * m)) == (s // (2 * m)) for m in HALF_SIZES]
    masks.append(t == s)
    masks.append(t >= s)
    masks = np.stack(masks).astype(np.float32)
    return np.concatenate([masks, masks], axis=2)


def _rms(x):
    return x * lax.rsqrt(jnp.mean(x * x, axis=-1, keepdims=True) + EPS)


def _sigmoid(x):
    return 1.0 / (1.0 + jnp.exp(-x))


def _log2_sigmoid(x):
    return jnp.minimum(x, 0.0) * LOG2E - jnp.log2(1.0 + jnp.exp(-jnp.abs(x)))


def _ada_kernel(c_ref, w_ref, b_ref, gpre_ref, out_ref):
    n = pl.program_id(0)
    val = jnp.dot(c_ref[...].astype(BF16), w_ref[...].astype(BF16),
                  preferred_element_type=F32) + b_ref[...]
    out_ref[...] = jnp.where(n == 1, gpre_ref[...] * (1.0 + val), val)


def _ada_call(c_all, w_ada, b_ada, g_pre):
    nb = c_all.shape[0]
    return pl.pallas_call(
        _ada_kernel,
        grid=(3,),
        in_specs=[
            pl.BlockSpec((nb, D_MODEL), lambda n: (0, 0)),
            pl.BlockSpec((D_MODEL, D_MODEL), lambda n: (0, n)),
            pl.BlockSpec((1, D_MODEL), lambda n: (0, n)),
            pl.BlockSpec((1, D_MODEL), lambda n: (0, 0)),
        ],
        out_specs=pl.BlockSpec((nb, D_MODEL), lambda n: (0, n)),
        out_shape=jax.ShapeDtypeStruct((nb, 3 * D_MODEL), F32),
        name="ada_mod",
    )(c_all, w_ada, b_ada, g_pre)


def _ref_rows(b_ref, block, off):
    pieces = []
    if block >= SUBLANES:
        for j in range(CHUNK // block):
            row = b_ref[j * block + off:j * block + off + 1, :]
            pieces.append(jnp.broadcast_to(row, (block, LANES)))
    else:
        assert block == SUBLANES // 2
        sub = lax.broadcasted_iota(jnp.int32, (SUBLANES, LANES), 0)
        for i in range(CHUNK // SUBLANES):
            lo = b_ref[SUBLANES * i + off:SUBLANES * i + off + 1, :]
            hi = b_ref[SUBLANES * i + block + off:SUBLANES * i + block + off + 1, :]
            pieces.append(jnp.where(sub < block,
                                    jnp.broadcast_to(lo, (SUBLANES, LANES)),
                                    jnp.broadcast_to(hi, (SUBLANES, LANES))))
    return jnp.concatenate(pieces, axis=0)


def _mixer_kernel(x_ref, mods_ref, stg0_ref, sth0_ref, win_ref, walpha_ref, balpha_ref, lbl_ref,
                  gon_ref, wout_ref, gpost_ref, masks_ref,
                  y_ref, stg_ref, sth_ref,
                  hb_s, q_s, k_s, g_s, vbd_s, zs_s, o_s, b_s, qst_s, kag_s, kah_s, ksg_s, ksh_s, dcol_s, sbg_s, sbh_s,
                  lql_s, klg_s, klh_s, gb_s, po_s,
                  *, bb, tt):
    rows = bb * tt
    n_chunks = rows // CHUNK
    chunks_per_seq = tt // CHUNK

    @pl.when((pl.program_id(0) == 0) & (pl.program_id(1) == 0))
    def _():
        for ref in (vbd_s, kah_s, klh_s, sbh_s):
            ref[...] = jnp.zeros(ref.shape, BF16)

    @pl.when(pl.program_id(1) == 0)
    def _():
        stg_ref[...] = stg0_ref[...]
        sth_ref[...] = sth0_ref[...]

    def mod_rows(i):
        if bb == 1:
            return mods_ref[0, i:i + 1, :]
        return jnp.concatenate(
            [jnp.broadcast_to(mods_ref[bi, i:i + 1, :], (tt, D_MODEL)) for bi in range(bb)], axis=0)

    half = rows // 2
    x = x_ref[...].reshape(rows, D_MODEL)
    hb_s[...] = (_rms(x) * mod_rows(1) + mod_rows(0)).astype(BF16)

    def proj(c0, c1):
        return jnp.dot(hb_s[...], win_ref[:, c0:c1], preferred_element_type=F32)

    col_q, col_k, col_v, col_z, col_a = 0, K_LANES, 2 * K_LANES, 2 * K_LANES + D_MODEL, 2 * K_LANES + 2 * D_MODEL

    a_lr = proj(col_a, col_a + RANK_PAD)
    ga = jnp.dot(a_lr.astype(BF16), walpha_ref[...], preferred_element_type=F32) + balpha_ref[...]
    g_s[:, :GLA_K] = _log2_sigmoid(ga) * (1.0 / GATE_TAU)

    pk = proj(col_k, col_k + K_LANES)
    k_s[:, :GLA_K] = pk[:, :GLA_K]
    lbl = lbl_ref[...]
    lmax = jnp.max(lbl, axis=0, keepdims=True)
    lexp = jnp.exp(lbl - lmax)
    lb = lexp[0:1, :] / jnp.sum(lexp, axis=0, keepdims=True)
    f = lb + (1.0 - lb) * _sigmoid(pk[:, GLA_K:])
    k_s[:, GLA_K:] = 1.0 - f
    g_s[:, GLA_K:] = jnp.log2(f)

    pq = proj(col_q, col_q + K_LANES)
    q_s[:, :GLA_K] = pq[:, :GLA_K] * (GLA_DK ** -0.5)
    qh = pq[:, GLA_K:]
    q_s[:, GLA_K:] = qh * _sigmoid(qh)

    def project_values():
        pv = proj(col_v, col_v + D_MODEL).astype(BF16)
        for ci in range(n_chunks):
            r = slice(ci * CHUNK, (ci + 1) * CHUNK)
            for p in range(N_PAIRS):
                vbd_s[p, ci, :CHUNK, :LANES] = pv[r, p * PAIR_V:p * PAIR_V + LANES]
                vbd_s[p, ci, CHUNK:, LANES:] = pv[r, p * PAIR_V + LANES:(p + 1) * PAIR_V]

    def project_gates(n):
        cols = slice(n * PAIR_V, (n + 1) * PAIR_V)
        pz = proj(col_z + n * PAIR_V, col_z + (n + 1) * PAIR_V)
        zs_s[:, cols] = pz * _sigmoid(pz)

    row_i = lax.broadcasted_iota(jnp.int32, (CHUNK, LANES), 0)
    lane_i = lax.broadcasted_iota(jnp.int32, (CHUNK, LANES), 1)
    head0 = lane_i < GLA_DK

    def tile_base(ci, lt):
        r = slice(ci * CHUNK, (ci + 1) * CHUNK)
        lanes = slice(lt * LANES, (lt + 1) * LANES)
        b_t = b_s.at[ci, lt]
        g = g_s[r, lanes]
        q = q_s[r, lanes]
        k = k_s[r, lanes]
        b = g
        for sh in (1, 2, 4):
            b = b + jnp.where(row_i >= sh, pltpu.roll(b, sh, 0), 0.0)
        for sh in (8, 16, 32):
            b = b + jnp.concatenate([jnp.zeros((sh, LANES), F32), b[:CHUNK - sh]], axis=0)
        b_t[...] = b
        b_last = b_t[CHUNK - 1:CHUNK, :]
        dcol_s[ci, lt] = jnp.exp2(jnp.broadcast_to(b_last, (LANES, LANES)).T)
        qst_s[r, lanes] = (q * jnp.exp2(b)).astype(BF16)
        k_state = k * jnp.exp2(b_last - b)
        if lt < N_GLA_PAIRS:
            ksg_s[lt, ci, :CHUNK, :] = jnp.where(head0, k_state, 0.0).astype(BF16)
            ksg_s[lt, ci, CHUNK:, :] = jnp.where(head0, 0.0, k_state).astype(BF16)
        else:
            ksh_s[lt - N_GLA_PAIRS, ci] = k_state.astype(BF16)
        return k * jnp.exp2(-b)

    def tile_reload(ci, lt):
        r = slice(ci * CHUNK, (ci + 1) * CHUNK)
        lanes = slice(lt * LANES, (lt + 1) * LANES)
        b_t = b_s.at[ci, lt]
        return q_s[r, lanes], k_s[r, lanes], g_s[r, lanes], b_t[...], b_t

    def tile_level(base, lv):
        q, k, g, b, b_t = base
        if lv == N_LEVELS - 1:
            return q, k
        m = HALF_SIZES[lv]
        upper = (row_i & (2 * m - 1)) >= m
        if m == 1:
            lhs, rhs = q * jnp.exp2(g), k
        else:
            e = jnp.exp2(-jnp.abs(b - _ref_rows(b_t, 2 * m, m - 1)))
            lhs, rhs = q * e, k * e
        return jnp.where(upper, lhs, 0.0), jnp.where(upper, 0.0, rhs)

    def store_keys_t(dst, p, rhs_tiles):
        if p < N_GLA_PAIRS:
            rhs = rhs_tiles[0]
            stacked = jnp.concatenate([jnp.where(head0, rhs, 0.0), jnp.where(head0, 0.0, rhs)], axis=0)
            dst[...] = stacked.astype(BF16).T
        else:
            kt = jnp.concatenate(rhs_tiles, axis=0).astype(BF16).T
            dst[:LANES, :CHUNK] = kt[:, :CHUNK]
            dst[LANES:, CHUNK:] = kt[:, CHUNK:]

    def tiles_of_pair(p):
        if p < N_GLA_PAIRS:
            return (p,)
        return tuple(N_GLA_PAIRS + 2 * (p - N_GLA_PAIRS) + pos for pos in range(2))

    def pair_lanes(p):
        if p < N_GLA_PAIRS:
            return slice(p * LANES, (p + 1) * LANES)
        return slice(GLA_K + (p - N_GLA_PAIRS) * PAIR_V, GLA_K + (p - N_GLA_PAIRS + 1) * PAIR_V)

    def anchor_keys(ci, p):
        return kag_s.at[p, ci] if p < N_GLA_PAIRS else kah_s.at[p - N_GLA_PAIRS, ci]

    def level_keys(lv, p, ci):
        return klg_s.at[lv, p, ci] if p < N_GLA_PAIRS else klh_s.at[lv, p - N_GLA_PAIRS, ci]

    def prep_pair(ci, p):
        store_keys_t(anchor_keys(ci, p), p, [tile_base(ci, lt) for lt in tiles_of_pair(p)])

    def prep_pair_levels(ci, p):
        tiles = tiles_of_pair(p)
        bases = [tile_reload(ci, lt) for lt in tiles]
        for lv in range(N_LEVELS):
            rhs_tiles = []
            for lt, base in zip(tiles, bases):
                lhs, rhs = tile_level(base, lv)
                lql_s[lv, ci * CHUNK:(ci + 1) * CHUNK, lt * LANES:(lt + 1) * LANES] = lhs.astype(BF16)
                rhs_tiles.append(rhs)
            store_keys_t(level_keys(lv, p, ci), p, rhs_tiles)

    def pair_attention(ci, p, single_anchor):
        lanes = pair_lanes(p)
        if single_anchor:
            prod = jnp.dot(qst_s[ci * CHUNK:(ci + 1) * CHUNK, lanes], anchor_keys(ci, p)[...],
                           preferred_element_type=F32)
            return jnp.where(masks_ref[MASK_CAUSAL] > 0.0, prod, 0.0).astype(BF16)
        att = None
        for lv in range(N_LEVELS):
            prod = jnp.dot(lql_s[lv, ci * CHUNK:(ci + 1) * CHUNK, lanes], level_keys(lv, p, ci)[...],
                           preferred_element_type=F32)
            if lv > 0:
                prod = prod * masks_ref[lv]
            att = prod if att is None else att + prod
        return att.astype(BF16)

    def snapshot_states(ci):
        bi = ci // chunks_per_seq
        for p in range(N_GLA_PAIRS):
            sbg_s[ci, p] = stg_ref[bi, p].astype(BF16)
        for hh in range(HGRN_HEADS):
            pos = hh % 2
            sbh_s[ci, hh // 2, pos * LANES:(pos + 1) * LANES, pos * LANES:(pos + 1) * LANES] = (
                sth_ref[bi, hh].astype(BF16))

    def update_states(ci):
        bi = ci // chunks_per_seq
        for p in range(N_GLA_PAIRS):
            dcol = dcol_s[ci, p]
            stg_ref[bi, p] = (stg_ref[bi, p] * jnp.concatenate([dcol, dcol], axis=1)
                              + lax.dot_general(ksg_s[p, ci], vbd_s[p, ci], TN, preferred_element_type=F32))
        for hh in range(HGRN_HEADS):
            pg, pos = N_GLA_PAIRS + hh // 2, hh % 2
            v_head = vbd_s[pg, ci, pos * CHUNK:(pos + 1) * CHUNK, pos * LANES:(pos + 1) * LANES]
            sth_ref[bi, hh] = sth_ref[bi, hh] * dcol_s[ci, N_GLA_PAIRS + hh] + lax.dot_general(
                ksh_s[hh, ci], v_head, TN, preferred_element_type=F32)

    def pair_output(ci, p, att):
        r = slice(ci * CHUNK, (ci + 1) * CHUNK)
        state = sbg_s[ci, p] if p < N_GLA_PAIRS else sbh_s[ci, p - N_GLA_PAIRS]
        o_s[r, p * PAIR_V:(p + 1) * PAIR_V] = (
            jnp.dot(att, vbd_s[p, ci], preferred_element_type=F32)
            + jnp.dot(qst_s[r, pair_lanes(p)], state, preferred_element_type=F32))

    gate_rows = mod_rows(2)

    def gate_half(h):
        rr = slice(h * half, (h + 1) * half)
        for ht in range(GLA_HEADS + HGRN_HEADS):
            lanes = slice(ht * LANES, (ht + 1) * LANES)
            gb_s[rr, lanes] = (_rms(o_s[rr, lanes]) * gon_ref[:, lanes] * zs_s[rr, lanes]).astype(BF16)

    def project_half(h, n):
        rr = slice(h * half, (h + 1) * half)
        cols = slice(n * PAIR_V, (n + 1) * PAIR_V)
        po_s[rr, cols] = jnp.dot(gb_s[rr, :], wout_ref[:, cols], preferred_element_type=F32)

    def post_half(h):
        rr = slice(h * half, (h + 1) * half)
        gate_h = gate_rows if bb == 1 else gate_rows[rr]
        res = gate_h * (_rms(po_s[rr, :]) * gpost_ref[...])
        if bb == 1:
            y_ref[0, rr, :] = x_ref[0, rr, :] + res
        else:
            seqs = slice(h * (bb // 2), (h + 1) * (bb // 2))
            y_ref[seqs] = x_ref[seqs] + res.reshape(bb // 2, tt, D_MODEL)

    def finish_stages(h):
        return ([functools.partial(project_half, h, n) for n in range(N_COL_BLOCKS)]
                + [functools.partial(post_half, h)])

    fillers = [functools.partial(project_gates, n) for n in range(N_COL_BLOCKS)]
    atts = {}
    for step in range(n_chunks + 2):
        if step < n_chunks:
            for p in range(N_PAIRS):
                prep_pair(step, p)
        if step == 0:
            project_values()
        if 1 <= step <= n_chunks:
            snapshot_states(step - 1)
            if step < n_chunks:
                update_states(step - 1)
            atts[step - 1] = [pair_attention(step - 1, p, True) for p in range(N_PAIRS)]
        if fillers:
            fillers.pop(0)()
        if step >= 2:
            for p in range(N_PAIRS):
                pair_output(step - 2, p, atts[step - 2][p])
            if (step - 1) * CHUNK == half:
                while fillers:
                    fillers.pop(0)()
                gate_half(0)
                fillers = finish_stages(0)
    last_stages = finish_stages(1)
    for stage in ([functools.partial(gate_half, 1)] + fillers + last_stages[:-1]
                  + [functools.partial(update_states, n_chunks - 1)] + last_stages[-1:]):
        stage()

    total = None
    for ci in range(n_chunks):
        t = jnp.sum(g_s[ci * CHUNK:(ci + 1) * CHUNK, :], axis=0, keepdims=True)
        total = t if total is None else jnp.minimum(total, t)
    mild = jnp.min(total) >= -SINGLE_ANCHOR_MAX_LOG2

    @pl.when(jnp.logical_not(mild))
    def _():
        for ci in range(n_chunks):
            for p in range(N_PAIRS):
                prep_pair_levels(ci, p)
            for p in range(N_PAIRS):
                pair_output(ci, p, pair_attention(ci, p, False))
        for h in range(2):
            for stage in [functools.partial(gate_half, h)] + finish_stages(h):
                stage()


def _mixer_call(x, mods, stg0, sth0, w_in, w_alpha, b_alpha, lb_logits, g_on, w_out, g_post, masks, *, bb, tt):
    nb, seq, _ = x.shape
    rows = bb * tt
    assert nb % bb == 0 and seq % tt == 0 and tt % CHUNK == 0
    n_chunks = rows // CHUNK
    in_cols = w_in.shape[1]
    per_seq4 = lambda b, j: (b, 0, 0, 0)

    def resident(shape):
        return pl.BlockSpec(shape, lambda b, j: (0,) * len(shape), pipeline_mode=pl.Buffered(1))

    kernel = functools.partial(_mixer_kernel, bb=bb, tt=tt)
    stg_shape = (nb, N_GLA_PAIRS, LANES, PAIR_V)
    sth_shape = (nb, HGRN_HEADS, HGRN_D, HGRN_D)
    return pl.pallas_call(
        kernel,
        grid=(nb // bb, seq // tt),
        in_specs=[
            pl.BlockSpec((bb, tt, D_MODEL), lambda b, j: (b, j, 0)),
            pl.BlockSpec((bb, 3, D_MODEL), lambda b, j: (b, 0, 0)),
            pl.BlockSpec((bb,) + stg_shape[1:], per_seq4),
            pl.BlockSpec((bb,) + sth_shape[1:], per_seq4),
            resident((D_MODEL, in_cols)),
            resident((RANK_PAD, GLA_K)),
            resident((1, GLA_K)),
            resident(lb_logits.shape),
            resident((1, D_MODEL)),
            resident((D_MODEL, D_MODEL)),
            resident((1, D_MODEL)),
            resident(masks.shape),
        ],
        out_specs=[
            pl.BlockSpec((bb, tt, D_MODEL), lambda b, j: (b, j, 0)),
            pl.BlockSpec((bb,) + stg_shape[1:], per_seq4),
            pl.BlockSpec((bb,) + sth_shape[1:], per_seq4),
        ],
        out_shape=[
            jax.ShapeDtypeStruct(x.shape, F32),
            jax.ShapeDtypeStruct(stg_shape, F32),
            jax.ShapeDtypeStruct(sth_shape, F32),
        ],
        scratch_shapes=[
            pltpu.VMEM((rows, D_MODEL), BF16),
            pltpu.VMEM((rows, K_LANES), F32),
            pltpu.VMEM((rows, K_LANES), F32),
            pltpu.VMEM((rows, K_LANES), F32),
            pltpu.VMEM((N_PAIRS, n_chunks, 2 * CHUNK, PAIR_V), BF16),
            pltpu.VMEM((rows, D_MODEL), F32),
            pltpu.VMEM((rows, D_MODEL), F32),
            pltpu.VMEM((n_chunks, N_K_TILES, CHUNK, LANES), F32),
            pltpu.VMEM((rows, K_LANES), BF16),
            pltpu.VMEM((N_GLA_PAIRS, n_chunks, LANES, 2 * CHUNK), BF16),
            pltpu.VMEM((N_HGRN_PAIRS, n_chunks, PAIR_V, 2 * CHUNK), BF16),
            pltpu.VMEM((N_GLA_PAIRS, n_chunks, 2 * CHUNK, LANES), BF16),
            pltpu.VMEM((HGRN_HEADS, n_chunks, CHUNK, LANES), BF16),
            pltpu.VMEM((n_chunks, N_K_TILES, LANES, LANES), F32),
            pltpu.VMEM((n_chunks, N_GLA_PAIRS, LANES, PAIR_V), BF16),
            pltpu.VMEM((n_chunks, N_HGRN_PAIRS, PAIR_V, PAIR_V), BF16),
            pltpu.VMEM((N_LEVELS, rows, K_LANES), BF16),
            pltpu.VMEM((N_LEVELS, N_GLA_PAIRS, n_chunks, LANES, 2 * CHUNK), BF16),
            pltpu.VMEM((N_LEVELS, N_HGRN_PAIRS, n_chunks, PAIR_V, 2 * CHUNK), BF16),
            pltpu.VMEM((rows, D_MODEL), BF16),
            pltpu.VMEM((rows, D_MODEL), F32),
        ],
        compiler_params=pltpu.CompilerParams(
            dimension_semantics=("arbitrary", "arbitrary"),
            vmem_limit_bytes=VMEM_LIMIT_BYTES,
        ),
        name="mixer_bb%d_tt%d" % (bb, tt),
    )(x, mods, stg0, sth0, w_in, w_alpha, b_alpha, lb_logits, g_on, w_out, g_post, masks)


def _pair_gla_states(s_gla):
    nb = s_gla.shape[0]
    a = s_gla.reshape(nb, N_GLA_PAIRS, 2, GLA_DK, GLA_DV)
    z = jnp.zeros_like(a[:, :, 0])
    return jnp.concatenate([jnp.concatenate([a[:, :, 0], z], axis=-1),
                            jnp.concatenate([z, a[:, :, 1]], axis=-1)], axis=2)


def _unpair_gla_states(stg):
    nb = stg.shape[0]
    heads = [stg[:, :, h * GLA_DK:(h + 1) * GLA_DK, h * GLA_DV:(h + 1) * GLA_DV] for h in range(2)]
    return jnp.stack(heads, axis=2).reshape(nb, GLA_HEADS, GLA_DK, GLA_DV)


def kernel(x_prompt, x_sample, c_prompt, c_sample, state_gla, state_hgrn, w_ada, b_ada, g_pre, w_in,
           w_alpha, b_alpha, g_onorm_gla, hgrn_lb_logits, g_onorm_hgrn, w_out, g_post):
    assert w_ada.shape[0] == 1, "single-layer problem"
    bp = x_prompt.shape[0]

    o_qa, o_ka, o_va, o_za = 0, GLA_K, 2 * GLA_K, 2 * GLA_K + GLA_V
    o_al = o_za + GLA_V
    o_qh = o_al + GATE_RANK
    o_fh, o_ih, o_zh = o_qh + HGRN_W, o_qh + 2 * HGRN_W, o_qh + 3 * HGRN_W
    w = w_in[0]
    cols = lambda o, n: w[:, o:o + n]
    w_in_r = jnp.concatenate([
        cols(o_qa, GLA_K), cols(o_qh, HGRN_W), cols(o_ka, GLA_K), cols(o_fh, HGRN_W),
        cols(o_va, GLA_V), cols(o_ih, HGRN_W), cols(o_za, GLA_V), cols(o_zh, HGRN_W),
        cols(o_al, GATE_RANK), jnp.zeros((D_MODEL, RANK_PAD - GATE_RANK), w.dtype)], axis=1).astype(BF16)
    w_alpha_p = jnp.concatenate(
        [w_alpha[0], jnp.zeros((RANK_PAD - GATE_RANK, GLA_K), w_alpha.dtype)], axis=0).astype(BF16)
    g_on = jnp.concatenate([jnp.tile(g_onorm_gla[0], GLA_HEADS),
                            jnp.tile(g_onorm_hgrn[0], HGRN_HEADS)])[None, :]
    masks = jnp.asarray(_level_masks())

    c_all = jnp.concatenate([c_prompt, c_sample], axis=0)
    mods = _ada_call(c_all, w_ada[0], b_ada, g_pre).reshape(c_all.shape[0], 3, D_MODEL)

    shared = (w_in_r, w_alpha_p, b_alpha, hgrn_lb_logits, g_on, w_out[0].astype(BF16), g_post, masks)
    stg0_p = jnp.zeros((bp, N_GLA_PAIRS, LANES, PAIR_V), F32)
    sth0_p = jnp.zeros((bp, HGRN_HEADS, HGRN_D, HGRN_D), F32)
    yp, stg_p, hgrn_p = _mixer_call(x_prompt, mods[:bp], stg0_p, sth0_p, *shared, bb=1, tt=ROWS_PER_STEP)
    ys, stg_s, hgrn_s = _mixer_call(x_sample, mods[bp:], _pair_gla_states(state_gla[0]), state_hgrn[0],
                                    *shared, bb=ROWS_PER_STEP // x_sample.shape[1], tt=x_sample.shape[1])
    return (yp, ys, _unpair_gla_states(stg_p)[None], hgrn_p[None],
            _unpair_gla_states(stg_s)[None].astype(state_gla.dtype), hgrn_s[None].astype(state_hgrn.dtype))
```

```python
import functools
import math

import numpy as np
import jax
import jax.numpy as jnp
from jax import lax
from jax.experimental import pallas as pl
from jax.experimental.pallas import tpu as pltpu

D_MODEL = 1024
CHUNK = 64
GLA_HEADS = 4
GLA_V = D_MODEL // 2
GLA_K = GLA_V // 2
GLA_DK = GLA_K // GLA_HEADS
GLA_DV = GLA_V // GLA_HEADS
GATE_RANK = 16
GATE_TAU = 16.0
HGRN_HEADS = 4
HGRN_W = D_MODEL - GLA_V
HGRN_D = HGRN_W // HGRN_HEADS
EPS = 1e-6
LOG2E = math.log2(math.e)

LANES = 128
SUBLANES = 8
K_LANES = GLA_K + HGRN_W
N_K_TILES = K_LANES // LANES
N_GLA_PAIRS = GLA_HEADS // 2
N_HGRN_PAIRS = HGRN_HEADS // 2
N_PAIRS = N_GLA_PAIRS + N_HGRN_PAIRS
PAIR_V = 2 * LANES
N_COL_BLOCKS = D_MODEL // PAIR_V
RANK_PAD = LANES
HALF_SIZES = (32, 16, 8, 4, 2, 1)
N_LEVELS = len(HALF_SIZES) + 1
MASK_CAUSAL = N_LEVELS
SINGLE_ANCHOR_MAX_LOG2 = 90.0
ROWS_PER_STEP = 256
VMEM_LIMIT_BYTES = 56 * 1024 * 1024

F32 = jnp.float32
BF16 = jnp.bfloat16
NT = (((1,), (1,)), ((), ()))
TN = (((0,), (0,)), ((), ()))


def _level_masks():
    t = np.arange(CHUNK)[:, None]
    s = np.arange(CHUNK)[None, :]
    masks = [(t // (2 * m)) == (s // (2 * m)) for m in HALF_SIZES]
    masks.append(t == s)
    masks.append(t >= s)
    masks = np.stack(masks).astype(np.float32)
    return np.concatenate([masks, masks], axis=2)


def _rms(x):
    return x * lax.rsqrt(jnp.mean(x * x, axis=-1, keepdims=True) + EPS)


def _sigmoid(x):
    return 1.0 / (1.0 + jnp.exp(-x))


def _log2_sigmoid(x):
    return jnp.minimum(x, 0.0) * LOG2E - jnp.log2(1.0 + jnp.exp(-jnp.abs(x)))


def _ada_kernel(c_ref, w_ref, b_ref, gpre_ref, out_ref):
    n = pl.program_id(0)
    val = jnp.dot(c_ref[...].astype(BF16), w_ref[...].astype(BF16),
                  preferred_element_type=F32) + b_ref[...]
    out_ref[...] = jnp.where(n == 1, gpre_ref[...] * (1.0 + val), val)


def _ada_call(c_all, w_ada, b_ada, g_pre):
    nb = c_all.shape[0]
    return pl.pallas_call(
        _ada_kernel,
        grid=(3,),
        in_specs=[
            pl.BlockSpec((nb, D_MODEL), lambda n: (0, 0)),
            pl.BlockSpec((D_MODEL, D_MODEL), lambda n: (0, n)),
            pl.BlockSpec((1, D_MODEL), lambda n: (0, n)),
            pl.BlockSpec((1, D_MODEL), lambda n: (0, 0)),
        ],
        out_specs=pl.BlockSpec((nb, D_MODEL), lambda n: (0, n)),
        out_shape=jax.ShapeDtypeStruct((nb, 3 * D_MODEL), F32),
        name="ada_mod",
    )(c_all, w_ada, b_ada, g_pre)


def _ref_rows(b_ref, block, off):
    pieces = []
    if block >= SUBLANES:
        for j in range(CHUNK // block):
            row = b_ref[j * block + off:j * block + off + 1, :]
            pieces.append(jnp.broadcast_to(row, (block, LANES)))
    else:
        assert block == SUBLANES // 2
        sub = lax.broadcasted_iota(jnp.int32, (SUBLANES, LANES), 0)
        for i in range(CHUNK // SUBLANES):
            lo = b_ref[SUBLANES * i + off:SUBLANES * i + off + 1, :]
            hi = b_ref[SUBLANES * i + block + off:SUBLANES * i + block + off + 1, :]
            pieces.append(jnp.where(sub < block,
                                    jnp.broadcast_to(lo, (SUBLANES, LANES)),
                                    jnp.broadcast_to(hi, (SUBLANES, LANES))))
    return jnp.concatenate(pieces, axis=0)


def _mixer_kernel(x_ref, mods_ref, stg0_ref, sth0_ref, win_ref, walpha_ref, balpha_ref, lbl_ref,
                  gon_ref, wout_ref, gpost_ref, masks_ref,
                  y_ref, stg_ref, sth_ref,
                  hb_s, q_s, k_s, g_s, vbd_s, zs_s, o_s, b_s, qst_s, kag_s, kah_s, ksg_s, ksh_s, dcol_s, sbg_s, sbh_s,
                  lql_s, klg_s, klh_s, gb_s, po_s,
                  *, bb, tt):
    rows = bb * tt
    n_chunks = rows // CHUNK
    chunks_per_seq = tt // CHUNK

    @pl.when((pl.program_id(0) == 0) & (pl.program_id(1) == 0))
    def _():
        for ref in (vbd_s, kah_s, klh_s, sbh_s):
            ref[...] = jnp.zeros(ref.shape, BF16)

    @pl.when(pl.program_id(1) == 0)
    def _():
        stg_ref[...] = stg0_ref[...]
        sth_ref[...] = sth0_ref[...]

    def mod_rows(i):
        if bb == 1:
            return mods_ref[0, i:i + 1, :]
        return jnp.concatenate(
            [jnp.broadcast_to(mods_ref[bi, i:i + 1, :], (tt, D_MODEL)) for bi in range(bb)], axis=0)

    half = rows // 2
    x = x_ref[...].reshape(rows, D_MODEL)
    hb_s[...] = (_rms(x) * mod_rows(1) + mod_rows(0)).astype(BF16)

    def proj(c0, c1):
        return jnp.dot(hb_s[...], win_ref[:, c0:c1], preferred_element_type=F32)

    col_q, col_k, col_v, col_z, col_a = 0, K_LANES, 2 * K_LANES, 2 * K_LANES + D_MODEL, 2 * K_LANES + 2 * D_MODEL

    a_lr = proj(col_a, col_a + RANK_PAD)
    ga = jnp.dot(a_lr.astype(BF16), walpha_ref[...], preferred_element_type=F32) + balpha_ref[...]
    g_s[:, :GLA_K] = _log2_sigmoid(ga) * (1.0 / GATE_TAU)

    k_s[:, :GLA_K] = proj(col_k, col_k + GLA_K)
    lbl = lbl_ref[...]
    lmax = jnp.max(lbl, axis=0, keepdims=True)
    lexp = jnp.exp(lbl - lmax)
    lb = lexp[0:1, :] / jnp.sum(lexp, axis=0, keepdims=True)
    for n in range(HGRN_W // PAIR_V):
        cols = slice(GLA_K + n * PAIR_V, GLA_K + (n + 1) * PAIR_V)
        lb_n = lb[:, n * PAIR_V:(n + 1) * PAIR_V]
        f = lb_n + (1.0 - lb_n) * _sigmoid(proj(col_k + cols.start, col_k + cols.stop))
        k_s[:, cols] = 1.0 - f
        g_s[:, cols] = jnp.log2(f)

    q_s[:, :GLA_K] = proj(col_q, col_q + GLA_K) * (GLA_DK ** -0.5)
    for n in range(HGRN_W // PAIR_V):
        cols = slice(GLA_K + n * PAIR_V, GLA_K + (n + 1) * PAIR_V)
        qh = proj(col_q + cols.start, col_q + cols.stop)
        q_s[:, cols] = qh * _sigmoid(qh)

    def project_values():
        for p in range(N_PAIRS):
            pv = proj(col_v + p * PAIR_V, col_v + (p + 1) * PAIR_V).astype(BF16)
            for ci in range(n_chunks):
                r = slice(ci * CHUNK, (ci + 1) * CHUNK)
                vbd_s[p, ci, :CHUNK, :LANES] = pv[r, :LANES]
                vbd_s[p, ci, CHUNK:, LANES:] = pv[r, LANES:]

    def project_gates(n):
        cols = slice(n * PAIR_V, (n + 1) * PAIR_V)
        pz = proj(col_z + n * PAIR_V, col_z + (n + 1) * PAIR_V)
        zs_s[:, cols] = pz * _sigmoid(pz)

    row_i = lax.broadcasted_iota(jnp.int32, (CHUNK, LANES), 0)
    lane_i = lax.broadcasted_iota(jnp.int32, (CHUNK, LANES), 1)
    head0 = lane_i < GLA_DK

    def tile_base(ci, lt):
        r = slice(ci * CHUNK, (ci + 1) * CHUNK)
        lanes = slice(lt * LANES, (lt + 1) * LANES)
        b_t = b_s.at[ci, lt]
        g = g_s[r, lanes]
        q = q_s[r, lanes]
        k = k_s[r, lanes]
        b = g
        for sh in (1, 2, 4):
            b = b + jnp.where(row_i >= sh, pltpu.roll(b, sh, 0), 0.0)
        for sh in (8, 16, 32):
            b = b + jnp.concatenate([jnp.zeros((sh, LANES), F32), b[:CHUNK - sh]], axis=0)
        b_t[...] = b
        b_last = b_t[CHUNK - 1:CHUNK, :]
        dcol_s[ci, lt] = jnp.exp2(jnp.broadcast_to(b_last, (LANES, LANES)).T)
        qst_s[r, lanes] = (q * jnp.exp2(b)).astype(BF16)
        k_state = k * jnp.exp2(b_last - b)
        if lt < N_GLA_PAIRS:
            ksg_s[lt, ci, :CHUNK, :] = jnp.where(head0, k_state, 0.0).astype(BF16)
            ksg_s[lt, ci, CHUNK:, :] = jnp.where(head0, 0.0, k_state).astype(BF16)
        else:
            ksh_s[lt - N_GLA_PAIRS, ci] = k_state.astype(BF16)
        return k * jnp.exp2(-b)

    def tile_reload(ci, lt):
        r = slice(ci * CHUNK, (ci + 1) * CHUNK)
        lanes = slice(lt * LANES, (lt + 1) * LANES)
        b_t = b_s.at[ci, lt]
        return q_s[r, lanes], k_s[r, lanes], g_s[r, lanes], b_t[...], b_t

    def tile_level(base, lv):
        q, k, g, b, b_t = base
        if lv == N_LEVELS - 1:
            return q, k
        m = HALF_SIZES[lv]
        upper = (row_i & (2 * m - 1)) >= m
        if m == 1:
            lhs, rhs = q * jnp.exp2(g), k
        else:
            e = jnp.exp2(-jnp.abs(b - _ref_rows(b_t, 2 * m, m - 1)))
            lhs, rhs = q * e, k * e
        return jnp.where(upper, lhs, 0.0), jnp.where(upper, 0.0, rhs)

    def store_keys_t(dst, p, rhs_tiles):
        if p < N_GLA_PAIRS:
            rhs = rhs_tiles[0]
            stacked = jnp.concatenate([jnp.where(head0, rhs, 0.0), jnp.where(head0, 0.0, rhs)], axis=0)
            dst[...] = stacked.astype(BF16).T
        else:
            kt = jnp.concatenate(rhs_tiles, axis=0).astype(BF16).T
            dst[:LANES, :CHUNK] = kt[:, :CHUNK]
            dst[LANES:, CHUNK:] = kt[:, CHUNK:]

    def tiles_of_pair(p):
        if p < N_GLA_PAIRS:
            return (p,)
        return tuple(N_GLA_PAIRS + 2 * (p - N_GLA_PAIRS) + pos for pos in range(2))

    def pair_lanes(p):
        if p < N_GLA_PAIRS:
            return slice(p * LANES, (p + 1) * LANES)
        return slice(GLA_K + (p - N_GLA_PAIRS) * PAIR_V, GLA_K + (p - N_GLA_PAIRS + 1) * PAIR_V)

    def anchor_keys(ci, p):
        return kag_s.at[p, ci] if p < N_GLA_PAIRS else kah_s.at[p - N_GLA_PAIRS, ci]

    def level_keys(lv, p, ci):
        return klg_s.at[lv, p, ci] if p < N_GLA_PAIRS else klh_s.at[lv, p - N_GLA_PAIRS, ci]

    def prep_pair(ci, p):
        store_keys_t(anchor_keys(ci, p), p, [tile_base(ci, lt) for lt in tiles_of_pair(p)])

    def prep_pair_levels(ci, p):
        tiles = tiles_of_pair(p)
        bases = [tile_reload(ci, lt) for lt in tiles]
        for lv in range(N_LEVELS):
            rhs_tiles = []
            for lt, base in zip(tiles, bases):
                lhs, rhs = tile_level(base, lv)
                lql_s[lv, ci * CHUNK:(ci + 1) * CHUNK, lt * LANES:(lt + 1) * LANES] = lhs.astype(BF16)
                rhs_tiles.append(rhs)
            store_keys_t(level_keys(lv, p, ci), p, rhs_tiles)

    def pair_attention(ci, p, single_anchor):
        lanes = pair_lanes(p)
        if single_anchor:
            prod = jnp.dot(qst_s[ci * CHUNK:(ci + 1) * CHUNK, lanes], anchor_keys(ci, p)[...],
                           preferred_element_type=F32)
            return jnp.where(masks_ref[MASK_CAUSAL] > 0.0, prod, 0.0).astype(BF16)
        att = None
        for lv in range(N_LEVELS):
            prod = jnp.dot(lql_s[lv, ci * CHUNK:(ci + 1) * CHUNK, lanes], level_keys(lv, p, ci)[...],
                           preferred_element_type=F32)
            if lv > 0:
                prod = prod * masks_ref[lv]
            att = prod if att is None else att + prod
        return att.astype(BF16)

    def snapshot_states(ci):
        bi = ci // chunks_per_seq
        for p in range(N_GLA_PAIRS):
            sbg_s[ci, p] = stg_ref[bi, p].astype(BF16)
        for hh in range(HGRN_HEADS):
            pos = hh % 2
            sbh_s[ci, hh // 2, pos * LANES:(pos + 1) * LANES, pos * LANES:(pos + 1) * LANES] = (
                sth_ref[bi, hh].astype(BF16))

    def update_states(ci):
        bi = ci // chunks_per_seq
        for p in range(N_GLA_PAIRS):
            dcol = dcol_s[ci, p]
            stg_ref[bi, p] = (stg_ref[bi, p] * jnp.concatenate([dcol, dcol], axis=1)
                              + lax.dot_general(ksg_s[p, ci], vbd_s[p, ci], TN, preferred_element_type=F32))
        for hh in range(HGRN_HEADS):
            pg, pos = N_GLA_PAIRS + hh // 2, hh % 2
            v_head = vbd_s[pg, ci, pos * CHUNK:(pos + 1) * CHUNK, pos * LANES:(pos + 1) * LANES]
            sth_ref[bi, hh] = sth_ref[bi, hh] * dcol_s[ci, N_GLA_PAIRS + hh] + lax.dot_general(
                ksh_s[hh, ci], v_head, TN, preferred_element_type=F32)

    def pair_output(ci, p, att):
        r = slice(ci * CHUNK, (ci + 1) * CHUNK)
        state = sbg_s[ci, p] if p < N_GLA_PAIRS else sbh_s[ci, p - N_GLA_PAIRS]
        o_s[r, p * PAIR_V:(p + 1) * PAIR_V] = (
            jnp.dot(att, vbd_s[p, ci], preferred_element_type=F32)
            + jnp.dot(qst_s[r, pair_lanes(p)], state, preferred_element_type=F32))

    gate_rows = mod_rows(2)

    def gate_half(h):
        rr = slice(h * half, (h + 1) * half)
        for ht in range(GLA_HEADS + HGRN_HEADS):
            lanes = slice(ht * LANES, (ht + 1) * LANES)
            gb_s[rr, lanes] = (_rms(o_s[rr, lanes]) * gon_ref[:, lanes] * zs_s[rr, lanes]).astype(BF16)

    def project_half(h, n):
        rr = slice(h * half, (h + 1) * half)
        cols = slice(n * PAIR_V, (n + 1) * PAIR_V)
        po_s[rr, cols] = jnp.dot(gb_s[rr, :], wout_ref[:, cols], preferred_element_type=F32)

    def post_half(h):
        rr = slice(h * half, (h + 1) * half)
        gate_h = gate_rows if bb == 1 else gate_rows[rr]
        res = gate_h * (_rms(po_s[rr, :]) * gpost_ref[...])
        if bb == 1:
            y_ref[0, rr, :] = x_ref[0, rr, :] + res
        else:
            seqs = slice(h * (bb // 2), (h + 1) * (bb // 2))
            y_ref[seqs] = x_ref[seqs] + res.reshape(bb // 2, tt, D_MODEL)

    def finish_stages(h):
        return ([functools.partial(project_half, h, n) for n in range(N_COL_BLOCKS)]
                + [functools.partial(post_half, h)])

    fillers = [functools.partial(project_gates, n) for n in range(N_COL_BLOCKS)]
    atts = {}
    for step in range(n_chunks + 2):
        if step < n_chunks:
            for p in range(N_PAIRS):
                prep_pair(step, p)
        if step == 0:
            project_values()
        if 1 <= step <= n_chunks:
            snapshot_states(step - 1)
            if step < n_chunks:
                update_states(step - 1)
            atts[step - 1] = [pair_attention(step - 1, p, True) for p in range(N_PAIRS)]
        if fillers:
            fillers.pop(0)()
        if step >= 2:
            for p in range(N_PAIRS):
                pair_output(step - 2, p, atts[step - 2][p])
            if (step - 1) * CHUNK == half:
                while fillers:
                    fillers.pop(0)()
                gate_half(0)
                fillers = finish_stages(0)
    last_stages = finish_stages(1)
    for stage in ([functools.partial(gate_half, 1)] + fillers + last_stages[:-1]
                  + [functools.partial(update_states, n_chunks - 1)] + last_stages[-1:]):
        stage()

    total = None
    for ci in range(n_chunks):
        t = jnp.sum(g_s[ci * CHUNK:(ci + 1) * CHUNK, :], axis=0, keepdims=True)
        total = t if total is None else jnp.minimum(total, t)
    mild = jnp.min(total) >= -SINGLE_ANCHOR_MAX_LOG2

    @pl.when(jnp.logical_not(mild))
    def _():
        for ci in range(n_chunks):
            for p in range(N_PAIRS):
                prep_pair_levels(ci, p)
            for p in range(N_PAIRS):
                pair_output(ci, p, pair_attention(ci, p, False))
        for h in range(2):
            for stage in [functools.partial(gate_half, h)] + finish_stages(h):
                stage()


def _mixer_call(x, mods, stg0, sth0, w_in, w_alpha, b_alpha, lb_logits, g_on, w_out, g_post, masks, *, bb, tt):
    nb, seq, _ = x.shape
    rows = bb * tt
    assert nb % bb == 0 and seq % tt == 0 and tt % CHUNK == 0
    n_chunks = rows // CHUNK
    in_cols = w_in.shape[1]
    per_seq4 = lambda b, j: (b, 0, 0, 0)

    def resident(shape):
        return pl.BlockSpec(shape, lambda b, j: (0,) * len(shape), pipeline_mode=pl.Buffered(1))

    kernel = functools.partial(_mixer_kernel, bb=bb, tt=tt)
    stg_shape = (nb, N_GLA_PAIRS, LANES, PAIR_V)
    sth_shape = (nb, HGRN_HEADS, HGRN_D, HGRN_D)
    return pl.pallas_call(
        kernel,
        grid=(nb // bb, seq // tt),
        in_specs=[
            pl.BlockSpec((bb, tt, D_MODEL), lambda b, j: (b, j, 0)),
            pl.BlockSpec((bb, 3, D_MODEL), lambda b, j: (b, 0, 0)),
            pl.BlockSpec((bb,) + stg_shape[1:], per_seq4),
            pl.BlockSpec((bb,) + sth_shape[1:], per_seq4),
            resident((D_MODEL, in_cols)),
            resident((RANK_PAD, GLA_K)),
            resident((1, GLA_K)),
            resident(lb_logits.shape),
            resident((1, D_MODEL)),
            resident((D_MODEL, D_MODEL)),
            resident((1, D_MODEL)),
            resident(masks.shape),
        ],
        out_specs=[
            pl.BlockSpec((bb, tt, D_MODEL), lambda b, j: (b, j, 0)),
            pl.BlockSpec((bb,) + stg_shape[1:], per_seq4),
            pl.BlockSpec((bb,) + sth_shape[1:], per_seq4),
        ],
        out_shape=[
            jax.ShapeDtypeStruct(x.shape, F32),
            jax.ShapeDtypeStruct(stg_shape, F32),
            jax.ShapeDtypeStruct(sth_shape, F32),
        ],
        scratch_shapes=[
            pltpu.VMEM((rows, D_MODEL), BF16),
            pltpu.VMEM((rows, K_LANES), F32),
            pltpu.VMEM((rows, K_LANES), F32),
            pltpu.VMEM((rows, K_LANES), F32),
            pltpu.VMEM((N_PAIRS, n_chunks, 2 * CHUNK, PAIR_V), BF16),
            pltpu.VMEM((rows, D_MODEL), F32),
            pltpu.VMEM((rows, D_MODEL), F32),
            pltpu.VMEM((n_chunks, N_K_TILES, CHUNK, LANES), F32),
            pltpu.VMEM((rows, K_LANES), BF16),
            pltpu.VMEM((N_GLA_PAIRS, n_chunks, LANES, 2 * CHUNK), BF16),
            pltpu.VMEM((N_HGRN_PAIRS, n_chunks, PAIR_V, 2 * CHUNK), BF16),
            pltpu.VMEM((N_GLA_PAIRS, n_chunks, 2 * CHUNK, LANES), BF16),
            pltpu.VMEM((HGRN_HEADS, n_chunks, CHUNK, LANES), BF16),
            pltpu.VMEM((n_chunks, N_K_TILES, LANES, LANES), F32),
            pltpu.VMEM((n_chunks, N_GLA_PAIRS, LANES, PAIR_V), BF16),
            pltpu.VMEM((n_chunks, N_HGRN_PAIRS, PAIR_V, PAIR_V), BF16),
            pltpu.VMEM((N_LEVELS, rows, K_LANES), BF16),
            pltpu.VMEM((N_LEVELS, N_GLA_PAIRS, n_chunks, LANES, 2 * CHUNK), BF16),
            pltpu.VMEM((N_LEVELS, N_HGRN_PAIRS, n_chunks, PAIR_V, 2 * CHUNK), BF16),
            pltpu.VMEM((rows, D_MODEL), BF16),
            pltpu.VMEM((rows, D_MODEL), F32),
        ],
        compiler_params=pltpu.CompilerParams(
            dimension_semantics=("arbitrary", "arbitrary"),
            vmem_limit_bytes=VMEM_LIMIT_BYTES,
        ),
        name="mixer_bb%d_tt%d" % (bb, tt),
    )(x, mods, stg0, sth0, w_in, w_alpha, b_alpha, lb_logits, g_on, w_out, g_post, masks)


def _pair_gla_states(s_gla):
    nb = s_gla.shape[0]
    a = s_gla.reshape(nb, N_GLA_PAIRS, 2, GLA_DK, GLA_DV)
    z = jnp.zeros_like(a[:, :, 0])
    return jnp.concatenate([jnp.concatenate([a[:, :, 0], z], axis=-1),
                            jnp.concatenate([z, a[:, :, 1]], axis=-1)], axis=2)


def _unpair_gla_states(stg):
    nb = stg.shape[0]
    heads = [stg[:, :, h * GLA_DK:(h + 1) * GLA_DK, h * GLA_DV:(h + 1) * GLA_DV] for h in range(2)]
    return jnp.stack(heads, axis=2).reshape(nb, GLA_HEADS, GLA_DK, GLA_DV)


def kernel(x_prompt, x_sample, c_prompt, c_sample, state_gla, state_hgrn, w_ada, b_ada, g_pre, w_in,
           w_alpha, b_alpha, g_onorm_gla, hgrn_lb_logits, g_onorm_hgrn, w_out, g_post):
    assert w_ada.shape[0] == 1, "single-layer problem"
    bp = x_prompt.shape[0]

    o_qa, o_ka, o_va, o_za = 0, GLA_K, 2 * GLA_K, 2 * GLA_K + GLA_V
    o_al = o_za + GLA_V
    o_qh = o_al + GATE_RANK
    o_fh, o_ih, o_zh = o_qh + HGRN_W, o_qh + 2 * HGRN_W, o_qh + 3 * HGRN_W
    w = w_in[0]
    cols = lambda o, n: w[:, o:o + n].astype(BF16)
    w_in_r = jnp.concatenate([
        cols(o_qa, GLA_K), cols(o_qh, HGRN_W), cols(o_ka, GLA_K), cols(o_fh, HGRN_W),
        cols(o_va, GLA_V), cols(o_ih, HGRN_W), cols(o_za, GLA_V), cols(o_zh, HGRN_W),
        cols(o_al, GATE_RANK), jnp.zeros((D_MODEL, RANK_PAD - GATE_RANK), BF16)], axis=1)
    w_alpha_p = jnp.concatenate(
        [w_alpha[0], jnp.zeros((RANK_PAD - GATE_RANK, GLA_K), w_alpha.dtype)], axis=0).astype(BF16)
    g_on = jnp.concatenate([jnp.tile(g_onorm_gla[0], GLA_HEADS),
                            jnp.tile(g_onorm_hgrn[0], HGRN_HEADS)])[None, :]
    masks = jnp.asarray(_level_masks())

    c_all = jnp.concatenate([c_prompt, c_sample], axis=0)
    mods = _ada_call(c_all, w_ada[0], b_ada, g_pre).reshape(c_all.shape[0], 3, D_MODEL)

    shared = (w_in_r, w_alpha_p, b_alpha, hgrn_lb_logits, g_on, w_out[0].astype(BF16), g_post, masks)
    stg0_p = jnp.zeros((bp, N_GLA_PAIRS, LANES, PAIR_V), F32)
    sth0_p = jnp.zeros((bp, HGRN_HEADS, HGRN_D, HGRN_D), F32)
    yp, stg_p, hgrn_p = _mixer_call(x_prompt, mods[:bp], stg0_p, sth0_p, *shared, bb=1, tt=ROWS_PER_STEP)
    ys, stg_s, hgrn_s = _mixer_call(x_sample, mods[bp:], _pair_gla_states(state_gla[0]), state_hgrn[0],
                                    *shared, bb=ROWS_PER_STEP // x_sample.shape[1], tt=x_sample.shape[1])
    return (yp, ys, _unpair_gla_states(stg_p)[None], hgrn_p[None],
            _unpair_gla_states(stg_s)[None].astype(state_gla.dtype), hgrn_s[None].astype(state_hgrn.dtype))
```

```python
import functools
import math

import numpy as np
import jax
import jax.numpy as jnp
from jax import lax
from jax.experimental import pallas as pl
from jax.experimental.pallas import tpu as pltpu

D_MODEL = 1024
CHUNK = 64
GLA_HEADS = 4
GLA_V = D_MODEL // 2
GLA_K = GLA_V // 2
GLA_DK = GLA_K // GLA_HEADS
GLA_DV = GLA_V // GLA_HEADS
GATE_RANK = 16
GATE_TAU = 16.0
HGRN_HEADS = 4
HGRN_W = D_MODEL - GLA_V
HGRN_D = HGRN_W // HGRN_HEADS
IN_SPLITS = (GLA_K, GLA_K, GLA_V, GLA_V, GATE_RANK, HGRN_W, HGRN_W, HGRN_W, HGRN_W)
EPS = 1e-6
LOG2E = math.log2(math.e)

LANES = 128
SUBLANES = 8
K_LANES = GLA_K + HGRN_W
N_K_TILES = K_LANES // LANES
N_GLA_PAIRS = GLA_HEADS // 2
N_HGRN_PAIRS = HGRN_HEADS // 2
N_PAIRS = N_GLA_PAIRS + N_HGRN_PAIRS
PAIR_V = 2 * LANES
N_COL_BLOCKS = D_MODEL // PAIR_V
RANK_PAD = LANES
HALF_SIZES = (32, 16, 8, 4, 2, 1)
N_LEVELS = len(HALF_SIZES) + 1
MASK_CAUSAL = N_LEVELS
SINGLE_ANCHOR_MAX_LOG2 = 90.0
ROWS_PER_STEP = 256
VMEM_LIMIT_BYTES = 56 * 1024 * 1024

F32 = jnp.float32
BF16 = jnp.bfloat16
NT = (((1,), (1,)), ((), ()))
TN = (((0,), (0,)), ((), ()))


def _level_masks():
    t = np.arange(CHUNK)[:, None]
    s = np.arange(CHUNK)[None, :]
    masks = [(t // (2 * m)) == (s // (2 * m)) for m in HALF_SIZES]
    masks.append(t == s)
    masks.append(t >= s)
    masks = np.stack(masks).astype(np.float32)
    return np.concatenate([masks, masks], axis=2)


def _rms(x):
    return x * lax.rsqrt(jnp.mean(x * x, axis=-1, keepdims=True) + EPS)


def _sigmoid(x):
    return 1.0 / (1.0 + jnp.exp(-x))


def _log2_sigmoid(x):
    return jnp.minimum(x, 0.0) * LOG2E - jnp.log2(1.0 + jnp.exp(-jnp.abs(x)))


def _ada_kernel(c_ref, w_ref, b_ref, gpre_ref, out_ref):
    n = pl.program_id(0)
    val = jnp.dot(c_ref[...].astype(BF16), w_ref[...].astype(BF16),
                  preferred_element_type=F32) + b_ref[...]
    out_ref[...] = jnp.where(n == 1, gpre_ref[...] * (1.0 + val), val)


def _ada_call(c_all, w_ada, b_ada, g_pre):
    nb = c_all.shape[0]
    return pl.pallas_call(
        _ada_kernel,
        grid=(3,),
        in_specs=[
            pl.BlockSpec((nb, D_MODEL), lambda n: (0, 0)),
            pl.BlockSpec((D_MODEL, D_MODEL), lambda n: (0, n)),
            pl.BlockSpec((1, D_MODEL), lambda n: (0, n)),
            pl.BlockSpec((1, D_MODEL), lambda n: (0, 0)),
        ],
        out_specs=pl.BlockSpec((nb, D_MODEL), lambda n: (0, n)),
        out_shape=jax.ShapeDtypeStruct((nb, 3 * D_MODEL), F32),
        name="ada_mod",
    )(c_all, w_ada, b_ada, g_pre)


def _ref_rows(b_ref, block, off):
    pieces = []
    if block >= SUBLANES:
        for j in range(CHUNK // block):
            row = b_ref[j * block + off:j * block + off + 1, :]
            pieces.append(jnp.broadcast_to(row, (block, LANES)))
    else:
        assert block == SUBLANES // 2
        sub = lax.broadcasted_iota(jnp.int32, (SUBLANES, LANES), 0)
        for i in range(CHUNK // SUBLANES):
            lo = b_ref[SUBLANES * i + off:SUBLANES * i + off + 1, :]
            hi = b_ref[SUBLANES * i + block + off:SUBLANES * i + block + off + 1, :]
            pieces.append(jnp.where(sub < block,
                                    jnp.broadcast_to(lo, (SUBLANES, LANES)),
                                    jnp.broadcast_to(hi, (SUBLANES, LANES))))
    return jnp.concatenate(pieces, axis=0)


def _mixer_kernel(x_ref, mods_ref, stg0_ref, sth0_ref,
                  wqa_ref, wka_ref, wva_ref, wza_ref, wal_ref, wqh_ref, wfh_ref, wih_ref, wzh_ref,
                  walpha_ref, balpha_ref, lbl_ref,
                  gon_ref, wout_ref, gpost_ref, masks_ref,
                  y_ref, stg_ref, sth_ref,
                  hb_s, q_s, k_s, g_s, vbd_s, zs_s, o_s, b_s, qst_s, kag_s, kah_s, ksg_s, ksh_s, dcol_s, sbg_s, sbh_s,
                  lql_s, klg_s, klh_s, gb_s, po_s,
                  *, bb, tt):
    rows = bb * tt
    n_chunks = rows // CHUNK
    chunks_per_seq = tt // CHUNK

    @pl.when((pl.program_id(0) == 0) & (pl.program_id(1) == 0))
    def _():
        for ref in (vbd_s, kah_s, klh_s, sbh_s):
            ref[...] = jnp.zeros(ref.shape, BF16)

    @pl.when(pl.program_id(1) == 0)
    def _():
        stg_ref[...] = stg0_ref[...]
        sth_ref[...] = sth0_ref[...]

    def mod_rows(i):
        if bb == 1:
            return mods_ref[0, i:i + 1, :]
        return jnp.concatenate(
            [jnp.broadcast_to(mods_ref[bi, i:i + 1, :], (tt, D_MODEL)) for bi in range(bb)], axis=0)

    half = rows // 2
    x = x_ref[...].reshape(rows, D_MODEL)
    hb_s[...] = (_rms(x) * mod_rows(1) + mod_rows(0)).astype(BF16)

    def proj(w_ref, n=0):
        width = min(PAIR_V, w_ref.shape[1])
        return jnp.dot(hb_s[...], w_ref[:, n * width:(n + 1) * width], preferred_element_type=F32)

    def head_group_block(gla_ref, hgrn_ref, n):
        n_gla = GLA_V // PAIR_V
        return proj(gla_ref, n) if n < n_gla else proj(hgrn_ref, n - n_gla)

    a_lr = proj(wal_ref)
    ga = jnp.dot(a_lr.astype(BF16), walpha_ref[...], preferred_element_type=F32) + balpha_ref[...]
    g_s[:, :GLA_K] = _log2_sigmoid(ga) * (1.0 / GATE_TAU)

    k_s[:, :GLA_K] = proj(wka_ref)
    lbl = lbl_ref[...]
    lmax = jnp.max(lbl, axis=0, keepdims=True)
    lexp = jnp.exp(lbl - lmax)
    lb = lexp[0:1, :] / jnp.sum(lexp, axis=0, keepdims=True)
    for n in range(HGRN_W // PAIR_V):
        cols = slice(GLA_K + n * PAIR_V, GLA_K + (n + 1) * PAIR_V)
        lb_n = lb[:, n * PAIR_V:(n + 1) * PAIR_V]
        f = lb_n + (1.0 - lb_n) * _sigmoid(proj(wfh_ref, n))
        k_s[:, cols] = 1.0 - f
        g_s[:, cols] = jnp.log2(f)

    q_s[:, :GLA_K] = proj(wqa_ref) * (GLA_DK ** -0.5)
    for n in range(HGRN_W // PAIR_V):
        cols = slice(GLA_K + n * PAIR_V, GLA_K + (n + 1) * PAIR_V)
        qh = proj(wqh_ref, n)
        q_s[:, cols] = qh * _sigmoid(qh)

    def project_values():
        for p in range(N_PAIRS):
            pv = head_group_block(wva_ref, wih_ref, p).astype(BF16)
            for ci in range(n_chunks):
                r = slice(ci * CHUNK, (ci + 1) * CHUNK)
                vbd_s[p, ci, :CHUNK, :LANES] = pv[r, :LANES]
                vbd_s[p, ci, CHUNK:, LANES:] = pv[r, LANES:]

    def project_gates(n):
        cols = slice(n * PAIR_V, (n + 1) * PAIR_V)
        pz = head_group_block(wza_ref, wzh_ref, n)
        zs_s[:, cols] = pz * _sigmoid(pz)

    row_i = lax.broadcasted_iota(jnp.int32, (CHUNK, LANES), 0)
    lane_i = lax.broadcasted_iota(jnp.int32, (CHUNK, LANES), 1)
    head0 = lane_i < GLA_DK

    def tile_base(ci, lt):
        r = slice(ci * CHUNK, (ci + 1) * CHUNK)
        lanes = slice(lt * LANES, (lt + 1) * LANES)
        b_t = b_s.at[ci, lt]
        g = g_s[r, lanes]
        q = q_s[r, lanes]
        k = k_s[r, lanes]
        b = g
        for sh in (1, 2, 4):
            b = b + jnp.where(row_i >= sh, pltpu.roll(b, sh, 0), 0.0)
        for sh in (8, 16, 32):
            b = b + jnp.concatenate([jnp.zeros((sh, LANES), F32), b[:CHUNK - sh]], axis=0)
        b_t[...] = b
        b_last = b_t[CHUNK - 1:CHUNK, :]
        dcol_s[ci, lt] = jnp.exp2(jnp.broadcast_to(b_last, (LANES, LANES)).T)
        qst_s[r, lanes] = (q * jnp.exp2(b)).astype(BF16)
        k_state = k * jnp.exp2(b_last - b)
        if lt < N_GLA_PAIRS:
            ksg_s[lt, ci, :CHUNK, :] = jnp.where(head0, k_state, 0.0).astype(BF16)
            ksg_s[lt, ci, CHUNK:, :] = jnp.where(head0, 0.0, k_state).astype(BF16)
        else:
            ksh_s[lt - N_GLA_PAIRS, ci] = k_state.astype(BF16)
        return k * jnp.exp2(-b)

    def tile_reload(ci, lt):
        r = slice(ci * CHUNK, (ci + 1) * CHUNK)
        lanes = slice(lt * LANES, (lt + 1) * LANES)
        b_t = b_s.at[ci, lt]
        return q_s[r, lanes], k_s[r, lanes], g_s[r, lanes], b_t[...], b_t

    def tile_level(base, lv):
        q, k, g, b, b_t = base
        if lv == N_LEVELS - 1:
            return q, k
        m = HALF_SIZES[lv]
        upper = (row_i & (2 * m - 1)) >= m
        if m == 1:
            lhs, rhs = q * jnp.exp2(g), k
        else:
            e = jnp.exp2(-jnp.abs(b - _ref_rows(b_t, 2 * m, m - 1)))
            lhs, rhs = q * e, k * e
        return jnp.where(upper, lhs, 0.0), jnp.where(upper, 0.0, rhs)

    def store_keys_t(dst, p, rhs_tiles):
        if p < N_GLA_PAIRS:
            rhs = rhs_tiles[0]
            stacked = jnp.concatenate([jnp.where(head0, rhs, 0.0), jnp.where(head0, 0.0, rhs)], axis=0)
            dst[...] = stacked.astype(BF16).T
        else:
            kt = jnp.concatenate(rhs_tiles, axis=0).astype(BF16).T
            dst[:LANES, :CHUNK] = kt[:, :CHUNK]
            dst[LANES:, CHUNK:] = kt[:, CHUNK:]

    def tiles_of_pair(p):
        if p < N_GLA_PAIRS:
            return (p,)
        return tuple(N_GLA_PAIRS + 2 * (p - N_GLA_PAIRS) + pos for pos in range(2))

    def pair_lanes(p):
        if p < N_GLA_PAIRS:
            return slice(p * LANES, (p + 1) * LANES)
        return slice(GLA_K + (p - N_GLA_PAIRS) * PAIR_V, GLA_K + (p - N_GLA_PAIRS + 1) * PAIR_V)

    def anchor_keys(ci, p):
        return kag_s.at[p, ci] if p < N_GLA_PAIRS else kah_s.at[p - N_GLA_PAIRS, ci]

    def level_keys(lv, p, ci):
        return klg_s.at[lv, p, ci] if p < N_GLA_PAIRS else klh_s.at[lv, p - N_GLA_PAIRS, ci]

    def prep_pair(ci, p):
        store_keys_t(anchor_keys(ci, p), p, [tile_base(ci, lt) for lt in tiles_of_pair(p)])

    def prep_pair_levels(ci, p):
        tiles = tiles_of_pair(p)
        bases = [tile_reload(ci, lt) for lt in tiles]
        for lv in range(N_LEVELS):
            rhs_tiles = []
            for lt, base in zip(tiles, bases):
                lhs, rhs = tile_level(base, lv)
                lql_s[lv, ci * CHUNK:(ci + 1) * CHUNK, lt * LANES:(lt + 1) * LANES] = lhs.astype(BF16)
                rhs_tiles.append(rhs)
            store_keys_t(level_keys(lv, p, ci), p, rhs_tiles)

    def pair_attention(ci, p, single_anchor):
        lanes = pair_lanes(p)
        if single_anchor:
            prod = jnp.dot(qst_s[ci * CHUNK:(ci + 1) * CHUNK, lanes], anchor_keys(ci, p)[...],
                           preferred_element_type=F32)
            return jnp.where(masks_ref[MASK_CAUSAL] > 0.0, prod, 0.0).astype(BF16)
        att = None
        for lv in range(N_LEVELS):
            prod = jnp.dot(lql_s[lv, ci * CHUNK:(ci + 1) * CHUNK, lanes], level_keys(lv, p, ci)[...],
                           preferred_element_type=F32)
            if lv > 0:
                prod = prod * masks_ref[lv]
            att = prod if att is None else att + prod
        return att.astype(BF16)

    def snapshot_states(ci):
        bi = ci // chunks_per_seq
        for p in range(N_GLA_PAIRS):
            sbg_s[ci, p] = stg_ref[bi, p].astype(BF16)
        for hh in range(HGRN_HEADS):
            pos = hh % 2
            sbh_s[ci, hh // 2, pos * LANES:(pos + 1) * LANES, pos * LANES:(pos + 1) * LANES] = (
                sth_ref[bi, hh].astype(BF16))

    def update_states(ci):
        bi = ci // chunks_per_seq
        for p in range(N_GLA_PAIRS):
            dcol = dcol_s[ci, p]
            stg_ref[bi, p] = (stg_ref[bi, p] * jnp.concatenate([dcol, dcol], axis=1)
                              + lax.dot_general(ksg_s[p, ci], vbd_s[p, ci], TN, preferred_element_type=F32))
        for hh in range(HGRN_HEADS):
            pg, pos = N_GLA_PAIRS + hh // 2, hh % 2
            v_head = vbd_s[pg, ci, pos * CHUNK:(pos + 1) * CHUNK, pos * LANES:(pos + 1) * LANES]
            sth_ref[bi, hh] = sth_ref[bi, hh] * dcol_s[ci, N_GLA_PAIRS + hh] + lax.dot_general(
                ksh_s[hh, ci], v_head, TN, preferred_element_type=F32)

    def pair_output(ci, p, att):
        r = slice(ci * CHUNK, (ci + 1) * CHUNK)
        state = sbg_s[ci, p] if p < N_GLA_PAIRS else sbh_s[ci, p - N_GLA_PAIRS]
        o_s[r, p * PAIR_V:(p + 1) * PAIR_V] = (
            jnp.dot(att, vbd_s[p, ci], preferred_element_type=F32)
            + jnp.dot(qst_s[r, pair_lanes(p)], state, preferred_element_type=F32))

    gate_rows = mod_rows(2)

    def gate_half(h):
        rr = slice(h * half, (h + 1) * half)
        for ht in range(GLA_HEADS + HGRN_HEADS):
            lanes = slice(ht * LANES, (ht + 1) * LANES)
            gb_s[rr, lanes] = (_rms(o_s[rr, lanes]) * gon_ref[:, lanes] * zs_s[rr, lanes]).astype(BF16)

    def project_half(h, n):
        rr = slice(h * half, (h + 1) * half)
        cols = slice(n * PAIR_V, (n + 1) * PAIR_V)
        po_s[rr, cols] = jnp.dot(gb_s[rr, :], wout_ref[:, cols], preferred_element_type=F32)

    def post_half(h):
        rr = slice(h * half, (h + 1) * half)
        gate_h = gate_rows if bb == 1 else gate_rows[rr]
        res = gate_h * (_rms(po_s[rr, :]) * gpost_ref[...])
        if bb == 1:
            y_ref[0, rr, :] = x_ref[0, rr, :] + res
        else:
            seqs = slice(h * (bb // 2), (h + 1) * (bb // 2))
            y_ref[seqs] = x_ref[seqs] + res.reshape(bb // 2, tt, D_MODEL)

    def finish_stages(h):
        return ([functools.partial(project_half, h, n) for n in range(N_COL_BLOCKS)]
                + [functools.partial(post_half, h)])

    fillers = [functools.partial(project_gates, n) for n in range(N_COL_BLOCKS)]
    atts = {}
    for step in range(n_chunks + 2):
        if step < n_chunks:
            for p in range(N_PAIRS):
                prep_pair(step, p)
        if step == 0:
            project_values()
        if 1 <= step <= n_chunks:
            snapshot_states(step - 1)
            if step < n_chunks:
                update_states(step - 1)
            atts[step - 1] = [pair_attention(step - 1, p, True) for p in range(N_PAIRS)]
        if fillers:
            fillers.pop(0)()
        if step >= 2:
            for p in range(N_PAIRS):
                pair_output(step - 2, p, atts[step - 2][p])
            if (step - 1) * CHUNK == half:
                while fillers:
                    fillers.pop(0)()
                gate_half(0)
                fillers = finish_stages(0)
    last_stages = finish_stages(1)
    for stage in ([functools.partial(gate_half, 1)] + fillers + last_stages[:-1]
                  + [functools.partial(update_states, n_chunks - 1)] + last_stages[-1:]):
        stage()

    total = None
    for ci in range(n_chunks):
        t = jnp.sum(g_s[ci * CHUNK:(ci + 1) * CHUNK, :], axis=0, keepdims=True)
        total = t if total is None else jnp.minimum(total, t)
    mild = jnp.min(total) >= -SINGLE_ANCHOR_MAX_LOG2

    @pl.when(jnp.logical_not(mild))
    def _():
        for ci in range(n_chunks):
            for p in range(N_PAIRS):
                prep_pair_levels(ci, p)
            for p in range(N_PAIRS):
                pair_output(ci, p, pair_attention(ci, p, False))
        for h in range(2):
            for stage in [functools.partial(gate_half, h)] + finish_stages(h):
                stage()


def _mixer_call(x, mods, stg0, sth0, w_in_parts, w_alpha, b_alpha, lb_logits, g_on, w_out, g_post, masks,
                *, bb, tt):
    nb, seq, _ = x.shape
    rows = bb * tt
    assert nb % bb == 0 and seq % tt == 0 and tt % CHUNK == 0
    n_chunks = rows // CHUNK
    per_seq4 = lambda b, j: (b, 0, 0, 0)

    def resident(shape):
        return pl.BlockSpec(shape, lambda b, j: (0,) * len(shape), pipeline_mode=pl.Buffered(1))

    kernel = functools.partial(_mixer_kernel, bb=bb, tt=tt)
    stg_shape = (nb, N_GLA_PAIRS, LANES, PAIR_V)
    sth_shape = (nb, HGRN_HEADS, HGRN_D, HGRN_D)
    return pl.pallas_call(
        kernel,
        grid=(nb // bb, seq // tt),
        in_specs=[
            pl.BlockSpec((bb, tt, D_MODEL), lambda b, j: (b, j, 0)),
            pl.BlockSpec((bb, 3, D_MODEL), lambda b, j: (b, 0, 0)),
            pl.BlockSpec((bb,) + stg_shape[1:], per_seq4),
            pl.BlockSpec((bb,) + sth_shape[1:], per_seq4),
            *[resident(part.shape) for part in w_in_parts],
            resident((RANK_PAD, GLA_K)),
            resident((1, GLA_K)),
            resident(lb_logits.shape),
            resident((1, D_MODEL)),
            resident((D_MODEL, D_MODEL)),
            resident((1, D_MODEL)),
            resident(masks.shape),
        ],
        out_specs=[
            pl.BlockSpec((bb, tt, D_MODEL), lambda b, j: (b, j, 0)),
            pl.BlockSpec((bb,) + stg_shape[1:], per_seq4),
            pl.BlockSpec((bb,) + sth_shape[1:], per_seq4),
        ],
        out_shape=[
            jax.ShapeDtypeStruct(x.shape, F32),
            jax.ShapeDtypeStruct(stg_shape, F32),
            jax.ShapeDtypeStruct(sth_shape, F32),
        ],
        scratch_shapes=[
            pltpu.VMEM((rows, D_MODEL), BF16),
            pltpu.VMEM((rows, K_LANES), F32),
            pltpu.VMEM((rows, K_LANES), F32),
            pltpu.VMEM((rows, K_LANES), F32),
            pltpu.VMEM((N_PAIRS, n_chunks, 2 * CHUNK, PAIR_V), BF16),
            pltpu.VMEM((rows, D_MODEL), F32),
            pltpu.VMEM((rows, D_MODEL), F32),
            pltpu.VMEM((n_chunks, N_K_TILES, CHUNK, LANES), F32),
            pltpu.VMEM((rows, K_LANES), BF16),
            pltpu.VMEM((N_GLA_PAIRS, n_chunks, LANES, 2 * CHUNK), BF16),
            pltpu.VMEM((N_HGRN_PAIRS, n_chunks, PAIR_V, 2 * CHUNK), BF16),
            pltpu.VMEM((N_GLA_PAIRS, n_chunks, 2 * CHUNK, LANES), BF16),
            pltpu.VMEM((HGRN_HEADS, n_chunks, CHUNK, LANES), BF16),
            pltpu.VMEM((n_chunks, N_K_TILES, LANES, LANES), F32),
            pltpu.VMEM((n_chunks, N_GLA_PAIRS, LANES, PAIR_V), BF16),
            pltpu.VMEM((n_chunks, N_HGRN_PAIRS, PAIR_V, PAIR_V), BF16),
            pltpu.VMEM((N_LEVELS, rows, K_LANES), BF16),
            pltpu.VMEM((N_LEVELS, N_GLA_PAIRS, n_chunks, LANES, 2 * CHUNK), BF16),
            pltpu.VMEM((N_LEVELS, N_HGRN_PAIRS, n_chunks, PAIR_V, 2 * CHUNK), BF16),
            pltpu.VMEM((rows, D_MODEL), BF16),
            pltpu.VMEM((rows, D_MODEL), F32),
        ],
        compiler_params=pltpu.CompilerParams(
            dimension_semantics=("arbitrary", "arbitrary"),
            vmem_limit_bytes=VMEM_LIMIT_BYTES,
        ),
        name="mixer_bb%d_tt%d" % (bb, tt),
    )(x, mods, stg0, sth0, *w_in_parts, w_alpha, b_alpha, lb_logits, g_on, w_out, g_post, masks)


def _pair_gla_states(s_gla):
    nb = s_gla.shape[0]
    a = s_gla.reshape(nb, N_GLA_PAIRS, 2, GLA_DK, GLA_DV)
    z = jnp.zeros_like(a[:, :, 0])
    return jnp.concatenate([jnp.concatenate([a[:, :, 0], z], axis=-1),
                            jnp.concatenate([z, a[:, :, 1]], axis=-1)], axis=2)


def _unpair_gla_states(stg):
    nb = stg.shape[0]
    heads = [stg[:, :, h * GLA_DK:(h + 1) * GLA_DK, h * GLA_DV:(h + 1) * GLA_DV] for h in range(2)]
    return jnp.stack(heads, axis=2).reshape(nb, GLA_HEADS, GLA_DK, GLA_DV)


def kernel(x_prompt, x_sample, c_prompt, c_sample, state_gla, state_hgrn, w_ada, b_ada, g_pre, w_in,
           w_alpha, b_alpha, g_onorm_gla, hgrn_lb_logits, g_onorm_hgrn, w_out, g_post):
    assert w_ada.shape[0] == 1, "single-layer problem"
    bp = x_prompt.shape[0]

    w_in_parts, offset = [], 0
    for width in IN_SPLITS:
        part = w_in[0][:, offset:offset + width].astype(BF16)
        if width < LANES:
            part = jnp.pad(part, ((0, 0), (0, RANK_PAD - width)))
        w_in_parts.append(part)
        offset += width
    w_alpha_p = jnp.concatenate(
        [w_alpha[0], jnp.zeros((RANK_PAD - GATE_RANK, GLA_K), w_alpha.dtype)], axis=0).astype(BF16)
    g_on = jnp.concatenate([jnp.tile(g_onorm_gla[0], GLA_HEADS),
                            jnp.tile(g_onorm_hgrn[0], HGRN_HEADS)])[None, :]
    masks = jnp.asarray(_level_masks())

    c_all = jnp.concatenate([c_prompt, c_sample], axis=0)
    mods = _ada_call(c_all, w_ada[0], b_ada, g_pre).reshape(c_all.shape[0], 3, D_MODEL)

    shared = (tuple(w_in_parts), w_alpha_p, b_alpha, hgrn_lb_logits, g_on, w_out[0].astype(BF16), g_post, masks)
    stg0_p = jnp.zeros((bp, N_GLA_PAIRS, LANES, PAIR_V), F32)
    sth0_p = jnp.zeros((bp, HGRN_HEADS, HGRN_D, HGRN_D), F32)
    yp, stg_p, hgrn_p = _mixer_call(x_prompt, mods[:bp], stg0_p, sth0_p, *shared, bb=1, tt=ROWS_PER_STEP)
    ys, stg_s, hgrn_s = _mixer_call(x_sample, mods[bp:], _pair_gla_states(state_gla[0]), state_hgrn[0],
                                    *shared, bb=ROWS_PER_STEP // x_sample.shape[1], tt=x_sample.shape[1])
    return (yp, ys, _unpair_gla_states(stg_p)[None], hgrn_p[None],
            _unpair_gla_states(stg_s)[None].astype(state_gla.dtype), hgrn_s[None].astype(state_hgrn.dtype))
```

```python
import functools
import math

import numpy as np
import jax
import jax.numpy as jnp
from jax import lax
from jax.experimental import pallas as pl
from jax.experimental.pallas import tpu as pltpu

D_MODEL = 1024
CHUNK = 64
GLA_HEADS = 4
GLA_V = D_MODEL // 2
GLA_K = GLA_V // 2
GLA_DK = GLA_K // GLA_HEADS
GLA_DV = GLA_V // GLA_HEADS
GATE_RANK = 16
GATE_TAU = 16.0
HGRN_HEADS = 4
HGRN_W = D_MODEL - GLA_V
HGRN_D = HGRN_W // HGRN_HEADS
IN_SPLITS = (GLA_K, GLA_K, GLA_V, GLA_V, GATE_RANK, HGRN_W, HGRN_W, HGRN_W, HGRN_W)
EPS = 1e-6
LOG2E = math.log2(math.e)

LANES = 128
SUBLANES = 8
K_LANES = GLA_K + HGRN_W
N_K_TILES = K_LANES // LANES
N_GLA_PAIRS = GLA_HEADS // 2
N_HGRN_PAIRS = HGRN_HEADS // 2
N_PAIRS = N_GLA_PAIRS + N_HGRN_PAIRS
PAIR_V = 2 * LANES
N_COL_BLOCKS = D_MODEL // PAIR_V
RANK_PAD = LANES
HALF_SIZES = (32, 16, 8, 4, 2, 1)
N_LEVELS = len(HALF_SIZES) + 1
MASK_CAUSAL = N_LEVELS
SINGLE_ANCHOR_MAX_LOG2 = 90.0
ROWS_PER_STEP = 256
VMEM_LIMIT_BYTES = 56 * 1024 * 1024

F32 = jnp.float32
BF16 = jnp.bfloat16
NT = (((1,), (1,)), ((), ()))
TN = (((0,), (0,)), ((), ()))


def _level_masks():
    t = np.arange(CHUNK)[:, None]
    s = np.arange(CHUNK)[None, :]
    masks = [(t // (2 * m)) == (s // (2 * m)) for m in HALF_SIZES]
    masks.append(t == s)
    masks.append(t >= s)
    masks = np.stack(masks).astype(np.float32)
    return np.concatenate([masks, masks], axis=2)


def _rms(x):
    return x * lax.rsqrt(jnp.mean(x * x, axis=-1, keepdims=True) + EPS)


def _sigmoid(x):
    return 1.0 / (1.0 + jnp.exp(-x))


def _log2_sigmoid(x):
    return jnp.minimum(x, 0.0) * LOG2E - jnp.log2(1.0 + jnp.exp(-jnp.abs(x)))


def _ada_kernel(c_ref, w_ref, b_ref, gpre_ref, out_ref):
    n = pl.program_id(0)
    val = jnp.dot(c_ref[...].astype(BF16), w_ref[...].astype(BF16),
                  preferred_element_type=F32) + b_ref[...]
    out_ref[...] = jnp.where(n == 1, gpre_ref[...] * (1.0 + val), val)


def _ada_call(c_all, w_ada, b_ada, g_pre):
    nb = c_all.shape[0]
    return pl.pallas_call(
        _ada_kernel,
        grid=(3,),
        in_specs=[
            pl.BlockSpec((nb, D_MODEL), lambda n: (0, 0)),
            pl.BlockSpec((D_MODEL, D_MODEL), lambda n: (0, n)),
            pl.BlockSpec((1, D_MODEL), lambda n: (0, n)),
            pl.BlockSpec((1, D_MODEL), lambda n: (0, 0)),
        ],
        out_specs=pl.BlockSpec((nb, D_MODEL), lambda n: (0, n)),
        out_shape=jax.ShapeDtypeStruct((nb, 3 * D_MODEL), F32),
        name="ada_mod",
    )(c_all, w_ada, b_ada, g_pre)


WEIGHT_ROWS_PER_STEP = 128


def _split_weights_kernel(w_ref, *part_refs):
    offset = 0
    for width, part_ref in zip(IN_SPLITS, part_refs):
        part = w_ref[:, offset:offset + width].astype(BF16)
        if width < part_ref.shape[1]:
            part_ref[...] = jnp.zeros(part_ref.shape, BF16)
            part_ref[:, :width] = part
        else:
            part_ref[...] = part
        offset += width


def _split_weights_call(w_in):
    rows, in_cols = w_in.shape
    widths = [max(width, RANK_PAD) for width in IN_SPLITS]
    return pl.pallas_call(
        _split_weights_kernel,
        grid=(rows // WEIGHT_ROWS_PER_STEP,),
        in_specs=[pl.BlockSpec((WEIGHT_ROWS_PER_STEP, in_cols), lambda i: (i, 0))],
        out_specs=[pl.BlockSpec((WEIGHT_ROWS_PER_STEP, width), lambda i: (i, 0)) for width in widths],
        out_shape=[jax.ShapeDtypeStruct((rows, width), BF16) for width in widths],
        name="split_weights",
    )(w_in)


def _ref_rows(b_ref, block, off):
    pieces = []
    if block >= SUBLANES:
        for j in range(CHUNK // block):
            row = b_ref[j * block + off:j * block + off + 1, :]
            pieces.append(jnp.broadcast_to(row, (block, LANES)))
    else:
        assert block == SUBLANES // 2
        sub = lax.broadcasted_iota(jnp.int32, (SUBLANES, LANES), 0)
        for i in range(CHUNK // SUBLANES):
            lo = b_ref[SUBLANES * i + off:SUBLANES * i + off + 1, :]
            hi = b_ref[SUBLANES * i + block + off:SUBLANES * i + block + off + 1, :]
            pieces.append(jnp.where(sub < block,
                                    jnp.broadcast_to(lo, (SUBLANES, LANES)),
                                    jnp.broadcast_to(hi, (SUBLANES, LANES))))
    return jnp.concatenate(pieces, axis=0)


def _mixer_kernel(x_ref, mods_ref, stg0_ref, sth0_ref,
                  wqa_ref, wka_ref, wva_ref, wza_ref, wal_ref, wqh_ref, wfh_ref, wih_ref, wzh_ref,
                  walpha_ref, balpha_ref, lbl_ref,
                  gon_ref, wout_ref, gpost_ref, masks_ref,
                  y_ref, stg_ref, sth_ref,
                  hb_s, q_s, k_s, g_s, vbd_s, zs_s, o_s, b_s, qst_s, kag_s, kah_s, ksg_s, ksh_s, dcol_s, sbg_s, sbh_s,
                  lql_s, klg_s, klh_s, gb_s, po_s,
                  *, bb, tt):
    rows = bb * tt
    n_chunks = rows // CHUNK
    chunks_per_seq = tt // CHUNK

    @pl.when((pl.program_id(0) == 0) & (pl.program_id(1) == 0))
    def _():
        for ref in (vbd_s, kah_s, klh_s, sbh_s):
            ref[...] = jnp.zeros(ref.shape, BF16)

    @pl.when(pl.program_id(1) == 0)
    def _():
        stg_ref[...] = stg0_ref[...]
        sth_ref[...] = sth0_ref[...]

    def mod_rows(i):
        if bb == 1:
            return mods_ref[0, i:i + 1, :]
        return jnp.concatenate(
            [jnp.broadcast_to(mods_ref[bi, i:i + 1, :], (tt, D_MODEL)) for bi in range(bb)], axis=0)

    half = rows // 2
    x = x_ref[...].reshape(rows, D_MODEL)
    hb_s[...] = (_rms(x) * mod_rows(1) + mod_rows(0)).astype(BF16)

    def proj(w_ref, n=0):
        width = min(PAIR_V, w_ref.shape[1])
        return jnp.dot(hb_s[...], w_ref[:, n * width:(n + 1) * width], preferred_element_type=F32)

    def head_group_block(gla_ref, hgrn_ref, n):
        n_gla = GLA_V // PAIR_V
        return proj(gla_ref, n) if n < n_gla else proj(hgrn_ref, n - n_gla)

    a_lr = proj(wal_ref)
    ga = jnp.dot(a_lr.astype(BF16), walpha_ref[...], preferred_element_type=F32) + balpha_ref[...]
    g_s[:, :GLA_K] = _log2_sigmoid(ga) * (1.0 / GATE_TAU)

    k_s[:, :GLA_K] = proj(wka_ref)
    lbl = lbl_ref[...]
    lmax = jnp.max(lbl, axis=0, keepdims=True)
    lexp = jnp.exp(lbl - lmax)
    lb = lexp[0:1, :] / jnp.sum(lexp, axis=0, keepdims=True)
    for n in range(HGRN_W // PAIR_V):
        cols = slice(GLA_K + n * PAIR_V, GLA_K + (n + 1) * PAIR_V)
        lb_n = lb[:, n * PAIR_V:(n + 1) * PAIR_V]
        f = lb_n + (1.0 - lb_n) * _sigmoid(proj(wfh_ref, n))
        k_s[:, cols] = 1.0 - f
        g_s[:, cols] = jnp.log2(f)

    q_s[:, :GLA_K] = proj(wqa_ref) * (GLA_DK ** -0.5)
    for n in range(HGRN_W // PAIR_V):
        cols = slice(GLA_K + n * PAIR_V, GLA_K + (n + 1) * PAIR_V)
        qh = proj(wqh_ref, n)
        q_s[:, cols] = qh * _sigmoid(qh)

    def project_values():
        for p in range(N_PAIRS):
            pv = head_group_block(wva_ref, wih_ref, p).astype(BF16)
            for ci in range(n_chunks):
                r = slice(ci * CHUNK, (ci + 1) * CHUNK)
                vbd_s[p, ci, :CHUNK, :LANES] = pv[r, :LANES]
                vbd_s[p, ci, CHUNK:, LANES:] = pv[r, LANES:]

    def project_gates(n):
        cols = slice(n * PAIR_V, (n + 1) * PAIR_V)
        pz = head_group_block(wza_ref, wzh_ref, n)
        zs_s[:, cols] = pz * _sigmoid(pz)

    row_i = lax.broadcasted_iota(jnp.int32, (CHUNK, LANES), 0)
    lane_i = lax.broadcasted_iota(jnp.int32, (CHUNK, LANES), 1)
    head0 = lane_i < GLA_DK

    def tile_base(ci, lt):
        r = slice(ci * CHUNK, (ci + 1) * CHUNK)
        lanes = slice(lt * LANES, (lt + 1) * LANES)
        b_t = b_s.at[ci, lt]
        g = g_s[r, lanes]
        q = q_s[r, lanes]
        k = k_s[r, lanes]
        b = g
        for sh in (1, 2, 4):
            b = b + jnp.where(row_i >= sh, pltpu.roll(b, sh, 0), 0.0)
        for sh in (8, 16, 32):
            b = b + jnp.concatenate([jnp.zeros((sh, LANES), F32), b[:CHUNK - sh]], axis=0)
        b_t[...] = b
        b_last = b_t[CHUNK - 1:CHUNK, :]
        dcol_s[ci, lt] = jnp.exp2(jnp.broadcast_to(b_last, (LANES, LANES)).T)
        qst_s[r, lanes] = (q * jnp.exp2(b)).astype(BF16)
        k_state = k * jnp.exp2(b_last - b)
        if lt < N_GLA_PAIRS:
            ksg_s[lt, ci, :CHUNK, :] = jnp.where(head0, k_state, 0.0).astype(BF16)
            ksg_s[lt, ci, CHUNK:, :] = jnp.where(head0, 0.0, k_state).astype(BF16)
        else:
            ksh_s[lt - N_GLA_PAIRS, ci] = k_state.astype(BF16)
        return k * jnp.exp2(-b)

    def tile_reload(ci, lt):
        r = slice(ci * CHUNK, (ci + 1) * CHUNK)
        lanes = slice(lt * LANES, (lt + 1) * LANES)
        b_t = b_s.at[ci, lt]
        return q_s[r, lanes], k_s[r, lanes], g_s[r, lanes], b_t[...], b_t

    def tile_level(base, lv):
        q, k, g, b, b_t = base
        if lv == N_LEVELS - 1:
            return q, k
        m = HALF_SIZES[lv]
        upper = (row_i & (2 * m - 1)) >= m
        if m == 1:
            lhs, rhs = q * jnp.exp2(g), k
        else:
            e = jnp.exp2(-jnp.abs(b - _ref_rows(b_t, 2 * m, m - 1)))
            lhs, rhs = q * e, k * e
        return jnp.where(upper, lhs, 0.0), jnp.where(upper, 0.0, rhs)

    def store_keys_t(dst, p, rhs_tiles):
        if p < N_GLA_PAIRS:
            rhs = rhs_tiles[0]
            stacked = jnp.concatenate([jnp.where(head0, rhs, 0.0), jnp.where(head0, 0.0, rhs)], axis=0)
            dst[...] = stacked.astype(BF16).T
        else:
            kt = jnp.concatenate(rhs_tiles, axis=0).astype(BF16).T
            dst[:LANES, :CHUNK] = kt[:, :CHUNK]
            dst[LANES:, CHUNK:] = kt[:, CHUNK:]

    def tiles_of_pair(p):
        if p < N_GLA_PAIRS:
            return (p,)
        return tuple(N_GLA_PAIRS + 2 * (p - N_GLA_PAIRS) + pos for pos in range(2))

    def pair_lanes(p):
        if p < N_GLA_PAIRS:
            return slice(p * LANES, (p + 1) * LANES)
        return slice(GLA_K + (p - N_GLA_PAIRS) * PAIR_V, GLA_K + (p - N_GLA_PAIRS + 1) * PAIR_V)

    def anchor_keys(ci, p):
        return kag_s.at[p, ci] if p < N_GLA_PAIRS else kah_s.at[p - N_GLA_PAIRS, ci]

    def level_keys(lv, p, ci):
        return klg_s.at[lv, p, ci] if p < N_GLA_PAIRS else klh_s.at[lv, p - N_GLA_PAIRS, ci]

    def prep_pair(ci, p):
        store_keys_t(anchor_keys(ci, p), p, [tile_base(ci, lt) for lt in tiles_of_pair(p)])

    def prep_pair_levels(ci, p):
        tiles = tiles_of_pair(p)
        bases = [tile_reload(ci, lt) for lt in tiles]
        for lv in range(N_LEVELS):
            rhs_tiles = []
            for lt, base in zip(tiles, bases):
                lhs, rhs = tile_level(base, lv)
                lql_s[lv, ci * CHUNK:(ci + 1) * CHUNK, lt * LANES:(lt + 1) * LANES] = lhs.astype(BF16)
                rhs_tiles.append(rhs)
            store_keys_t(level_keys(lv, p, ci), p, rhs_tiles)

    def pair_attention(ci, p, single_anchor):
        lanes = pair_lanes(p)
        if single_anchor:
            prod = jnp.dot(qst_s[ci * CHUNK:(ci + 1) * CHUNK, lanes], anchor_keys(ci, p)[...],
                           preferred_element_type=F32)
            return jnp.where(masks_ref[MASK_CAUSAL] > 0.0, prod, 0.0).astype(BF16)
        att = None
        for lv in range(N_LEVELS):
            prod = jnp.dot(lql_s[lv, ci * CHUNK:(ci + 1) * CHUNK, lanes], level_keys(lv, p, ci)[...],
                           preferred_element_type=F32)
            if lv > 0:
                prod = prod * masks_ref[lv]
            att = prod if att is None else att + prod
        return att.astype(BF16)

    def snapshot_states(ci):
        bi = ci // chunks_per_seq
        for p in range(N_GLA_PAIRS):
            sbg_s[ci, p] = stg_ref[bi, p].astype(BF16)
        for hh in range(HGRN_HEADS):
            pos = hh % 2
            sbh_s[ci, hh // 2, pos * LANES:(pos + 1) * LANES, pos * LANES:(pos + 1) * LANES] = (
                sth_ref[bi, hh].astype(BF16))

    def update_states(ci):
        bi = ci // chunks_per_seq
        for p in range(N_GLA_PAIRS):
            dcol = dcol_s[ci, p]
            stg_ref[bi, p] = (stg_ref[bi, p] * jnp.concatenate([dcol, dcol], axis=1)
                              + lax.dot_general(ksg_s[p, ci], vbd_s[p, ci], TN, preferred_element_type=F32))
        for hh in range(HGRN_HEADS):
            pg, pos = N_GLA_PAIRS + hh // 2, hh % 2
            v_head = vbd_s[pg, ci, pos * CHUNK:(pos + 1) * CHUNK, pos * LANES:(pos + 1) * LANES]
            sth_ref[bi, hh] = sth_ref[bi, hh] * dcol_s[ci, N_GLA_PAIRS + hh] + lax.dot_general(
                ksh_s[hh, ci], v_head, TN, preferred_element_type=F32)

    def pair_output(ci, p, att):
        r = slice(ci * CHUNK, (ci + 1) * CHUNK)
        state = sbg_s[ci, p] if p < N_GLA_PAIRS else sbh_s[ci, p - N_GLA_PAIRS]
        o_s[r, p * PAIR_V:(p + 1) * PAIR_V] = (
            jnp.dot(att, vbd_s[p, ci], preferred_element_type=F32)
            + jnp.dot(qst_s[r, pair_lanes(p)], state, preferred_element_type=F32))

    gate_rows = mod_rows(2)

    def gate_half(h):
        rr = slice(h * half, (h + 1) * half)
        for ht in range(GLA_HEADS + HGRN_HEADS):
            lanes = slice(ht * LANES, (ht + 1) * LANES)
            gb_s[rr, lanes] = (_rms(o_s[rr, lanes]) * gon_ref[:, lanes] * zs_s[rr, lanes]).astype(BF16)

    def project_half(h, n):
        rr = slice(h * half, (h + 1) * half)
        cols = slice(n * PAIR_V, (n + 1) * PAIR_V)
        po_s[rr, cols] = jnp.dot(gb_s[rr, :], wout_ref[:, cols], preferred_element_type=F32)

    def post_half(h):
        rr = slice(h * half, (h + 1) * half)
        gate_h = gate_rows if bb == 1 else gate_rows[rr]
        res = gate_h * (_rms(po_s[rr, :]) * gpost_ref[...])
        if bb == 1:
            y_ref[0, rr, :] = x_ref[0, rr, :] + res
        else:
            seqs = slice(h * (bb // 2), (h + 1) * (bb // 2))
            y_ref[seqs] = x_ref[seqs] + res.reshape(bb // 2, tt, D_MODEL)

    def finish_stages(h):
        return ([functools.partial(project_half, h, n) for n in range(N_COL_BLOCKS)]
                + [functools.partial(post_half, h)])

    fillers = [functools.partial(project_gates, n) for n in range(N_COL_BLOCKS)]
    atts = {}
    for step in range(n_chunks + 2):
        if step < n_chunks:
            for p in range(N_PAIRS):
                prep_pair(step, p)
        if step == 0:
            project_values()
        if 1 <= step <= n_chunks:
            snapshot_states(step - 1)
            if step < n_chunks:
                update_states(step - 1)
            atts[step - 1] = [pair_attention(step - 1, p, True) for p in range(N_PAIRS)]
        if fillers:
            fillers.pop(0)()
        if step >= 2:
            for p in range(N_PAIRS):
                pair_output(step - 2, p, atts[step - 2][p])
            if (step - 1) * CHUNK == half:
                while fillers:
                    fillers.pop(0)()
                gate_half(0)
                fillers = finish_stages(0)
    last_stages = finish_stages(1)
    for stage in ([functools.partial(gate_half, 1)] + fillers + last_stages[:-1]
                  + [functools.partial(update_states, n_chunks - 1)] + last_stages[-1:]):
        stage()

    total = None
    for ci in range(n_chunks):
        t = jnp.sum(g_s[ci * CHUNK:(ci + 1) * CHUNK, :], axis=0, keepdims=True)
        total = t if total is None else jnp.minimum(total, t)
    mild = jnp.min(total) >= -SINGLE_ANCHOR_MAX_LOG2

    @pl.when(jnp.logical_not(mild))
    def _():
        for ci in range(n_chunks):
            for p in range(N_PAIRS):
                prep_pair_levels(ci, p)
            for p in range(N_PAIRS):
                pair_output(ci, p, pair_attention(ci, p, False))
        for h in range(2):
            for stage in [functools.partial(gate_half, h)] + finish_stages(h):
                stage()


def _mixer_call(x, mods, stg0, sth0, w_in_parts, w_alpha, b_alpha, lb_logits, g_on, w_out, g_post, masks,
                *, bb, tt):
    nb, seq, _ = x.shape
    rows = bb * tt
    assert nb % bb == 0 and seq % tt == 0 and tt % CHUNK == 0
    n_chunks = rows // CHUNK
    per_seq4 = lambda b, j: (b, 0, 0, 0)

    def resident(shape):
        return pl.BlockSpec(shape, lambda b, j: (0,) * len(shape), pipeline_mode=pl.Buffered(1))

    kernel = functools.partial(_mixer_kernel, bb=bb, tt=tt)
    stg_shape = (nb, N_GLA_PAIRS, LANES, PAIR_V)
    sth_shape = (nb, HGRN_HEADS, HGRN_D, HGRN_D)
    return pl.pallas_call(
        kernel,
        grid=(nb // bb, seq // tt),
        in_specs=[
            pl.BlockSpec((bb, tt, D_MODEL), lambda b, j: (b, j, 0)),
            pl.BlockSpec((bb, 3, D_MODEL), lambda b, j: (b, 0, 0)),
            pl.BlockSpec((bb,) + stg_shape[1:], per_seq4),
            pl.BlockSpec((bb,) + sth_shape[1:], per_seq4),
            *[resident(part.shape) for part in w_in_parts],
            resident((RANK_PAD, GLA_K)),
            resident((1, GLA_K)),
            resident(lb_logits.shape),
            resident((1, D_MODEL)),
            resident((D_MODEL, D_MODEL)),
            resident((1, D_MODEL)),
            resident(masks.shape),
        ],
        out_specs=[
            pl.BlockSpec((bb, tt, D_MODEL), lambda b, j: (b, j, 0)),
            pl.BlockSpec((bb,) + stg_shape[1:], per_seq4),
            pl.BlockSpec((bb,) + sth_shape[1:], per_seq4),
        ],
        out_shape=[
            jax.ShapeDtypeStruct(x.shape, F32),
            jax.ShapeDtypeStruct(stg_shape, F32),
            jax.ShapeDtypeStruct(sth_shape, F32),
        ],
        scratch_shapes=[
            pltpu.VMEM((rows, D_MODEL), BF16),
            pltpu.VMEM((rows, K_LANES), F32),
            pltpu.VMEM((rows, K_LANES), F32),
            pltpu.VMEM((rows, K_LANES), F32),
            pltpu.VMEM((N_PAIRS, n_chunks, 2 * CHUNK, PAIR_V), BF16),
            pltpu.VMEM((rows, D_MODEL), F32),
            pltpu.VMEM((rows, D_MODEL), F32),
            pltpu.VMEM((n_chunks, N_K_TILES, CHUNK, LANES), F32),
            pltpu.VMEM((rows, K_LANES), BF16),
            pltpu.VMEM((N_GLA_PAIRS, n_chunks, LANES, 2 * CHUNK), BF16),
            pltpu.VMEM((N_HGRN_PAIRS, n_chunks, PAIR_V, 2 * CHUNK), BF16),
            pltpu.VMEM((N_GLA_PAIRS, n_chunks, 2 * CHUNK, LANES), BF16),
            pltpu.VMEM((HGRN_HEADS, n_chunks, CHUNK, LANES), BF16),
            pltpu.VMEM((n_chunks, N_K_TILES, LANES, LANES), F32),
            pltpu.VMEM((n_chunks, N_GLA_PAIRS, LANES, PAIR_V), BF16),
            pltpu.VMEM((n_chunks, N_HGRN_PAIRS, PAIR_V, PAIR_V), BF16),
            pltpu.VMEM((N_LEVELS, rows, K_LANES), BF16),
            pltpu.VMEM((N_LEVELS, N_GLA_PAIRS, n_chunks, LANES, 2 * CHUNK), BF16),
            pltpu.VMEM((N_LEVELS, N_HGRN_PAIRS, n_chunks, PAIR_V, 2 * CHUNK), BF16),
            pltpu.VMEM((rows, D_MODEL), BF16),
            pltpu.VMEM((rows, D_MODEL), F32),
        ],
        compiler_params=pltpu.CompilerParams(
            dimension_semantics=("arbitrary", "arbitrary"),
            vmem_limit_bytes=VMEM_LIMIT_BYTES,
        ),
        name="mixer_bb%d_tt%d" % (bb, tt),
    )(x, mods, stg0, sth0, *w_in_parts, w_alpha, b_alpha, lb_logits, g_on, w_out, g_post, masks)


def _pair_gla_states(s_gla):
    nb = s_gla.shape[0]
    a = s_gla.reshape(nb, N_GLA_PAIRS, 2, GLA_DK, GLA_DV)
    z = jnp.zeros_like(a[:, :, 0])
    return jnp.concatenate([jnp.concatenate([a[:, :, 0], z], axis=-1),
                            jnp.concatenate([z, a[:, :, 1]], axis=-1)], axis=2)


def _unpair_gla_states(stg):
    nb = stg.shape[0]
    heads = [stg[:, :, h * GLA_DK:(h + 1) * GLA_DK, h * GLA_DV:(h + 1) * GLA_DV] for h in range(2)]
    return jnp.stack(heads, axis=2).reshape(nb, GLA_HEADS, GLA_DK, GLA_DV)


def kernel(x_prompt, x_sample, c_prompt, c_sample, state_gla, state_hgrn, w_ada, b_ada, g_pre, w_in,
           w_alpha, b_alpha, g_onorm_gla, hgrn_lb_logits, g_onorm_hgrn, w_out, g_post):
    assert w_ada.shape[0] == 1, "single-layer problem"
    bp = x_prompt.shape[0]

    w_in_parts = _split_weights_call(w_in[0])
    w_alpha_p = jnp.concatenate(
        [w_alpha[0], jnp.zeros((RANK_PAD - GATE_RANK, GLA_K), w_alpha.dtype)], axis=0).astype(BF16)
    g_on = jnp.concatenate([jnp.tile(g_onorm_gla[0], GLA_HEADS),
                            jnp.tile(g_onorm_hgrn[0], HGRN_HEADS)])[None, :]
    masks = jnp.asarray(_level_masks())

    c_all = jnp.concatenate([c_prompt, c_sample], axis=0)
    mods = _ada_call(c_all, w_ada[0], b_ada, g_pre).reshape(c_all.shape[0], 3, D_MODEL)

    shared = (tuple(w_in_parts), w_alpha_p, b_alpha, hgrn_lb_logits, g_on, w_out[0].astype(BF16), g_post, masks)
    stg0_p = jnp.zeros((bp, N_GLA_PAIRS, LANES, PAIR_V), F32)
    sth0_p = jnp.zeros((bp, HGRN_HEADS, HGRN_D, HGRN_D), F32)
    yp, stg_p, hgrn_p = _mixer_call(x_prompt, mods[:bp], stg0_p, sth0_p, *shared, bb=1, tt=ROWS_PER_STEP)
    ys, stg_s, hgrn_s = _mixer_call(x_sample, mods[bp:], _pair_gla_states(state_gla[0]), state_hgrn[0],
                                    *shared, bb=ROWS_PER_STEP // x_sample.shape[1], tt=x_sample.shape[1])
    return (yp, ys, _unpair_gla_states(stg_p)[None], hgrn_p[None],
            _unpair_gla_states(stg_s)[None].astype(state_gla.dtype), hgrn_s[None].astype(state_hgrn.dtype))
```

```python
import functools
import math

import numpy as np
import jax
import jax.numpy as jnp
from jax import lax
from jax.experimental import pallas as pl
from jax.experimental.pallas import tpu as pltpu

D_MODEL = 1024
CHUNK = 64
GLA_HEADS = 4
GLA_V = D_MODEL // 2
GLA_K = GLA_V // 2
GLA_DK = GLA_K // GLA_HEADS
GLA_DV = GLA_V // GLA_HEADS
GATE_RANK = 16
GATE_TAU = 16.0
HGRN_HEADS = 4
HGRN_W = D_MODEL - GLA_V
HGRN_D = HGRN_W // HGRN_HEADS
IN_SPLITS = (GLA_K, GLA_K, GLA_V, GLA_V, GATE_RANK, HGRN_W, HGRN_W, HGRN_W, HGRN_W)
EPS = 1e-6
LOG2E = math.log2(math.e)

LANES = 128
SUBLANES = 8
K_LANES = GLA_K + HGRN_W
N_K_TILES = K_LANES // LANES
N_GLA_PAIRS = GLA_HEADS // 2
N_HGRN_PAIRS = HGRN_HEADS // 2
N_PAIRS = N_GLA_PAIRS + N_HGRN_PAIRS
PAIR_V = 2 * LANES
N_COL_BLOCKS = D_MODEL // PAIR_V
RANK_PAD = LANES
HALF_SIZES = (32, 16, 8, 4, 2, 1)
N_LEVELS = len(HALF_SIZES) + 1
MASK_CAUSAL = N_LEVELS
SINGLE_ANCHOR_MAX_LOG2 = 90.0
ROWS_PER_STEP = 256
VMEM_LIMIT_BYTES = 56 * 1024 * 1024

F32 = jnp.float32
BF16 = jnp.bfloat16
TN = (((0,), (0,)), ((), ()))


def _level_masks():
    t = np.arange(CHUNK)[:, None]
    s = np.arange(CHUNK)[None, :]
    masks = [(t // (2 * m)) == (s // (2 * m)) for m in HALF_SIZES]
    masks.append(t == s)
    masks.append(t >= s)
    masks = np.stack(masks).astype(np.float32)
    return np.concatenate([masks, masks], axis=2)


def _rms(x):
    return x * lax.rsqrt(jnp.mean(x * x, axis=-1, keepdims=True) + EPS)


def _sigmoid(x):
    return 1.0 / (1.0 + jnp.exp(-x))


def _log2_sigmoid(x):
    return jnp.minimum(x, 0.0) * LOG2E - jnp.log2(1.0 + jnp.exp(-jnp.abs(x)))


def _ada_kernel(c_ref, w_ref, b_ref, gpre_ref, out_ref):
    n = pl.program_id(0)
    val = jnp.dot(c_ref[...].astype(BF16), w_ref[...].astype(BF16),
                  preferred_element_type=F32) + b_ref[...]
    out_ref[...] = jnp.where(n == 1, gpre_ref[...] * (1.0 + val), val)


def _ada_call(c_all, w_ada, b_ada, g_pre):
    nb = c_all.shape[0]
    return pl.pallas_call(
        _ada_kernel,
        grid=(3,),
        in_specs=[
            pl.BlockSpec((nb, D_MODEL), lambda n: (0, 0)),
            pl.BlockSpec((D_MODEL, D_MODEL), lambda n: (0, n)),
            pl.BlockSpec((1, D_MODEL), lambda n: (0, n)),
            pl.BlockSpec((1, D_MODEL), lambda n: (0, 0)),
        ],
        out_specs=pl.BlockSpec((nb, D_MODEL), lambda n: (0, n)),
        out_shape=jax.ShapeDtypeStruct((nb, 3 * D_MODEL), F32),
        name="ada_mod",
    )(c_all, w_ada, b_ada, g_pre)


def _ref_rows(b_ref, block, off):
    pieces = []
    if block >= SUBLANES:
        for j in range(CHUNK // block):
            row = b_ref[j * block + off:j * block + off + 1, :]
            pieces.append(jnp.broadcast_to(row, (block, LANES)))
    else:
        assert block == SUBLANES // 2
        sub = lax.broadcasted_iota(jnp.int32, (SUBLANES, LANES), 0)
        for i in range(CHUNK // SUBLANES):
            lo = b_ref[SUBLANES * i + off:SUBLANES * i + off + 1, :]
            hi = b_ref[SUBLANES * i + block + off:SUBLANES * i + block + off + 1, :]
            pieces.append(jnp.where(sub < block,
                                    jnp.broadcast_to(lo, (SUBLANES, LANES)),
                                    jnp.broadcast_to(hi, (SUBLANES, LANES))))
    return jnp.concatenate(pieces, axis=0)


def _mixer_kernel(x_ref, mods_ref, stg0_ref, sth0_ref,
                  wqa_ref, wka_ref, wva_ref, wza_ref, wal_ref, wqh_ref, wfh_ref, wih_ref, wzh_ref,
                  walpha_ref, balpha_ref, lbl_ref,
                  gon_ref, wout_ref, gpost_ref, masks_ref,
                  y_ref, stg_ref, sth_ref,
                  hb_s, q_s, k_s, g_s, vbd_s, zs_s, o_s, b_s, qst_s, kag_s, kah_s, ksg_s, ksh_s, dcol_s, sbg_s, sbh_s,
                  lql_s, klg_s, klh_s, gb_s, po_s,
                  *, bb, tt):
    rows = bb * tt
    n_chunks = rows // CHUNK
    chunks_per_seq = tt // CHUNK

    @pl.when((pl.program_id(0) == 0) & (pl.program_id(1) == 0))
    def _():
        for ref in (vbd_s, kah_s, klh_s, sbh_s):
            ref[...] = jnp.zeros(ref.shape, BF16)

    @pl.when(pl.program_id(1) == 0)
    def _():
        stg_ref[...] = stg0_ref[...]
        sth_ref[...] = sth0_ref[...]

    def mod_rows(i):
        if bb == 1:
            return mods_ref[0, i:i + 1, :]
        return jnp.concatenate(
            [jnp.broadcast_to(mods_ref[bi, i:i + 1, :], (tt, D_MODEL)) for bi in range(bb)], axis=0)

    half = rows // 2
    x = x_ref[...].reshape(rows, D_MODEL)
    hb_s[...] = (_rms(x) * mod_rows(1) + mod_rows(0)).astype(BF16)

    def proj(w_ref, n=0):
        width = min(PAIR_V, w_ref.shape[1])
        return jnp.dot(hb_s[...], w_ref[:, n * width:(n + 1) * width], preferred_element_type=F32)

    def head_group_block(gla_ref, hgrn_ref, n):
        n_gla = GLA_V // PAIR_V
        return proj(gla_ref, n) if n < n_gla else proj(hgrn_ref, n - n_gla)

    a_lr = proj(wal_ref)
    ga = jnp.dot(a_lr.astype(BF16), walpha_ref[...], preferred_element_type=F32) + balpha_ref[...]
    g_s[:, :GLA_K] = _log2_sigmoid(ga) * (1.0 / GATE_TAU)

    k_s[:, :GLA_K] = proj(wka_ref)
    lbl = lbl_ref[...]
    lmax = jnp.max(lbl, axis=0, keepdims=True)
    lexp = jnp.exp(lbl - lmax)
    lb = lexp[0:1, :] / jnp.sum(lexp, axis=0, keepdims=True)
    for n in range(HGRN_W // PAIR_V):
        cols = slice(GLA_K + n * PAIR_V, GLA_K + (n + 1) * PAIR_V)
        lb_n = lb[:, n * PAIR_V:(n + 1) * PAIR_V]
        f = lb_n + (1.0 - lb_n) * _sigmoid(proj(wfh_ref, n))
        k_s[:, cols] = 1.0 - f
        g_s[:, cols] = jnp.log2(f)

    q_s[:, :GLA_K] = proj(wqa_ref) * (GLA_DK ** -0.5)
    for n in range(HGRN_W // PAIR_V):
        cols = slice(GLA_K + n * PAIR_V, GLA_K + (n + 1) * PAIR_V)
        qh = proj(wqh_ref, n)
        q_s[:, cols] = qh * _sigmoid(qh)

    def project_values():
        for p in range(N_PAIRS):
            pv = head_group_block(wva_ref, wih_ref, p).astype(BF16)
            for ci in range(n_chunks):
                r = slice(ci * CHUNK, (ci + 1) * CHUNK)
                vbd_s[p, ci, :CHUNK, :LANES] = pv[r, :LANES]
                vbd_s[p, ci, CHUNK:, LANES:] = pv[r, LANES:]

    def project_gates(n):
        cols = slice(n * PAIR_V, (n + 1) * PAIR_V)
        pz = head_group_block(wza_ref, wzh_ref, n)
        zs_s[:, cols] = pz * _sigmoid(pz)

    row_i = lax.broadcasted_iota(jnp.int32, (CHUNK, LANES), 0)
    lane_i = lax.broadcasted_iota(jnp.int32, (CHUNK, LANES), 1)
    head0 = lane_i < GLA_DK

    def tile_base(ci, lt):
        r = slice(ci * CHUNK, (ci + 1) * CHUNK)
        lanes = slice(lt * LANES, (lt + 1) * LANES)
        b_t = b_s.at[ci, lt]
        g = g_s[r, lanes]
        q = q_s[r, lanes]
        k = k_s[r, lanes]
        b = g
        for sh in (1, 2, 4):
            b = b + jnp.where(row_i >= sh, pltpu.roll(b, sh, 0), 0.0)
        for sh in (8, 16, 32):
            b = b + jnp.concatenate([jnp.zeros((sh, LANES), F32), b[:CHUNK - sh]], axis=0)
        b_t[...] = b
        b_last = b_t[CHUNK - 1:CHUNK, :]
        dcol_s[ci, lt] = jnp.exp2(jnp.broadcast_to(b_last, (LANES, LANES)).T)
        qst_s[r, lanes] = (q * jnp.exp2(b)).astype(BF16)
        k_state = k * jnp.exp2(b_last - b)
        if lt < N_GLA_PAIRS:
            ksg_s[lt, ci, :CHUNK, :] = jnp.where(head0, k_state, 0.0).astype(BF16)
            ksg_s[lt, ci, CHUNK:, :] = jnp.where(head0, 0.0, k_state).astype(BF16)
        else:
            ksh_s[lt - N_GLA_PAIRS, ci] = k_state.astype(BF16)
        return k * jnp.exp2(-b)

    def tile_reload(ci, lt):
        r = slice(ci * CHUNK, (ci + 1) * CHUNK)
        lanes = slice(lt * LANES, (lt + 1) * LANES)
        b_t = b_s.at[ci, lt]
        return q_s[r, lanes], k_s[r, lanes], g_s[r, lanes], b_t[...], b_t

    def tile_level(base, lv):
        q, k, g, b, b_t = base
        if lv == N_LEVELS - 1:
            return q, k
        m = HALF_SIZES[lv]
        upper = (row_i & (2 * m - 1)) >= m
        if m == 1:
            lhs, rhs = q * jnp.exp2(g), k
        else:
            e = jnp.exp2(-jnp.abs(b - _ref_rows(b_t, 2 * m, m - 1)))
            lhs, rhs = q * e, k * e
        return jnp.where(upper, lhs, 0.0), jnp.where(upper, 0.0, rhs)

    def store_keys_t(dst, p, rhs_tiles):
        if p < N_GLA_PAIRS:
            rhs = rhs_tiles[0]
            stacked = jnp.concatenate([jnp.where(head0, rhs, 0.0), jnp.where(head0, 0.0, rhs)], axis=0)
            dst[...] = stacked.astype(BF16).T
        else:
            kt = jnp.concatenate(rhs_tiles, axis=0).astype(BF16).T
            dst[:LANES, :CHUNK] = kt[:, :CHUNK]
            dst[LANES:, CHUNK:] = kt[:, CHUNK:]

    def tiles_of_pair(p):
        if p < N_GLA_PAIRS:
            return (p,)
        return tuple(N_GLA_PAIRS + 2 * (p - N_GLA_PAIRS) + pos for pos in range(2))

    def pair_lanes(p):
        if p < N_GLA_PAIRS:
            return slice(p * LANES, (p + 1) * LANES)
        return slice(GLA_K + (p - N_GLA_PAIRS) * PAIR_V, GLA_K + (p - N_GLA_PAIRS + 1) * PAIR_V)

    def anchor_keys(ci, p):
        return kag_s.at[p, ci] if p < N_GLA_PAIRS else kah_s.at[p - N_GLA_PAIRS, ci]

    def level_keys(lv, p, ci):
        return klg_s.at[lv, p, ci] if p < N_GLA_PAIRS else klh_s.at[lv, p - N_GLA_PAIRS, ci]

    def prep_pair(ci, p):
        store_keys_t(anchor_keys(ci, p), p, [tile_base(ci, lt) for lt in tiles_of_pair(p)])

    def prep_pair_levels(ci, p):
        tiles = tiles_of_pair(p)
        bases = [tile_reload(ci, lt) for lt in tiles]
        for lv in range(N_LEVELS):
            rhs_tiles = []
            for lt, base in zip(tiles, bases):
                lhs, rhs = tile_level(base, lv)
                lql_s[lv, ci * CHUNK:(ci + 1) * CHUNK, lt * LANES:(lt + 1) * LANES] = lhs.astype(BF16)
                rhs_tiles.append(rhs)
            store_keys_t(level_keys(lv, p, ci), p, rhs_tiles)

    def pair_attention(ci, p, single_anchor):
        lanes = pair_lanes(p)
        if single_anchor:
            prod = jnp.dot(qst_s[ci * CHUNK:(ci + 1) * CHUNK, lanes], anchor_keys(ci, p)[...],
                           preferred_element_type=F32)
            return jnp.where(masks_ref[MASK_CAUSAL] > 0.0, prod, 0.0).astype(BF16)
        att = None
        for lv in range(N_LEVELS):
            prod = jnp.dot(lql_s[lv, ci * CHUNK:(ci + 1) * CHUNK, lanes], level_keys(lv, p, ci)[...],
                           preferred_element_type=F32)
            if lv > 0:
                prod = prod * masks_ref[lv]
            att = prod if att is None else att + prod
        return att.astype(BF16)

    def snapshot_states(ci):
        bi = ci // chunks_per_seq
        for p in range(N_GLA_PAIRS):
            sbg_s[ci, p] = stg_ref[bi, p].astype(BF16)
        for hh in range(HGRN_HEADS):
            pos = hh % 2
            sbh_s[ci, hh // 2, pos * LANES:(pos + 1) * LANES, pos * LANES:(pos + 1) * LANES] = (
                sth_ref[bi, hh].astype(BF16))

    def update_states(ci):
        bi = ci // chunks_per_seq
        for p in range(N_GLA_PAIRS):
            dcol = dcol_s[ci, p]
            stg_ref[bi, p] = (stg_ref[bi, p] * jnp.concatenate([dcol, dcol], axis=1)
                              + lax.dot_general(ksg_s[p, ci], vbd_s[p, ci], TN, preferred_element_type=F32))
        for hh in range(HGRN_HEADS):
            pg, pos = N_GLA_PAIRS + hh // 2, hh % 2
            v_head = vbd_s[pg, ci, pos * CHUNK:(pos + 1) * CHUNK, pos * LANES:(pos + 1) * LANES]
            sth_ref[bi, hh] = sth_ref[bi, hh] * dcol_s[ci, N_GLA_PAIRS + hh] + lax.dot_general(
                ksh_s[hh, ci], v_head, TN, preferred_element_type=F32)

    def pair_output(ci, p, att):
        r = slice(ci * CHUNK, (ci + 1) * CHUNK)
        state = sbg_s[ci, p] if p < N_GLA_PAIRS else sbh_s[ci, p - N_GLA_PAIRS]
        lhs = jnp.concatenate([att, qst_s[r, pair_lanes(p)]], axis=1)
        rhs = jnp.concatenate([vbd_s[p, ci], state], axis=0)
        o_s[r, p * PAIR_V:(p + 1) * PAIR_V] = jnp.dot(lhs, rhs, preferred_element_type=F32)

    gate_rows = mod_rows(2)

    def gate_half(h):
        rr = slice(h * half, (h + 1) * half)
        for ht in range(GLA_HEADS + HGRN_HEADS):
            lanes = slice(ht * LANES, (ht + 1) * LANES)
            gb_s[rr, lanes] = (_rms(o_s[rr, lanes]) * gon_ref[:, lanes] * zs_s[rr, lanes]).astype(BF16)

    def project_half(h, n):
        rr = slice(h * half, (h + 1) * half)
        cols = slice(n * PAIR_V, (n + 1) * PAIR_V)
        po_s[rr, cols] = jnp.dot(gb_s[rr, :], wout_ref[:, cols], preferred_element_type=F32)

    def post_half(h):
        rr = slice(h * half, (h + 1) * half)
        gate_h = gate_rows if bb == 1 else gate_rows[rr]
        res = gate_h * (_rms(po_s[rr, :]) * gpost_ref[...])
        if bb == 1:
            y_ref[0, rr, :] = x_ref[0, rr, :] + res
        else:
            seqs = slice(h * (bb // 2), (h + 1) * (bb // 2))
            y_ref[seqs] = x_ref[seqs] + res.reshape(bb // 2, tt, D_MODEL)

    def finish_stages(h):
        return ([functools.partial(project_half, h, n) for n in range(N_COL_BLOCKS)]
                + [functools.partial(post_half, h)])

    fillers = [functools.partial(project_gates, n) for n in range(N_COL_BLOCKS)]
    atts = {}
    for step in range(n_chunks + 2):
        if step < n_chunks:
            for p in range(N_PAIRS):
                prep_pair(step, p)
        if step == 0:
            project_values()
        if 1 <= step <= n_chunks:
            snapshot_states(step - 1)
            if step < n_chunks:
                update_states(step - 1)
            atts[step - 1] = [pair_attention(step - 1, p, True) for p in range(N_PAIRS)]
        if fillers:
            fillers.pop(0)()
        if step >= 2:
            for p in range(N_PAIRS):
                pair_output(step - 2, p, atts[step - 2][p])
            if (step - 1) * CHUNK == half:
                while fillers:
                    fillers.pop(0)()
                gate_half(0)
                fillers = finish_stages(0)
    last_stages = finish_stages(1)
    for stage in ([functools.partial(gate_half, 1)] + fillers + last_stages[:-1]
                  + [functools.partial(update_states, n_chunks - 1)] + last_stages[-1:]):
        stage()

    total = None
    for ci in range(n_chunks):
        t = jnp.sum(g_s[ci * CHUNK:(ci + 1) * CHUNK, :], axis=0, keepdims=True)
        total = t if total is None else jnp.minimum(total, t)
    mild = jnp.min(total) >= -SINGLE_ANCHOR_MAX_LOG2

    @pl.when(jnp.logical_not(mild))
    def _():
        for ci in range(n_chunks):
            for p in range(N_PAIRS):
                prep_pair_levels(ci, p)
            for p in range(N_PAIRS):
                pair_output(ci, p, pair_attention(ci, p, False))
        for h in range(2):
            for stage in [functools.partial(gate_half, h)] + finish_stages(h):
                stage()


def _mixer_call(x, mods, stg0, sth0, w_in_parts, w_alpha, b_alpha, lb_logits, g_on, w_out, g_post, masks,
                *, bb, tt):
    nb, seq, _ = x.shape
    rows = bb * tt
    assert nb % bb == 0 and seq % tt == 0 and tt % CHUNK == 0
    n_chunks = rows // CHUNK
    per_seq4 = lambda b, j: (b, 0, 0, 0)

    def resident(shape):
        return pl.BlockSpec(shape, lambda b, j: (0,) * len(shape), pipeline_mode=pl.Buffered(1))

    kernel = functools.partial(_mixer_kernel, bb=bb, tt=tt)
    stg_shape = (nb, N_GLA_PAIRS, LANES, PAIR_V)
    sth_shape = (nb, HGRN_HEADS, HGRN_D, HGRN_D)
    return pl.pallas_call(
        kernel,
        grid=(nb // bb, seq // tt),
        in_specs=[
            pl.BlockSpec((bb, tt, D_MODEL), lambda b, j: (b, j, 0)),
            pl.BlockSpec((bb, 3, D_MODEL), lambda b, j: (b, 0, 0)),
            pl.BlockSpec((bb,) + stg_shape[1:], per_seq4),
            pl.BlockSpec((bb,) + sth_shape[1:], per_seq4),
            *[resident(part.shape) for part in w_in_parts],
            resident((RANK_PAD, GLA_K)),
            resident((1, GLA_K)),
            resident(lb_logits.shape),
            resident((1, D_MODEL)),
            resident((D_MODEL, D_MODEL)),
            resident((1, D_MODEL)),
            resident(masks.shape),
        ],
        out_specs=[
            pl.BlockSpec((bb, tt, D_MODEL), lambda b, j: (b, j, 0)),
            pl.BlockSpec((bb,) + stg_shape[1:], per_seq4),
            pl.BlockSpec((bb,) + sth_shape[1:], per_seq4),
        ],
        out_shape=[
            jax.ShapeDtypeStruct(x.shape, F32),
            jax.ShapeDtypeStruct(stg_shape, F32),
            jax.ShapeDtypeStruct(sth_shape, F32),
        ],
        scratch_shapes=[
            pltpu.VMEM((rows, D_MODEL), BF16),
            pltpu.VMEM((rows, K_LANES), F32),
            pltpu.VMEM((rows, K_LANES), F32),
            pltpu.VMEM((rows, K_LANES), F32),
            pltpu.VMEM((N_PAIRS, n_chunks, 2 * CHUNK, PAIR_V), BF16),
            pltpu.VMEM((rows, D_MODEL), F32),
            pltpu.VMEM((rows, D_MODEL), F32),
            pltpu.VMEM((n_chunks, N_K_TILES, CHUNK, LANES), F32),
            pltpu.VMEM((rows, K_LANES), BF16),
            pltpu.VMEM((N_GLA_PAIRS, n_chunks, LANES, 2 * CHUNK), BF16),
            pltpu.VMEM((N_HGRN_PAIRS, n_chunks, PAIR_V, 2 * CHUNK), BF16),
            pltpu.VMEM((N_GLA_PAIRS, n_chunks, 2 * CHUNK, LANES), BF16),
            pltpu.VMEM((HGRN_HEADS, n_chunks, CHUNK, LANES), BF16),
            pltpu.VMEM((n_chunks, N_K_TILES, LANES, LANES), F32),
            pltpu.VMEM((n_chunks, N_GLA_PAIRS, LANES, PAIR_V), BF16),
            pltpu.VMEM((n_chunks, N_HGRN_PAIRS, PAIR_V, PAIR_V), BF16),
            pltpu.VMEM((N_LEVELS, rows, K_LANES), BF16),
            pltpu.VMEM((N_LEVELS, N_GLA_PAIRS, n_chunks, LANES, 2 * CHUNK), BF16),
            pltpu.VMEM((N_LEVELS, N_HGRN_PAIRS, n_chunks, PAIR_V, 2 * CHUNK), BF16),
            pltpu.VMEM((rows, D_MODEL), BF16),
            pltpu.VMEM((rows, D_MODEL), F32),
        ],
        compiler_params=pltpu.CompilerParams(
            dimension_semantics=("arbitrary", "arbitrary"),
            vmem_limit_bytes=VMEM_LIMIT_BYTES,
        ),
        name="mixer_bb%d_tt%d" % (bb, tt),
    )(x, mods, stg0, sth0, *w_in_parts, w_alpha, b_alpha, lb_logits, g_on, w_out, g_post, masks)


def _pair_gla_states(s_gla):
    nb = s_gla.shape[0]
    a = s_gla.reshape(nb, N_GLA_PAIRS, 2, GLA_DK, GLA_DV)
    z = jnp.zeros_like(a[:, :, 0])
    return jnp.concatenate([jnp.concatenate([a[:, :, 0], z], axis=-1),
                            jnp.concatenate([z, a[:, :, 1]], axis=-1)], axis=2)


def _unpair_gla_states(stg):
    nb = stg.shape[0]
    heads = [stg[:, :, h * GLA_DK:(h + 1) * GLA_DK, h * GLA_DV:(h + 1) * GLA_DV] for h in range(2)]
    return jnp.stack(heads, axis=2).reshape(nb, GLA_HEADS, GLA_DK, GLA_DV)


def kernel(x_prompt, x_sample, c_prompt, c_sample, state_gla, state_hgrn, w_ada, b_ada, g_pre, w_in,
           w_alpha, b_alpha, g_onorm_gla, hgrn_lb_logits, g_onorm_hgrn, w_out, g_post):
    assert w_ada.shape[0] == 1, "single-layer problem"
    bp = x_prompt.shape[0]

    w_in_parts, offset = [], 0
    for width in IN_SPLITS:
        part = w_in[0][:, offset:offset + width].astype(BF16)
        if width < LANES:
            part = jnp.pad(part, ((0, 0), (0, RANK_PAD - width)))
        w_in_parts.append(part)
        offset += width
    w_alpha_p = jnp.concatenate(
        [w_alpha[0], jnp.zeros((RANK_PAD - GATE_RANK, GLA_K), w_alpha.dtype)], axis=0).astype(BF16)
    g_on = jnp.concatenate([jnp.tile(g_onorm_gla[0], GLA_HEADS),
                            jnp.tile(g_onorm_hgrn[0], HGRN_HEADS)])[None, :]
    masks = jnp.asarray(_level_masks())

    c_all = jnp.concatenate([c_prompt, c_sample], axis=0)
    mods = _ada_call(c_all, w_ada[0], b_ada, g_pre).reshape(c_all.shape[0], 3, D_MODEL)

    shared = (tuple(w_in_parts), w_alpha_p, b_alpha, hgrn_lb_logits, g_on, w_out[0].astype(BF16), g_post, masks)
    stg0_p = jnp.zeros((bp, N_GLA_PAIRS, LANES, PAIR_V), F32)
    sth0_p = jnp.zeros((bp, HGRN_HEADS, HGRN_D, HGRN_D), F32)
    yp, stg_p, hgrn_p = _mixer_call(x_prompt, mods[:bp], stg0_p, sth0_p, *shared, bb=1, tt=ROWS_PER_STEP)
    ys, stg_s, hgrn_s = _mixer_call(x_sample, mods[bp:], _pair_gla_states(state_gla[0]), state_hgrn[0],
                                    *shared, bb=ROWS_PER_STEP // x_sample.shape[1], tt=x_sample.shape[1])
    return (yp, ys, _unpair_gla_states(stg_p)[None], hgrn_p[None],
            _unpair_gla_states(stg_s)[None].astype(state_gla.dtype), hgrn_s[None].astype(state_hgrn.dtype))
```

```python
import functools
import math

import numpy as np
import jax
import jax.numpy as jnp
from jax import lax
from jax.experimental import pallas as pl
from jax.experimental.pallas import tpu as pltpu

D_MODEL = 1024
CHUNK = 64
GLA_HEADS = 4
GLA_V = D_MODEL // 2
GLA_K = GLA_V // 2
GLA_DK = GLA_K // GLA_HEADS
GLA_DV = GLA_V // GLA_HEADS
GATE_RANK = 16
GATE_TAU = 16.0
HGRN_HEADS = 4
HGRN_W = D_MODEL - GLA_V
HGRN_D = HGRN_W // HGRN_HEADS
IN_SPLITS = (GLA_K, GLA_K, GLA_V, GLA_V, GATE_RANK, HGRN_W, HGRN_W, HGRN_W, HGRN_W)
EPS = 1e-6
LOG2E = math.log2(math.e)

LANES = 128
SUBLANES = 8
K_LANES = GLA_K + HGRN_W
N_K_TILES = K_LANES // LANES
N_GLA_PAIRS = GLA_HEADS // 2
N_HGRN_PAIRS = HGRN_HEADS // 2
N_PAIRS = N_GLA_PAIRS + N_HGRN_PAIRS
PAIR_V = 2 * LANES
N_COL_BLOCKS = D_MODEL // PAIR_V
RANK_PAD = LANES
HALF_SIZES = (32, 16, 8, 4, 2, 1)
N_LEVELS = len(HALF_SIZES) + 1
MASK_CAUSAL = N_LEVELS
SINGLE_ANCHOR_MAX_LOG2 = 90.0
ROWS_PER_STEP = 256
VMEM_LIMIT_BYTES = 56 * 1024 * 1024

F32 = jnp.float32
BF16 = jnp.bfloat16
TN = (((0,), (0,)), ((), ()))


def _level_masks():
    t = np.arange(CHUNK)[:, None]
    s = np.arange(CHUNK)[None, :]
    masks = [(t // (2 * m)) == (s // (2 * m)) for m in HALF_SIZES]
    masks.append(t == s)
    masks.append(t >= s)
    masks = np.stack(masks).astype(np.float32)
    return np.concatenate([masks, masks], axis=2)


def _rms(x):
    return x * lax.rsqrt(jnp.mean(x * x, axis=-1, keepdims=True) + EPS)


def _sigmoid(x):
    return 1.0 / (1.0 + jnp.exp(-x))


def _log2_sigmoid(x):
    return jnp.minimum(x, 0.0) * LOG2E - jnp.log2(1.0 + jnp.exp(-jnp.abs(x)))


def _ada_kernel(c_ref, w_ref, b_ref, gpre_ref, out_ref):
    n = pl.program_id(0)
    val = jnp.dot(c_ref[...].astype(BF16), w_ref[...].astype(BF16),
                  preferred_element_type=F32) + b_ref[...]
    out_ref[...] = jnp.where(n == 1, gpre_ref[...] * (1.0 + val), val)


def _ada_call(c_all, w_ada, b_ada, g_pre):
    nb = c_all.shape[0]
    return pl.pallas_call(
        _ada_kernel,
        grid=(3,),
        in_specs=[
            pl.BlockSpec((nb, D_MODEL), lambda n: (0, 0)),
            pl.BlockSpec((D_MODEL, D_MODEL), lambda n: (0, n)),
            pl.BlockSpec((1, D_MODEL), lambda n: (0, n)),
            pl.BlockSpec((1, D_MODEL), lambda n: (0, 0)),
        ],
        out_specs=pl.BlockSpec((nb, D_MODEL), lambda n: (0, n)),
        out_shape=jax.ShapeDtypeStruct((nb, 3 * D_MODEL), F32),
        name="ada_mod",
    )(c_all, w_ada, b_ada, g_pre)


def _ref_rows(b_ref, block, off):
    pieces = []
    if block >= SUBLANES:
        for j in range(CHUNK // block):
            row = b_ref[j * block + off:j * block + off + 1, :]
            pieces.append(jnp.broadcast_to(row, (block, LANES)))
    else:
        assert block == SUBLANES // 2
        sub = lax.broadcasted_iota(jnp.int32, (SUBLANES, LANES), 0)
        for i in range(CHUNK // SUBLANES):
            lo = b_ref[SUBLANES * i + off:SUBLANES * i + off + 1, :]
            hi = b_ref[SUBLANES * i + block + off:SUBLANES * i + block + off + 1, :]
            pieces.append(jnp.where(sub < block,
                                    jnp.broadcast_to(lo, (SUBLANES, LANES)),
                                    jnp.broadcast_to(hi, (SUBLANES, LANES))))
    return jnp.concatenate(pieces, axis=0)


def _mixer_kernel(x_ref, mods_ref, stg0_ref, sth0_ref,
                  wqa_ref, wka_ref, wva_ref, wza_ref, wal_ref, wqh_ref, wfh_ref, wih_ref, wzh_ref,
                  walpha_ref, balpha_ref, lbl_ref,
                  gon_ref, wout_ref, gpost_ref, masks_ref,
                  y_ref, stg_ref, sth_ref,
                  hb_s, q_s, k_s, g_s, vbd_s, zs_s, o_s, b_s, qst_s, kag_s, kah_s, ksg_s, ksh_s, dcol_s, sbg_s, sbh_s,
                  lql_s, klg_s, klh_s, gb_s, po_s, stg_s,
                  *, bb, tt):
    rows = bb * tt
    n_chunks = rows // CHUNK
    chunks_per_seq = tt // CHUNK

    @pl.when((pl.program_id(0) == 0) & (pl.program_id(1) == 0))
    def _():
        for ref in (vbd_s, kah_s, klh_s, sbh_s):
            ref[...] = jnp.zeros(ref.shape, BF16)

    @pl.when(pl.program_id(1) == 0)
    def _():
        zero = jnp.zeros((GLA_DK, GLA_DV), F32)
        for bi in range(bb):
            for p in range(N_GLA_PAIRS):
                first, second = stg0_ref[bi, 2 * p], stg0_ref[bi, 2 * p + 1]
                stg_s[bi, p] = jnp.concatenate([jnp.concatenate([first, zero], axis=1),
                                                jnp.concatenate([zero, second], axis=1)], axis=0)
        sth_ref[...] = sth0_ref[...]

    def mod_rows(i):
        if bb == 1:
            return mods_ref[0, i:i + 1, :]
        return jnp.concatenate(
            [jnp.broadcast_to(mods_ref[bi, i:i + 1, :], (tt, D_MODEL)) for bi in range(bb)], axis=0)

    half = rows // 2
    x = x_ref[...].reshape(rows, D_MODEL)
    hb_s[...] = (_rms(x) * mod_rows(1) + mod_rows(0)).astype(BF16)

    def proj(w_ref, n=0):
        width = min(PAIR_V, w_ref.shape[1])
        return jnp.dot(hb_s[...], w_ref[:, n * width:(n + 1) * width], preferred_element_type=F32)

    def head_group_block(gla_ref, hgrn_ref, n):
        n_gla = GLA_V // PAIR_V
        return proj(gla_ref, n) if n < n_gla else proj(hgrn_ref, n - n_gla)

    a_lr = proj(wal_ref)
    ga = jnp.dot(a_lr.astype(BF16), walpha_ref[...], preferred_element_type=F32) + balpha_ref[...]
    g_s[:, :GLA_K] = _log2_sigmoid(ga) * (1.0 / GATE_TAU)

    k_s[:, :GLA_K] = proj(wka_ref)
    lbl = lbl_ref[...]
    lmax = jnp.max(lbl, axis=0, keepdims=True)
    lexp = jnp.exp(lbl - lmax)
    lb = lexp[0:1, :] / jnp.sum(lexp, axis=0, keepdims=True)
    for n in range(HGRN_W // PAIR_V):
        cols = slice(GLA_K + n * PAIR_V, GLA_K + (n + 1) * PAIR_V)
        lb_n = lb[:, n * PAIR_V:(n + 1) * PAIR_V]
        f = lb_n + (1.0 - lb_n) * _sigmoid(proj(wfh_ref, n))
        k_s[:, cols] = 1.0 - f
        g_s[:, cols] = jnp.log2(f)

    q_s[:, :GLA_K] = proj(wqa_ref) * (GLA_DK ** -0.5)
    for n in range(HGRN_W // PAIR_V):
        cols = slice(GLA_K + n * PAIR_V, GLA_K + (n + 1) * PAIR_V)
        qh = proj(wqh_ref, n)
        q_s[:, cols] = qh * _sigmoid(qh)

    def project_values():
        for p in range(N_PAIRS):
            pv = head_group_block(wva_ref, wih_ref, p).astype(BF16)
            for ci in range(n_chunks):
                r = slice(ci * CHUNK, (ci + 1) * CHUNK)
                vbd_s[p, ci, :CHUNK, :LANES] = pv[r, :LANES]
                vbd_s[p, ci, CHUNK:, LANES:] = pv[r, LANES:]

    def project_gates(n):
        cols = slice(n * PAIR_V, (n + 1) * PAIR_V)
        pz = head_group_block(wza_ref, wzh_ref, n)
        zs_s[:, cols] = pz * _sigmoid(pz)

    row_i = lax.broadcasted_iota(jnp.int32, (CHUNK, LANES), 0)
    lane_i = lax.broadcasted_iota(jnp.int32, (CHUNK, LANES), 1)
    head0 = lane_i < GLA_DK

    def tile_base(ci, lt):
        r = slice(ci * CHUNK, (ci + 1) * CHUNK)
        lanes = slice(lt * LANES, (lt + 1) * LANES)
        b_t = b_s.at[ci, lt]
        g = g_s[r, lanes]
        q = q_s[r, lanes]
        k = k_s[r, lanes]
        b = g
        for sh in (1, 2, 4):
            b = b + jnp.where(row_i >= sh, pltpu.roll(b, sh, 0), 0.0)
        for sh in (8, 16, 32):
            b = b + jnp.concatenate([jnp.zeros((sh, LANES), F32), b[:CHUNK - sh]], axis=0)
        b_t[...] = b
        b_last = b_t[CHUNK - 1:CHUNK, :]
        dcol_s[ci, lt] = jnp.exp2(jnp.broadcast_to(b_last, (LANES, LANES)).T)
        qst_s[r, lanes] = (q * jnp.exp2(b)).astype(BF16)
        k_state = k * jnp.exp2(b_last - b)
        if lt < N_GLA_PAIRS:
            ksg_s[lt, ci, :CHUNK, :] = jnp.where(head0, k_state, 0.0).astype(BF16)
            ksg_s[lt, ci, CHUNK:, :] = jnp.where(head0, 0.0, k_state).astype(BF16)
        else:
            ksh_s[lt - N_GLA_PAIRS, ci] = k_state.astype(BF16)
        return k * jnp.exp2(-b)

    def tile_reload(ci, lt):
        r = slice(ci * CHUNK, (ci + 1) * CHUNK)
        lanes = slice(lt * LANES, (lt + 1) * LANES)
        b_t = b_s.at[ci, lt]
        return q_s[r, lanes], k_s[r, lanes], g_s[r, lanes], b_t[...], b_t

    def tile_level(base, lv):
        q, k, g, b, b_t = base
        if lv == N_LEVELS - 1:
            return q, k
        m = HALF_SIZES[lv]
        upper = (row_i & (2 * m - 1)) >= m
        if m == 1:
            lhs, rhs = q * jnp.exp2(g), k
        else:
            e = jnp.exp2(-jnp.abs(b - _ref_rows(b_t, 2 * m, m - 1)))
            lhs, rhs = q * e, k * e
        return jnp.where(upper, lhs, 0.0), jnp.where(upper, 0.0, rhs)

    def store_keys_t(dst, p, rhs_tiles):
        if p < N_GLA_PAIRS:
            rhs = rhs_tiles[0]
            stacked = jnp.concatenate([jnp.where(head0, rhs, 0.0), jnp.where(head0, 0.0, rhs)], axis=0)
            dst[...] = stacked.astype(BF16).T
        else:
            kt = jnp.concatenate(rhs_tiles, axis=0).astype(BF16).T
            dst[:LANES, :CHUNK] = kt[:, :CHUNK]
            dst[LANES:, CHUNK:] = kt[:, CHUNK:]

    def tiles_of_pair(p):
        if p < N_GLA_PAIRS:
            return (p,)
        return tuple(N_GLA_PAIRS + 2 * (p - N_GLA_PAIRS) + pos for pos in range(2))

    def pair_lanes(p):
        if p < N_GLA_PAIRS:
            return slice(p * LANES, (p + 1) * LANES)
        return slice(GLA_K + (p - N_GLA_PAIRS) * PAIR_V, GLA_K + (p - N_GLA_PAIRS + 1) * PAIR_V)

    def anchor_keys(ci, p):
        return kag_s.at[p, ci] if p < N_GLA_PAIRS else kah_s.at[p - N_GLA_PAIRS, ci]

    def level_keys(lv, p, ci):
        return klg_s.at[lv, p, ci] if p < N_GLA_PAIRS else klh_s.at[lv, p - N_GLA_PAIRS, ci]

    def prep_pair(ci, p):
        store_keys_t(anchor_keys(ci, p), p, [tile_base(ci, lt) for lt in tiles_of_pair(p)])

    def prep_pair_levels(ci, p):
        tiles = tiles_of_pair(p)
        bases = [tile_reload(ci, lt) for lt in tiles]
        for lv in range(N_LEVELS):
            rhs_tiles = []
            for lt, base in zip(tiles, bases):
                lhs, rhs = tile_level(base, lv)
                lql_s[lv, ci * CHUNK:(ci + 1) * CHUNK, lt * LANES:(lt + 1) * LANES] = lhs.astype(BF16)
                rhs_tiles.append(rhs)
            store_keys_t(level_keys(lv, p, ci), p, rhs_tiles)

    def pair_attention(ci, p, single_anchor):
        lanes = pair_lanes(p)
        if single_anchor:
            prod = jnp.dot(qst_s[ci * CHUNK:(ci + 1) * CHUNK, lanes], anchor_keys(ci, p)[...],
                           preferred_element_type=F32)
            return jnp.where(masks_ref[MASK_CAUSAL] > 0.0, prod, 0.0).astype(BF16)
        att = None
        for lv in range(N_LEVELS):
            prod = jnp.dot(lql_s[lv, ci * CHUNK:(ci + 1) * CHUNK, lanes], level_keys(lv, p, ci)[...],
                           preferred_element_type=F32)
            if lv > 0:
                prod = prod * masks_ref[lv]
            att = prod if att is None else att + prod
        return att.astype(BF16)

    def snapshot_states(ci):
        bi = ci // chunks_per_seq
        for p in range(N_GLA_PAIRS):
            sbg_s[ci, p] = stg_s[bi, p].astype(BF16)
        for hh in range(HGRN_HEADS):
            pos = hh % 2
            sbh_s[ci, hh // 2, pos * LANES:(pos + 1) * LANES, pos * LANES:(pos + 1) * LANES] = (
                sth_ref[bi, hh].astype(BF16))

    def update_states(ci):
        bi = ci // chunks_per_seq
        for p in range(N_GLA_PAIRS):
            dcol = dcol_s[ci, p]
            stg_s[bi, p] = (stg_s[bi, p] * jnp.concatenate([dcol, dcol], axis=1)
                              + lax.dot_general(ksg_s[p, ci], vbd_s[p, ci], TN, preferred_element_type=F32))
        for hh in range(HGRN_HEADS):
            pg, pos = N_GLA_PAIRS + hh // 2, hh % 2
            v_head = vbd_s[pg, ci, pos * CHUNK:(pos + 1) * CHUNK, pos * LANES:(pos + 1) * LANES]
            sth_ref[bi, hh] = sth_ref[bi, hh] * dcol_s[ci, N_GLA_PAIRS + hh] + lax.dot_general(
                ksh_s[hh, ci], v_head, TN, preferred_element_type=F32)

    def pair_output(ci, p, att):
        r = slice(ci * CHUNK, (ci + 1) * CHUNK)
        state = sbg_s[ci, p] if p < N_GLA_PAIRS else sbh_s[ci, p - N_GLA_PAIRS]
        lhs = jnp.concatenate([att, qst_s[r, pair_lanes(p)]], axis=1)
        rhs = jnp.concatenate([vbd_s[p, ci], state], axis=0)
        o_s[r, p * PAIR_V:(p + 1) * PAIR_V] = jnp.dot(lhs, rhs, preferred_element_type=F32)

    gate_rows = mod_rows(2)

    def gate_half(h):
        rr = slice(h * half, (h + 1) * half)
        for ht in range(GLA_HEADS + HGRN_HEADS):
            lanes = slice(ht * LANES, (ht + 1) * LANES)
            gb_s[rr, lanes] = (_rms(o_s[rr, lanes]) * gon_ref[:, lanes] * zs_s[rr, lanes]).astype(BF16)

    def project_half(h, n):
        rr = slice(h * half, (h + 1) * half)
        cols = slice(n * PAIR_V, (n + 1) * PAIR_V)
        po_s[rr, cols] = jnp.dot(gb_s[rr, :], wout_ref[:, cols], preferred_element_type=F32)

    def post_half(h):
        rr = slice(h * half, (h + 1) * half)
        gate_h = gate_rows if bb == 1 else gate_rows[rr]
        res = gate_h * (_rms(po_s[rr, :]) * gpost_ref[...])
        if bb == 1:
            y_ref[0, rr, :] = x_ref[0, rr, :] + res
        else:
            seqs = slice(h * (bb // 2), (h + 1) * (bb // 2))
            y_ref[seqs] = x_ref[seqs] + res.reshape(bb // 2, tt, D_MODEL)

    def finish_stages(h):
        return ([functools.partial(project_half, h, n) for n in range(N_COL_BLOCKS)]
                + [functools.partial(post_half, h)])

    fillers = [functools.partial(project_gates, n) for n in range(N_COL_BLOCKS)]
    atts = {}
    for step in range(n_chunks + 2):
        if step < n_chunks:
            for p in range(N_PAIRS):
                prep_pair(step, p)
        if step == 0:
            project_values()
        if 1 <= step <= n_chunks:
            snapshot_states(step - 1)
            if step < n_chunks:
                update_states(step - 1)
            atts[step - 1] = [pair_attention(step - 1, p, True) for p in range(N_PAIRS)]
        if fillers:
            fillers.pop(0)()
        if step >= 2:
            for p in range(N_PAIRS):
                pair_output(step - 2, p, atts[step - 2][p])
            if (step - 1) * CHUNK == half:
                while fillers:
                    fillers.pop(0)()
                gate_half(0)
                fillers = finish_stages(0)
    last_stages = finish_stages(1)
    for stage in ([functools.partial(gate_half, 1)] + fillers + last_stages[:-1]
                  + [functools.partial(update_states, n_chunks - 1)] + last_stages[-1:]):
        stage()

    total = None
    for ci in range(n_chunks):
        t = jnp.sum(g_s[ci * CHUNK:(ci + 1) * CHUNK, :], axis=0, keepdims=True)
        total = t if total is None else jnp.minimum(total, t)
    mild = jnp.min(total) >= -SINGLE_ANCHOR_MAX_LOG2

    @pl.when(jnp.logical_not(mild))
    def _():
        for ci in range(n_chunks):
            for p in range(N_PAIRS):
                prep_pair_levels(ci, p)
            for p in range(N_PAIRS):
                pair_output(ci, p, pair_attention(ci, p, False))
        for h in range(2):
            for stage in [functools.partial(gate_half, h)] + finish_stages(h):
                stage()

    @pl.when(pl.program_id(1) == pl.num_programs(1) - 1)
    def _():
        for bi in range(bb):
            for p in range(N_GLA_PAIRS):
                for h in range(2):
                    stg_ref[bi, 2 * p + h] = stg_s[bi, p, h * GLA_DK:(h + 1) * GLA_DK,
                                                   h * GLA_DV:(h + 1) * GLA_DV]


def _mixer_call(x, mods, stg0, sth0, w_in_parts, w_alpha, b_alpha, lb_logits, g_on, w_out, g_post, masks,
                *, bb, tt):
    nb, seq, _ = x.shape
    rows = bb * tt
    assert nb % bb == 0 and seq % tt == 0 and tt % CHUNK == 0
    n_chunks = rows // CHUNK
    per_seq4 = lambda b, j: (b, 0, 0, 0)

    def resident(shape):
        return pl.BlockSpec(shape, lambda b, j: (0,) * len(shape), pipeline_mode=pl.Buffered(1))

    kernel = functools.partial(_mixer_kernel, bb=bb, tt=tt)
    stg_shape = (nb, GLA_HEADS, GLA_DK, GLA_DV)
    sth_shape = (nb, HGRN_HEADS, HGRN_D, HGRN_D)
    return pl.pallas_call(
        kernel,
        grid=(nb // bb, seq // tt),
        in_specs=[
            pl.BlockSpec((bb, tt, D_MODEL), lambda b, j: (b, j, 0)),
            pl.BlockSpec((bb, 3, D_MODEL), lambda b, j: (b, 0, 0)),
            pl.BlockSpec((bb,) + stg_shape[1:], per_seq4),
            pl.BlockSpec((bb,) + sth_shape[1:], per_seq4),
            *[resident(part.shape) for part in w_in_parts],
            resident((RANK_PAD, GLA_K)),
            resident((1, GLA_K)),
            resident(lb_logits.shape),
            resident((1, D_MODEL)),
            resident((D_MODEL, D_MODEL)),
            resident((1, D_MODEL)),
            resident(masks.shape),
        ],
        out_specs=[
            pl.BlockSpec((bb, tt, D_MODEL), lambda b, j: (b, j, 0)),
            pl.BlockSpec((bb,) + stg_shape[1:], per_seq4),
            pl.BlockSpec((bb,) + sth_shape[1:], per_seq4),
        ],
        out_shape=[
            jax.ShapeDtypeStruct(x.shape, F32),
            jax.ShapeDtypeStruct(stg_shape, F32),
            jax.ShapeDtypeStruct(sth_shape, F32),
        ],
        scratch_shapes=[
            pltpu.VMEM((rows, D_MODEL), BF16),
            pltpu.VMEM((rows, K_LANES), F32),
            pltpu.VMEM((rows, K_LANES), F32),
            pltpu.VMEM((rows, K_LANES), F32),
            pltpu.VMEM((N_PAIRS, n_chunks, 2 * CHUNK, PAIR_V), BF16),
            pltpu.VMEM((rows, D_MODEL), F32),
            pltpu.VMEM((rows, D_MODEL), F32),
            pltpu.VMEM((n_chunks, N_K_TILES, CHUNK, LANES), F32),
            pltpu.VMEM((rows, K_LANES), BF16),
            pltpu.VMEM((N_GLA_PAIRS, n_chunks, LANES, 2 * CHUNK), BF16),
            pltpu.VMEM((N_HGRN_PAIRS, n_chunks, PAIR_V, 2 * CHUNK), BF16),
            pltpu.VMEM((N_GLA_PAIRS, n_chunks, 2 * CHUNK, LANES), BF16),
            pltpu.VMEM((HGRN_HEADS, n_chunks, CHUNK, LANES), BF16),
            pltpu.VMEM((n_chunks, N_K_TILES, LANES, LANES), F32),
            pltpu.VMEM((n_chunks, N_GLA_PAIRS, LANES, PAIR_V), BF16),
            pltpu.VMEM((n_chunks, N_HGRN_PAIRS, PAIR_V, PAIR_V), BF16),
            pltpu.VMEM((N_LEVELS, rows, K_LANES), BF16),
            pltpu.VMEM((N_LEVELS, N_GLA_PAIRS, n_chunks, LANES, 2 * CHUNK), BF16),
            pltpu.VMEM((N_LEVELS, N_HGRN_PAIRS, n_chunks, PAIR_V, 2 * CHUNK), BF16),
            pltpu.VMEM((rows, D_MODEL), BF16),
            pltpu.VMEM((rows, D_MODEL), F32),
            pltpu.VMEM((bb, N_GLA_PAIRS, LANES, PAIR_V), F32),
        ],
        compiler_params=pltpu.CompilerParams(
            dimension_semantics=("arbitrary", "arbitrary"),
            vmem_limit_bytes=VMEM_LIMIT_BYTES,
        ),
        name="mixer_bb%d_tt%d" % (bb, tt),
    )(x, mods, stg0, sth0, *w_in_parts, w_alpha, b_alpha, lb_logits, g_on, w_out, g_post, masks)


def kernel(x_prompt, x_sample, c_prompt, c_sample, state_gla, state_hgrn, w_ada, b_ada, g_pre, w_in,
           w_alpha, b_alpha, g_onorm_gla, hgrn_lb_logits, g_onorm_hgrn, w_out, g_post):
    assert w_ada.shape[0] == 1, "single-layer problem"
    bp = x_prompt.shape[0]

    w_in_parts, offset = [], 0
    for width in IN_SPLITS:
        part = w_in[0][:, offset:offset + width].astype(BF16)
        if width < LANES:
            part = jnp.pad(part, ((0, 0), (0, RANK_PAD - width)))
        w_in_parts.append(part)
        offset += width
    w_alpha_p = jnp.concatenate(
        [w_alpha[0], jnp.zeros((RANK_PAD - GATE_RANK, GLA_K), w_alpha.dtype)], axis=0).astype(BF16)
    g_on = jnp.concatenate([jnp.tile(g_onorm_gla[0], GLA_HEADS),
                            jnp.tile(g_onorm_hgrn[0], HGRN_HEADS)])[None, :]
    masks = jnp.asarray(_level_masks())

    c_all = jnp.concatenate([c_prompt, c_sample], axis=0)
    mods = _ada_call(c_all, w_ada[0], b_ada, g_pre).reshape(c_all.shape[0], 3, D_MODEL)

    shared = (tuple(w_in_parts), w_alpha_p, b_alpha, hgrn_lb_logits, g_on, w_out[0].astype(BF16), g_post, masks)
    stg0_p = jnp.zeros((bp, GLA_HEADS, GLA_DK, GLA_DV), F32)
    sth0_p = jnp.zeros((bp, HGRN_HEADS, HGRN_D, HGRN_D), F32)
    yp, gla_p, hgrn_p = _mixer_call(x_prompt, mods[:bp], stg0_p, sth0_p, *shared, bb=1, tt=ROWS_PER_STEP)
    ys, gla_s, hgrn_s = _mixer_call(x_sample, mods[bp:], state_gla[0], state_hgrn[0],
                                    *shared, bb=ROWS_PER_STEP // x_sample.shape[1], tt=x_sample.shape[1])
    return (yp, ys, gla_p[None], hgrn_p[None],
            gla_s[None].astype(state_gla.dtype), hgrn_s[None].astype(state_hgrn.dtype))
```

```python
import functools
import math

import numpy as np
import jax
import jax.numpy as jnp
from jax import lax
from jax.experimental import pallas as pl
from jax.experimental.pallas import tpu as pltpu

D_MODEL = 1024
CHUNK = 64
GLA_HEADS = 4
GLA_V = D_MODEL // 2
GLA_K = GLA_V // 2
GLA_DK = GLA_K // GLA_HEADS
GLA_DV = GLA_V // GLA_HEADS
GATE_RANK = 16
GATE_TAU = 16.0
HGRN_HEADS = 4
HGRN_W = D_MODEL - GLA_V
HGRN_D = HGRN_W // HGRN_HEADS
IN_SPLITS = (GLA_K, GLA_K, GLA_V, GLA_V, GATE_RANK, HGRN_W, HGRN_W, HGRN_W, HGRN_W)
EPS = 1e-6
LOG2E = math.log2(math.e)

LANES = 128
SUBLANES = 8
K_LANES = GLA_K + HGRN_W
N_K_TILES = K_LANES // LANES
N_GLA_PAIRS = GLA_HEADS // 2
N_HGRN_PAIRS = HGRN_HEADS // 2
N_PAIRS = N_GLA_PAIRS + N_HGRN_PAIRS
PAIR_V = 2 * LANES
N_COL_BLOCKS = D_MODEL // PAIR_V
RANK_PAD = LANES
HALF_SIZES = (32, 16, 8, 4, 2, 1)
N_LEVELS = len(HALF_SIZES) + 1
MASK_CAUSAL = N_LEVELS
SINGLE_ANCHOR_MAX_LOG2 = 90.0
ROWS_PER_STEP = 256
VMEM_LIMIT_BYTES = 56 * 1024 * 1024

F32 = jnp.float32
BF16 = jnp.bfloat16
TN = (((0,), (0,)), ((), ()))


def _level_masks():
    t = np.arange(CHUNK)[:, None]
    s = np.arange(CHUNK)[None, :]
    masks = [(t // (2 * m)) == (s // (2 * m)) for m in HALF_SIZES]
    masks.append(t == s)
    masks.append(t >= s)
    masks = np.stack(masks).astype(np.float32)
    return np.concatenate([masks, masks], axis=2)


def _rms(x):
    return x * lax.rsqrt(jnp.mean(x * x, axis=-1, keepdims=True) + EPS)


def _sigmoid(x):
    return 1.0 / (1.0 + jnp.exp(-x))


def _log2_sigmoid(x):
    return jnp.minimum(x, 0.0) * LOG2E - jnp.log2(1.0 + jnp.exp(-jnp.abs(x)))


def _ada_kernel(c_ref, w_ref, b_ref, gpre_ref, out_ref):
    n = pl.program_id(0)
    val = jnp.dot(c_ref[...].astype(BF16), w_ref[...].astype(BF16),
                  preferred_element_type=F32) + b_ref[...]
    out_ref[...] = jnp.where(n == 1, gpre_ref[...] * (1.0 + val), val)


def _ada_call(c_all, w_ada, b_ada, g_pre):
    nb = c_all.shape[0]
    return pl.pallas_call(
        _ada_kernel,
        grid=(3,),
        in_specs=[
            pl.BlockSpec((nb, D_MODEL), lambda n: (0, 0)),
            pl.BlockSpec((D_MODEL, D_MODEL), lambda n: (0, n)),
            pl.BlockSpec((1, D_MODEL), lambda n: (0, n)),
            pl.BlockSpec((1, D_MODEL), lambda n: (0, 0)),
        ],
        out_specs=pl.BlockSpec((nb, D_MODEL), lambda n: (0, n)),
        out_shape=jax.ShapeDtypeStruct((nb, 3 * D_MODEL), F32),
        name="ada_mod",
    )(c_all, w_ada, b_ada, g_pre)


def _ref_rows(b_ref, block, off):
    pieces = []
    if block >= SUBLANES:
        for j in range(CHUNK // block):
            row = b_ref[j * block + off:j * block + off + 1, :]
            pieces.append(jnp.broadcast_to(row, (block, LANES)))
    else:
        assert block == SUBLANES // 2
        sub = lax.broadcasted_iota(jnp.int32, (SUBLANES, LANES), 0)
        for i in range(CHUNK // SUBLANES):
            lo = b_ref[SUBLANES * i + off:SUBLANES * i + off + 1, :]
            hi = b_ref[SUBLANES * i + block + off:SUBLANES * i + block + off + 1, :]
            pieces.append(jnp.where(sub < block,
                                    jnp.broadcast_to(lo, (SUBLANES, LANES)),
                                    jnp.broadcast_to(hi, (SUBLANES, LANES))))
    return jnp.concatenate(pieces, axis=0)


def _mixer_kernel(x_ref, mods_ref, stg0_ref, sth0_ref,
                  wqa_ref, wka_ref, wva_ref, wza_ref, wal_ref, wqh_ref, wfh_ref, wih_ref, wzh_ref,
                  walpha_ref, balpha_ref, lbl_ref,
                  gon_ref, wout_ref, gpost_ref, masks_ref,
                  y_ref, stg_ref, sth_ref,
                  hb_s, q_s, k_s, g_s, vbd_s, zs_s, o_s, b_s, qst_s, kag_s, kah_s, ksg_s, ksh_s, dcol_s, sbg_s, sbh_s,
                  lql_s, klg_s, klh_s, gb_s, po_s, stg_s,
                  *, bb, tt):
    rows = bb * tt
    n_chunks = rows // CHUNK
    chunks_per_seq = tt // CHUNK

    @pl.when((pl.program_id(0) == 0) & (pl.program_id(1) == 0))
    def _():
        for ref in (vbd_s, kah_s, klh_s, sbh_s):
            ref[...] = jnp.zeros(ref.shape, BF16)

    @pl.when(pl.program_id(1) == 0)
    def _():
        zero = jnp.zeros((GLA_DK, GLA_DV), F32)
        for bi in range(bb):
            for p in range(N_GLA_PAIRS):
                first, second = stg0_ref[bi, 2 * p], stg0_ref[bi, 2 * p + 1]
                stg_s[bi, p] = jnp.concatenate([jnp.concatenate([first, zero], axis=1),
                                                jnp.concatenate([zero, second], axis=1)], axis=0)
        sth_ref[...] = sth0_ref[...]

    def mod_rows(i):
        if bb == 1:
            return mods_ref[0, i:i + 1, :]
        return jnp.concatenate(
            [jnp.broadcast_to(mods_ref[bi, i:i + 1, :], (tt, D_MODEL)) for bi in range(bb)], axis=0)

    half = rows // 2
    x = x_ref[...].reshape(rows, D_MODEL)
    hb_s[...] = (_rms(x) * mod_rows(1) + mod_rows(0)).astype(BF16)

    def proj(w_ref, n=0):
        width = min(PAIR_V, w_ref.shape[1])
        return jnp.dot(hb_s[...], w_ref[:, n * width:(n + 1) * width], preferred_element_type=F32)

    def head_group_block(gla_ref, hgrn_ref, n):
        n_gla = GLA_V // PAIR_V
        return proj(gla_ref, n) if n < n_gla else proj(hgrn_ref, n - n_gla)

    a_lr = proj(wal_ref)
    ga = jnp.dot(a_lr.astype(BF16), walpha_ref[...], preferred_element_type=F32) + balpha_ref[...]
    g_s[:, :GLA_K] = _log2_sigmoid(ga) * (1.0 / GATE_TAU)

    k_s[:, :GLA_K] = proj(wka_ref)
    lbl = lbl_ref[...]
    lmax = jnp.max(lbl, axis=0, keepdims=True)
    lexp = jnp.exp(lbl - lmax)
    lb = lexp[0:1, :] / jnp.sum(lexp, axis=0, keepdims=True)
    for n in range(HGRN_W // PAIR_V):
        cols = slice(GLA_K + n * PAIR_V, GLA_K + (n + 1) * PAIR_V)
        lb_n = lb[:, n * PAIR_V:(n + 1) * PAIR_V]
        f = lb_n + (1.0 - lb_n) * _sigmoid(proj(wfh_ref, n))
        k_s[:, cols] = 1.0 - f
        g_s[:, cols] = jnp.log2(f)

    q_s[:, :GLA_K] = proj(wqa_ref) * (GLA_DK ** -0.5)
    for n in range(HGRN_W // PAIR_V):
        cols = slice(GLA_K + n * PAIR_V, GLA_K + (n + 1) * PAIR_V)
        qh = proj(wqh_ref, n)
        q_s[:, cols] = qh * _sigmoid(qh)

    def project_values():
        for p in range(N_PAIRS):
            pv = head_group_block(wva_ref, wih_ref, p).astype(BF16)
            for ci in range(n_chunks):
                r = slice(ci * CHUNK, (ci + 1) * CHUNK)
                vbd_s[p, ci, :CHUNK, :LANES] = pv[r, :LANES]
                vbd_s[p, ci, CHUNK:, LANES:] = pv[r, LANES:]

    def project_gates(n):
        cols = slice(n * PAIR_V, (n + 1) * PAIR_V)
        pz = head_group_block(wza_ref, wzh_ref, n)
        zs_s[:, cols] = pz * _sigmoid(pz)

    row_i = lax.broadcasted_iota(jnp.int32, (CHUNK, LANES), 0)
    lane_i = lax.broadcasted_iota(jnp.int32, (CHUNK, LANES), 1)
    head0 = lane_i < GLA_DK

    def tile_base(ci, lt):
        r = slice(ci * CHUNK, (ci + 1) * CHUNK)
        lanes = slice(lt * LANES, (lt + 1) * LANES)
        b_t = b_s.at[ci, lt]
        g = g_s[r, lanes]
        q = q_s[r, lanes]
        k = k_s[r, lanes]
        b = g
        for sh in (1, 2, 4):
            b = b + jnp.where(row_i >= sh, pltpu.roll(b, sh, 0), 0.0)
        for sh in (8, 16, 32):
            b = b + jnp.concatenate([jnp.zeros((sh, LANES), F32), b[:CHUNK - sh]], axis=0)
        b_t[...] = b
        b_last = b_t[CHUNK - 1:CHUNK, :]
        dcol_s[ci, lt] = jnp.exp2(jnp.broadcast_to(b_last, (LANES, LANES)).T)
        qst_s[r, lanes] = (q * jnp.exp2(b)).astype(BF16)
        k_state = k * jnp.exp2(b_last - b)
        if lt < N_GLA_PAIRS:
            ksg_s[lt, ci, :CHUNK, :] = jnp.where(head0, k_state, 0.0).astype(BF16)
            ksg_s[lt, ci, CHUNK:, :] = jnp.where(head0, 0.0, k_state).astype(BF16)
        else:
            ksh_s[lt - N_GLA_PAIRS, ci] = k_state.astype(BF16)
        return k * jnp.exp2(-b)

    def tile_reload(ci, lt):
        r = slice(ci * CHUNK, (ci + 1) * CHUNK)
        lanes = slice(lt * LANES, (lt + 1) * LANES)
        b_t = b_s.at[ci, lt]
        return q_s[r, lanes], k_s[r, lanes], g_s[r, lanes], b_t[...], b_t

    def tile_level(base, lv):
        q, k, g, b, b_t = base
        if lv == N_LEVELS - 1:
            return q, k
        m = HALF_SIZES[lv]
        upper = (row_i & (2 * m - 1)) >= m
        if m == 1:
            lhs, rhs = q * jnp.exp2(g), k
        else:
            e = jnp.exp2(-jnp.abs(b - _ref_rows(b_t, 2 * m, m - 1)))
            lhs, rhs = q * e, k * e
        return jnp.where(upper, lhs, 0.0), jnp.where(upper, 0.0, rhs)

    def store_keys_t(dst, p, rhs_tiles):
        if p < N_GLA_PAIRS:
            rhs = rhs_tiles[0]
            stacked = jnp.concatenate([jnp.where(head0, rhs, 0.0), jnp.where(head0, 0.0, rhs)], axis=0)
            dst[...] = stacked.astype(BF16).T
        else:
            kt = jnp.concatenate(rhs_tiles, axis=0).astype(BF16).T
            dst[:LANES, :CHUNK] = kt[:, :CHUNK]
            dst[LANES:, CHUNK:] = kt[:, CHUNK:]

    def tiles_of_pair(p):
        if p < N_GLA_PAIRS:
            return (p,)
        return tuple(N_GLA_PAIRS + 2 * (p - N_GLA_PAIRS) + pos for pos in range(2))

    def pair_lanes(p):
        if p < N_GLA_PAIRS:
            return slice(p * LANES, (p + 1) * LANES)
        return slice(GLA_K + (p - N_GLA_PAIRS) * PAIR_V, GLA_K + (p - N_GLA_PAIRS + 1) * PAIR_V)

    def anchor_keys(ci, p):
        return kag_s.at[p, ci] if p < N_GLA_PAIRS else kah_s.at[p - N_GLA_PAIRS, ci]

    def level_keys(lv, p, ci):
        return klg_s.at[lv, p, ci] if p < N_GLA_PAIRS else klh_s.at[lv, p - N_GLA_PAIRS, ci]

    def prep_pair(ci, p):
        store_keys_t(anchor_keys(ci, p), p, [tile_base(ci, lt) for lt in tiles_of_pair(p)])

    def prep_pair_levels(ci, p):
        tiles = tiles_of_pair(p)
        bases = [tile_reload(ci, lt) for lt in tiles]
        for lv in range(N_LEVELS):
            rhs_tiles = []
            for lt, base in zip(tiles, bases):
                lhs, rhs = tile_level(base, lv)
                lql_s[lv, ci * CHUNK:(ci + 1) * CHUNK, lt * LANES:(lt + 1) * LANES] = lhs.astype(BF16)
                rhs_tiles.append(rhs)
            store_keys_t(level_keys(lv, p, ci), p, rhs_tiles)

    def pair_attention(ci, p, single_anchor):
        lanes = pair_lanes(p)
        if single_anchor:
            prod = jnp.dot(qst_s[ci * CHUNK:(ci + 1) * CHUNK, lanes], anchor_keys(ci, p)[...],
                           preferred_element_type=F32)
            return jnp.where(masks_ref[MASK_CAUSAL] > 0.0, prod, 0.0).astype(BF16)
        att = None
        for lv in range(N_LEVELS):
            prod = jnp.dot(lql_s[lv, ci * CHUNK:(ci + 1) * CHUNK, lanes], level_keys(lv, p, ci)[...],
                           preferred_element_type=F32)
            if lv > 0:
                prod = prod * masks_ref[lv]
            att = prod if att is None else att + prod
        return att.astype(BF16)

    def snapshot_states(ci):
        bi = ci // chunks_per_seq
        for p in range(N_GLA_PAIRS):
            sbg_s[ci, p] = stg_s[bi, p].astype(BF16)
        for hh in range(HGRN_HEADS):
            pos = hh % 2
            sbh_s[ci, hh // 2, pos * LANES:(pos + 1) * LANES, pos * LANES:(pos + 1) * LANES] = (
                sth_ref[bi, hh].astype(BF16))

    def update_states(ci):
        bi = ci // chunks_per_seq
        for p in range(N_GLA_PAIRS):
            dcol = dcol_s[ci, p]
            stg_s[bi, p] = (stg_s[bi, p] * jnp.concatenate([dcol, dcol], axis=1)
                              + lax.dot_general(ksg_s[p, ci], vbd_s[p, ci], TN, preferred_element_type=F32))
        for hh in range(HGRN_HEADS):
            pg, pos = N_GLA_PAIRS + hh // 2, hh % 2
            v_head = vbd_s[pg, ci, pos * CHUNK:(pos + 1) * CHUNK, pos * LANES:(pos + 1) * LANES]
            sth_ref[bi, hh] = sth_ref[bi, hh] * dcol_s[ci, N_GLA_PAIRS + hh] + lax.dot_general(
                ksh_s[hh, ci], v_head, TN, preferred_element_type=F32)

    def pair_output(ci, p, att):
        r = slice(ci * CHUNK, (ci + 1) * CHUNK)
        state = sbg_s[ci, p] if p < N_GLA_PAIRS else sbh_s[ci, p - N_GLA_PAIRS]
        lhs = jnp.concatenate([att, qst_s[r, pair_lanes(p)]], axis=1)
        rhs = jnp.concatenate([vbd_s[p, ci], state], axis=0)
        o_s[r, p * PAIR_V:(p + 1) * PAIR_V] = jnp.dot(lhs, rhs, preferred_element_type=F32)

    gate_rows = mod_rows(2)

    def gate_half(h):
        rr = slice(h * half, (h + 1) * half)
        for ht in range(GLA_HEADS + HGRN_HEADS):
            lanes = slice(ht * LANES, (ht + 1) * LANES)
            gb_s[rr, lanes] = (_rms(o_s[rr, lanes]) * gon_ref[:, lanes] * zs_s[rr, lanes]).astype(BF16)

    def project_half(h, n):
        rr = slice(h * half, (h + 1) * half)
        cols = slice(n * PAIR_V, (n + 1) * PAIR_V)
        po_s[rr, cols] = jnp.dot(gb_s[rr, :], wout_ref[:, cols], preferred_element_type=F32)

    def post_half(h):
        rr = slice(h * half, (h + 1) * half)
        gate_h = gate_rows if bb == 1 else gate_rows[rr]
        res = gate_h * (_rms(po_s[rr, :]) * gpost_ref[...])
        if bb == 1:
            y_ref[0, rr, :] = x_ref[0, rr, :] + res
        else:
            seqs = slice(h * (bb // 2), (h + 1) * (bb // 2))
            y_ref[seqs] = x_ref[seqs] + res.reshape(bb // 2, tt, D_MODEL)

    def finish_stages(h):
        return ([functools.partial(project_half, h, n) for n in range(N_COL_BLOCKS)]
                + [functools.partial(post_half, h)])

    fillers = [functools.partial(project_gates, n) for n in range(N_COL_BLOCKS)]
    atts = {}
    for step in range(n_chunks + 2):
        if step < n_chunks:
            for p in range(N_PAIRS):
                prep_pair(step, p)
        if step == 0:
            project_values()
        if 1 <= step <= n_chunks:
            snapshot_states(step - 1)
            if step < n_chunks:
                update_states(step - 1)
            atts[step - 1] = [pair_attention(step - 1, p, True) for p in range(N_PAIRS)]
        if fillers:
            fillers.pop(0)()
        if step >= 2:
            for p in range(N_PAIRS):
                pair_output(step - 2, p, atts[step - 2][p])
            if (step - 1) * CHUNK == half:
                while fillers:
                    fillers.pop(0)()
                gate_half(0)
                fillers = finish_stages(0)
    last_stages = finish_stages(1)
    for stage in ([functools.partial(gate_half, 1)] + fillers + last_stages[:-1]
                  + [functools.partial(update_states, n_chunks - 1)] + last_stages[-1:]):
        stage()

    total = None
    for ci in range(n_chunks):
        t = jnp.sum(g_s[ci * CHUNK:(ci + 1) * CHUNK, :], axis=0, keepdims=True)
        total = t if total is None else jnp.minimum(total, t)
    mild = jnp.min(total) >= -SINGLE_ANCHOR_MAX_LOG2

    @pl.when(jnp.logical_not(mild))
    def _():
        for ci in range(n_chunks):
            for p in range(N_PAIRS):
                prep_pair_levels(ci, p)
            for p in range(N_PAIRS):
                pair_output(ci, p, pair_attention(ci, p, False))
        for h in range(2):
            for stage in [functools.partial(gate_half, h)] + finish_stages(h):
                stage()

    @pl.when(pl.program_id(1) == pl.num_programs(1) - 1)
    def _():
        for bi in range(bb):
            for p in range(N_GLA_PAIRS):
                for h in range(2):
                    stg_ref[bi, 2 * p + h] = stg_s[bi, p, h * GLA_DK:(h + 1) * GLA_DK,
                                                   h * GLA_DV:(h + 1) * GLA_DV]


def _mixer_call(x, mods, stg0, sth0, w_in_parts, w_alpha, b_alpha, lb_logits, g_on, w_out, g_post, masks,
                *, bb, tt):
    nb, seq, _ = x.shape
    rows = bb * tt
    assert nb % bb == 0 and seq % tt == 0 and tt % CHUNK == 0
    n_chunks = rows // CHUNK
    per_seq4 = lambda b, j: (b, 0, 0, 0)

    def resident(shape):
        return pl.BlockSpec(shape, lambda b, j: (0,) * len(shape), pipeline_mode=pl.Buffered(1))

    def initial_state(st0):
        assert st0.shape[0] in (bb, nb)
        shared = st0.shape[0] == bb and nb != bb
        return pl.BlockSpec((bb,) + st0.shape[1:], (lambda b, j: (0, 0, 0, 0)) if shared else per_seq4)

    kernel = functools.partial(_mixer_kernel, bb=bb, tt=tt)
    stg_shape =(nb, GLA_HEADS, GLA_DK, GLA_DV)
    sth_shape = (nb, HGRN_HEADS, HGRN_D, HGRN_D)
    return pl.pallas_call(
        kernel,
        grid=(nb // bb, seq // tt),
        in_specs=[
            pl.BlockSpec((bb, tt, D_MODEL), lambda b, j: (b, j, 0)),
            pl.BlockSpec((bb, 3, D_MODEL), lambda b, j: (b, 0, 0)),
            initial_state(stg0),
            initial_state(sth0),
            *[resident(part.shape) for part in w_in_parts],
            resident((RANK_PAD, GLA_K)),
            resident((1, GLA_K)),
            resident(lb_logits.shape),
            resident((1, D_MODEL)),
            resident((D_MODEL, D_MODEL)),
            resident((1, D_MODEL)),
            resident(masks.shape),
        ],
        out_specs=[
            pl.BlockSpec((bb, tt, D_MODEL), lambda b, j: (b, j, 0)),
            pl.BlockSpec((bb,) + stg_shape[1:], per_seq4),
            pl.BlockSpec((bb,) + sth_shape[1:], per_seq4),
        ],
        out_shape=[
            jax.ShapeDtypeStruct(x.shape, F32),
            jax.ShapeDtypeStruct(stg_shape, F32),
            jax.ShapeDtypeStruct(sth_shape, F32),
        ],
        scratch_shapes=[
            pltpu.VMEM((rows, D_MODEL), BF16),
            pltpu.VMEM((rows, K_LANES), F32),
            pltpu.VMEM((rows, K_LANES), F32),
            pltpu.VMEM((rows, K_LANES), F32),
            pltpu.VMEM((N_PAIRS, n_chunks, 2 * CHUNK, PAIR_V), BF16),
            pltpu.VMEM((rows, D_MODEL), F32),
            pltpu.VMEM((rows, D_MODEL), F32),
            pltpu.VMEM((n_chunks, N_K_TILES, CHUNK, LANES), F32),
            pltpu.VMEM((rows, K_LANES), BF16),
            pltpu.VMEM((N_GLA_PAIRS, n_chunks, LANES, 2 * CHUNK), BF16),
            pltpu.VMEM((N_HGRN_PAIRS, n_chunks, PAIR_V, 2 * CHUNK), BF16),
            pltpu.VMEM((N_GLA_PAIRS, n_chunks, 2 * CHUNK, LANES), BF16),
            pltpu.VMEM((HGRN_HEADS, n_chunks, CHUNK, LANES), BF16),
            pltpu.VMEM((n_chunks, N_K_TILES, LANES, LANES), F32),
            pltpu.VMEM((n_chunks, N_GLA_PAIRS, LANES, PAIR_V), BF16),
            pltpu.VMEM((n_chunks, N_HGRN_PAIRS, PAIR_V, PAIR_V), BF16),
            pltpu.VMEM((N_LEVELS, rows, K_LANES), BF16),
            pltpu.VMEM((N_LEVELS, N_GLA_PAIRS, n_chunks, LANES, 2 * CHUNK), BF16),
            pltpu.VMEM((N_LEVELS, N_HGRN_PAIRS, n_chunks, PAIR_V, 2 * CHUNK), BF16),
            pltpu.VMEM((rows, D_MODEL), BF16),
            pltpu.VMEM((rows, D_MODEL), F32),
            pltpu.VMEM((bb, N_GLA_PAIRS, LANES, PAIR_V), F32),
        ],
        compiler_params=pltpu.CompilerParams(
            dimension_semantics=("arbitrary", "arbitrary"),
            vmem_limit_bytes=VMEM_LIMIT_BYTES,
        ),
        name="mixer_bb%d_tt%d" % (bb, tt),
    )(x, mods, stg0, sth0, *w_in_parts, w_alpha, b_alpha, lb_logits, g_on, w_out, g_post, masks)


def kernel(x_prompt, x_sample, c_prompt, c_sample, state_gla, state_hgrn, w_ada, b_ada, g_pre, w_in,
           w_alpha, b_alpha, g_onorm_gla, hgrn_lb_logits, g_onorm_hgrn, w_out, g_post):
    assert w_ada.shape[0] == 1, "single-layer problem"
    bp = x_prompt.shape[0]

    w_in_parts, offset = [], 0
    for width in IN_SPLITS:
        part = w_in[0][:, offset:offset + width].astype(BF16)
        if width < LANES:
            part = jnp.pad(part, ((0, 0), (0, RANK_PAD - width)))
        w_in_parts.append(part)
        offset += width
    w_alpha_p = jnp.concatenate(
        [w_alpha[0], jnp.zeros((RANK_PAD - GATE_RANK, GLA_K), w_alpha.dtype)], axis=0).astype(BF16)
    g_on = jnp.concatenate([jnp.tile(g_onorm_gla[0], GLA_HEADS),
                            jnp.tile(g_onorm_hgrn[0], HGRN_HEADS)])[None, :]
    masks = jnp.asarray(_level_masks())

    c_all = jnp.concatenate([c_prompt, c_sample], axis=0)
    mods = _ada_call(c_all, w_ada[0], b_ada, g_pre).reshape(c_all.shape[0], 3, D_MODEL)

    shared = (tuple(w_in_parts), w_alpha_p, b_alpha, hgrn_lb_logits, g_on, w_out[0].astype(BF16), g_post, masks)
    stg0_p = jnp.zeros((1, GLA_HEADS, GLA_DK, GLA_DV), F32)
    sth0_p = jnp.zeros((1, HGRN_HEADS, HGRN_D, HGRN_D), F32)
    yp, gla_p, hgrn_p = _mixer_call(x_prompt, mods[:bp], stg0_p, sth0_p, *shared, bb=1, tt=ROWS_PER_STEP)
    ys, gla_s, hgrn_s = _mixer_call(x_sample, mods[bp:], state_gla[0], state_hgrn[0],
                                    *shared, bb=ROWS_PER_STEP // x_sample.shape[1], tt=x_sample.shape[1])
    return (yp, ys, gla_p[None], hgrn_p[None],
            gla_s[None].astype(state_gla.dtype), hgrn_s[None].astype(state_hgrn.dtype))
```

```python
import functools
import math

import numpy as np
import jax
import jax.numpy as jnp
from jax import lax
from jax.experimental import pallas as pl
from jax.experimental.pallas import tpu as pltpu

D_MODEL = 1024
CHUNK = 64
GLA_HEADS = 4
GLA_V = D_MODEL // 2
GLA_K = GLA_V // 2
GLA_DK = GLA_K // GLA_HEADS
GLA_DV = GLA_V // GLA_HEADS
GATE_RANK = 16
GATE_TAU = 16.0
HGRN_HEADS = 4
HGRN_W = D_MODEL - GLA_V
HGRN_D = HGRN_W // HGRN_HEADS
IN_SPLITS = (GLA_K, GLA_K, GLA_V, GLA_V, GATE_RANK, HGRN_W, HGRN_W, HGRN_W, HGRN_W)
EPS = 1e-6
LOG2E = math.log2(math.e)

LANES = 128
SUBLANES = 8
K_LANES = GLA_K + HGRN_W
N_K_TILES = K_LANES // LANES
N_GLA_PAIRS = GLA_HEADS // 2
N_HGRN_PAIRS = HGRN_HEADS // 2
N_PAIRS = N_GLA_PAIRS + N_HGRN_PAIRS
PAIR_V = 2 * LANES
N_COL_BLOCKS = D_MODEL // PAIR_V
RANK_PAD = LANES
HALF_SIZES = (32, 16, 8, 4, 2, 1)
N_LEVELS = len(HALF_SIZES) + 1
MASK_CAUSAL = N_LEVELS
SINGLE_ANCHOR_MAX_LOG2 = 90.0
ROWS_PER_STEP = 256
VMEM_LIMIT_BYTES = 56 * 1024 * 1024

F32 = jnp.float32
BF16 = jnp.bfloat16
TN = (((0,), (0,)), ((), ()))


def _level_masks():
    t = np.arange(CHUNK)[:, None]
    s = np.arange(CHUNK)[None, :]
    masks = [(t // (2 * m)) == (s // (2 * m)) for m in HALF_SIZES]
    masks.append(t == s)
    masks.append(t >= s)
    masks = np.stack(masks).astype(np.float32)
    return np.concatenate([masks, masks], axis=2)


def _rms(x):
    return x * lax.rsqrt(jnp.mean(x * x, axis=-1, keepdims=True) + EPS)


def _sigmoid(x):
    return 1.0 / (1.0 + jnp.exp(-x))


def _log2_sigmoid(x):
    return jnp.minimum(x, 0.0) * LOG2E - jnp.log2(1.0 + jnp.exp(-jnp.abs(x)))


def _ada_kernel(c_ref, w_ref, b_ref, gpre_ref, out_ref):
    n = pl.program_id(0)
    val = jnp.dot(c_ref[...].astype(BF16), w_ref[...].astype(BF16),
                  preferred_element_type=F32) + b_ref[...]
    out_ref[...] = jnp.where(n == 1, gpre_ref[...] * (1.0 + val), val)


def _ada_call(c_all, w_ada, b_ada, g_pre):
    nb = c_all.shape[0]
    return pl.pallas_call(
        _ada_kernel,
        grid=(3,),
        in_specs=[
            pl.BlockSpec((nb, D_MODEL), lambda n: (0, 0)),
            pl.BlockSpec((D_MODEL, D_MODEL), lambda n: (0, n)),
            pl.BlockSpec((1, D_MODEL), lambda n: (0, n)),
            pl.BlockSpec((1, D_MODEL), lambda n: (0, 0)),
        ],
        out_specs=pl.BlockSpec((nb, D_MODEL), lambda n: (0, n)),
        out_shape=jax.ShapeDtypeStruct((nb, 3 * D_MODEL), F32),
        name="ada_mod",
    )(c_all, w_ada, b_ada, g_pre)


def _ref_rows(b_ref, block, off):
    pieces = []
    if block >= SUBLANES:
        for j in range(CHUNK // block):
            row = b_ref[j * block + off:j * block + off + 1, :]
            pieces.append(jnp.broadcast_to(row, (block, LANES)))
    else:
        assert block == SUBLANES // 2
        sub = lax.broadcasted_iota(jnp.int32, (SUBLANES, LANES), 0)
        for i in range(CHUNK // SUBLANES):
            lo = b_ref[SUBLANES * i + off:SUBLANES * i + off + 1, :]
            hi = b_ref[SUBLANES * i + block + off:SUBLANES * i + block + off + 1, :]
            pieces.append(jnp.where(sub < block,
                                    jnp.broadcast_to(lo, (SUBLANES, LANES)),
                                    jnp.broadcast_to(hi, (SUBLANES, LANES))))
    return jnp.concatenate(pieces, axis=0)


def _mixer_kernel(x_ref, mods_ref, stg0_ref, sth0_ref,
                  wqa_ref, wka_ref, wva_ref, wza_ref, wal_ref, wqh_ref, wfh_ref, wih_ref, wzh_ref,
                  walpha_ref, balpha_ref, lbl_ref,
                  gon_ref, wout_ref, gpost_ref, masks_ref,
                  y_ref, stg_ref, sth_ref,
                  hb_s, q_s, k_s, g_s, vbd_s, zs_s, o_s, b_s, qst_s, kag_s, kah_s, ksg_s, ksh_s, dcol_s, sbg_s, sbh_s,
                  lql_s, klg_s, klh_s, gb_s, po_s, stg_s,
                  *, bb, tt):
    rows = bb * tt
    n_chunks = rows // CHUNK
    chunks_per_seq = tt // CHUNK

    @pl.when((pl.program_id(0) == 0) & (pl.program_id(1) == 0))
    def _():
        for ref in (vbd_s, kah_s, klh_s, sbh_s):
            ref[...] = jnp.zeros(ref.shape, BF16)

    @pl.when(pl.program_id(1) == 0)
    def _():
        zero = jnp.zeros((GLA_DK, GLA_DV), F32)
        for bi in range(bb):
            for p in range(N_GLA_PAIRS):
                first, second = stg0_ref[bi, 2 * p], stg0_ref[bi, 2 * p + 1]
                stg_s[bi, p] = jnp.concatenate([jnp.concatenate([first, zero], axis=1),
                                                jnp.concatenate([zero, second], axis=1)], axis=0)
        sth_ref[...] = sth0_ref[...]

    def mod_rows(i):
        if bb == 1:
            return mods_ref[0, i:i + 1, :]
        return jnp.concatenate(
            [jnp.broadcast_to(mods_ref[bi, i:i + 1, :], (tt, D_MODEL)) for bi in range(bb)], axis=0)

    half = rows // 2
    x = x_ref[...].reshape(rows, D_MODEL)
    hb_s[...] = (_rms(x) * mod_rows(1) + mod_rows(0)).astype(BF16)

    def proj(w_ref, n=0):
        width = min(PAIR_V, w_ref.shape[1])
        return jnp.dot(hb_s[...], w_ref[:, n * width:(n + 1) * width], preferred_element_type=F32)

    def head_group_block(gla_ref, hgrn_ref, n):
        n_gla = GLA_V // PAIR_V
        return proj(gla_ref, n) if n < n_gla else proj(hgrn_ref, n - n_gla)

    a_lr = proj(wal_ref)
    ga = jnp.dot(a_lr.astype(BF16), walpha_ref[...], preferred_element_type=F32) + balpha_ref[...]
    g_s[:, :GLA_K] = _log2_sigmoid(ga) * (1.0 / GATE_TAU)

    k_s[:, :GLA_K] = proj(wka_ref)
    lbl = lbl_ref[...]
    lmax = jnp.max(lbl, axis=0, keepdims=True)
    lexp = jnp.exp(lbl - lmax)
    lb = lexp[0:1, :] / jnp.sum(lexp, axis=0, keepdims=True)
    for n in range(HGRN_W // PAIR_V):
        cols = slice(GLA_K + n * PAIR_V, GLA_K + (n + 1) * PAIR_V)
        lb_n = lb[:, n * PAIR_V:(n + 1) * PAIR_V]
        f = lb_n + (1.0 - lb_n) * _sigmoid(proj(wfh_ref, n))
        k_s[:, cols] = 1.0 - f
        g_s[:, cols] = jnp.log2(f)

    q_s[:, :GLA_K] = proj(wqa_ref) * (GLA_DK ** -0.5)
    for n in range(HGRN_W // PAIR_V):
        cols = slice(GLA_K + n * PAIR_V, GLA_K + (n + 1) * PAIR_V)
        qh = proj(wqh_ref, n)
        q_s[:, cols] = qh * _sigmoid(qh)

    def project_values():
        for p in range(N_PAIRS):
            pv = head_group_block(wva_ref, wih_ref, p).astype(BF16)
            for ci in range(n_chunks):
                r = slice(ci * CHUNK, (ci + 1) * CHUNK)
                vbd_s[p, ci, :CHUNK, :LANES] = pv[r, :LANES]
                vbd_s[p, ci, CHUNK:, LANES:] = pv[r, LANES:]

    def project_gates(n):
        cols = slice(n * PAIR_V, (n + 1) * PAIR_V)
        pz = head_group_block(wza_ref, wzh_ref, n)
        zs_s[:, cols] = pz * _sigmoid(pz)

    row_i = lax.broadcasted_iota(jnp.int32, (CHUNK, LANES), 0)
    lane_i = lax.broadcasted_iota(jnp.int32, (CHUNK, LANES), 1)
    head0 = lane_i < GLA_DK

    def tile_base(ci, lt):
        r = slice(ci * CHUNK, (ci + 1) * CHUNK)
        lanes = slice(lt * LANES, (lt + 1) * LANES)
        b_t = b_s.at[ci, lt]
        g = g_s[r, lanes]
        q = q_s[r, lanes]
        k = k_s[r, lanes]
        b = g
        for sh in (1, 2, 4):
            b = b + jnp.where(row_i >= sh, pltpu.roll(b, sh, 0), 0.0)
        for sh in (8, 16, 32):
            b = b + jnp.concatenate([jnp.zeros((sh, LANES), F32), b[:CHUNK - sh]], axis=0)
        b_t[...] = b
        b_last = b_t[CHUNK - 1:CHUNK, :]
        dcol_s[ci, lt] = jnp.exp2(jnp.broadcast_to(b_last, (LANES, LANES)).T)
        qst_s[r, lanes] = (q * jnp.exp2(b)).astype(BF16)
        k_state = k * jnp.exp2(b_last - b)
        if lt < N_GLA_PAIRS:
            ksg_s[lt, ci, :CHUNK, :] = jnp.where(head0, k_state, 0.0).astype(BF16)
            ksg_s[lt, ci, CHUNK:, :] = jnp.where(head0, 0.0, k_state).astype(BF16)
        else:
            ksh_s[lt - N_GLA_PAIRS, ci] = k_state.astype(BF16)
        return k * jnp.exp2(-b)

    def tile_reload(ci, lt):
        r = slice(ci * CHUNK, (ci + 1) * CHUNK)
        lanes = slice(lt * LANES, (lt + 1) * LANES)
        b_t = b_s.at[ci, lt]
        return q_s[r, lanes], k_s[r, lanes], g_s[r, lanes], b_t[...], b_t

    def tile_level(base, lv):
        q, k, g, b, b_t = base
        if lv == N_LEVELS - 1:
            return q, k
        m = HALF_SIZES[lv]
        upper = (row_i & (2 * m - 1)) >= m
        if m == 1:
            lhs, rhs = q * jnp.exp2(g), k
        else:
            e = jnp.exp2(-jnp.abs(b - _ref_rows(b_t, 2 * m, m - 1)))
            lhs, rhs = q * e, k * e
        return jnp.where(upper, lhs, 0.0), jnp.where(upper, 0.0, rhs)

    def store_keys_t(dst, p, rhs_tiles):
        if p < N_GLA_PAIRS:
            rhs = rhs_tiles[0]
            stacked = jnp.concatenate([jnp.where(head0, rhs, 0.0), jnp.where(head0, 0.0, rhs)], axis=0)
            dst[...] = stacked.astype(BF16).T
        else:
            kt = jnp.concatenate(rhs_tiles, axis=0).astype(BF16).T
            dst[:LANES, :CHUNK] = kt[:, :CHUNK]
            dst[LANES:, CHUNK:] = kt[:, CHUNK:]

    def tiles_of_pair(p):
        if p < N_GLA_PAIRS:
            return (p,)
        return tuple(N_GLA_PAIRS + 2 * (p - N_GLA_PAIRS) + pos for pos in range(2))

    def pair_lanes(p):
        if p < N_GLA_PAIRS:
            return slice(p * LANES, (p + 1) * LANES)
        return slice(GLA_K + (p - N_GLA_PAIRS) * PAIR_V, GLA_K + (p - N_GLA_PAIRS + 1) * PAIR_V)

    def anchor_keys(ci, p):
        return kag_s.at[p, ci] if p < N_GLA_PAIRS else kah_s.at[p - N_GLA_PAIRS, ci]

    def level_keys(lv, p, ci):
        return klg_s.at[lv, p, ci] if p < N_GLA_PAIRS else klh_s.at[lv, p - N_GLA_PAIRS, ci]

    def prep_pair(ci, p):
        store_keys_t(anchor_keys(ci, p), p, [tile_base(ci, lt) for lt in tiles_of_pair(p)])

    def prep_pair_levels(ci, p):
        tiles = tiles_of_pair(p)
        bases = [tile_reload(ci, lt) for lt in tiles]
        for lv in range(N_LEVELS):
            rhs_tiles = []
            for lt, base in zip(tiles, bases):
                lhs, rhs = tile_level(base, lv)
                lql_s[lv, ci * CHUNK:(ci + 1) * CHUNK, lt * LANES:(lt + 1) * LANES] = lhs.astype(BF16)
                rhs_tiles.append(rhs)
            store_keys_t(level_keys(lv, p, ci), p, rhs_tiles)

    def pair_attention(ci, p, single_anchor):
        lanes = pair_lanes(p)
        if single_anchor:
            prod = jnp.dot(qst_s[ci * CHUNK:(ci + 1) * CHUNK, lanes], anchor_keys(ci, p)[...],
                           preferred_element_type=F32)
            return jnp.where(masks_ref[MASK_CAUSAL] > 0.0, prod, 0.0).astype(BF16)
        att = None
        for lv in range(N_LEVELS):
            prod = jnp.dot(lql_s[lv, ci * CHUNK:(ci + 1) * CHUNK, lanes], level_keys(lv, p, ci)[...],
                           preferred_element_type=F32)
            if lv > 0:
                prod = prod * masks_ref[lv]
            att = prod if att is None else att + prod
        return att.astype(BF16)

    def snapshot_states(ci):
        bi = ci // chunks_per_seq
        for p in range(N_GLA_PAIRS):
            sbg_s[ci, p] = stg_s[bi, p].astype(BF16)
        for hh in range(HGRN_HEADS):
            pos = hh % 2
            sbh_s[ci, hh // 2, pos * LANES:(pos + 1) * LANES, pos * LANES:(pos + 1) * LANES] = (
                sth_ref[bi, hh].astype(BF16))

    def update_states(ci):
        bi = ci // chunks_per_seq
        for p in range(N_GLA_PAIRS):
            dcol = dcol_s[ci, p]
            stg_s[bi, p] = (stg_s[bi, p] * jnp.concatenate([dcol, dcol], axis=1)
                              + lax.dot_general(ksg_s[p, ci], vbd_s[p, ci], TN, preferred_element_type=F32))
        for hh in range(HGRN_HEADS):
            pg, pos = N_GLA_PAIRS + hh // 2, hh % 2
            v_head = vbd_s[pg, ci, pos * CHUNK:(pos + 1) * CHUNK, pos * LANES:(pos + 1) * LANES]
            sth_ref[bi, hh] = sth_ref[bi, hh] * dcol_s[ci, N_GLA_PAIRS + hh] + lax.dot_general(
                ksh_s[hh, ci], v_head, TN, preferred_element_type=F32)

    def pair_output(ci, p, att):
        r = slice(ci * CHUNK, (ci + 1) * CHUNK)
        state = sbg_s[ci, p] if p < N_GLA_PAIRS else sbh_s[ci, p - N_GLA_PAIRS]
        lhs = jnp.concatenate([att, qst_s[r, pair_lanes(p)]], axis=1)
        rhs = jnp.concatenate([vbd_s[p, ci], state], axis=0)
        o_s[r, p * PAIR_V:(p + 1) * PAIR_V] = jnp.dot(lhs, rhs, preferred_element_type=F32)

    gate_rows = mod_rows(2)

    def gate_half(h):
        rr = slice(h * half, (h + 1) * half)
        for ht in range(GLA_HEADS + HGRN_HEADS):
            lanes = slice(ht * LANES, (ht + 1) * LANES)
            gb_s[rr, lanes] = (_rms(o_s[rr, lanes]) * gon_ref[:, lanes] * zs_s[rr, lanes]).astype(BF16)

    def project_half(h, n):
        rr = slice(h * half, (h + 1) * half)
        cols = slice(n * PAIR_V, (n + 1) * PAIR_V)
        po_s[rr, cols] = jnp.dot(gb_s[rr, :], wout_ref[:, cols], preferred_element_type=F32)

    def post_half(h):
        rr = slice(h * half, (h + 1) * half)
        gate_h = gate_rows if bb == 1 else gate_rows[rr]
        res = gate_h * (_rms(po_s[rr, :]) * gpost_ref[...])
        if bb == 1:
            y_ref[0, rr, :] = x_ref[0, rr, :] + res
        else:
            seqs = slice(h * (bb // 2), (h + 1) * (bb // 2))
            y_ref[seqs] = x_ref[seqs] + res.reshape(bb // 2, tt, D_MODEL)

    def finish_stages(h):
        return ([functools.partial(project_half, h, n) for n in range(N_COL_BLOCKS)]
                + [functools.partial(post_half, h)])

    fillers = [functools.partial(project_gates, n) for n in range(N_COL_BLOCKS)]
    atts = {}
    for step in range(n_chunks + 2):
        if step < n_chunks:
            for p in range(N_PAIRS):
                prep_pair(step, p)
        if step == 0:
            project_values()
        if 1 <= step <= n_chunks:
            snapshot_states(step - 1)
            if step < n_chunks:
                update_states(step - 1)
            atts[step - 1] = [pair_attention(step - 1, p, True) for p in range(N_PAIRS)]
        if fillers:
            fillers.pop(0)()
        if step >= 2:
            for p in range(N_PAIRS):
                pair_output(step - 2, p, atts[step - 2][p])
            if (step - 1) * CHUNK == half:
                while fillers:
                    fillers.pop(0)()
                gate_half(0)
                fillers = finish_stages(0)
    last_stages = finish_stages(1)
    for stage in ([functools.partial(gate_half, 1)] + fillers + last_stages[:-1]
                  + [functools.partial(update_states, n_chunks - 1)] + last_stages[-1:]):
        stage()

    total = None
    for ci in range(n_chunks):
        t = jnp.sum(g_s[ci * CHUNK:(ci + 1) * CHUNK, :], axis=0, keepdims=True)
        total = t if total is None else jnp.minimum(total, t)
    mild = jnp.min(total) >= -SINGLE_ANCHOR_MAX_LOG2

    @pl.when(jnp.logical_not(mild))
    def _():
        for ci in range(n_chunks):
            for p in range(N_PAIRS):
                prep_pair_levels(ci, p)
            for p in range(N_PAIRS):
                pair_output(ci, p, pair_attention(ci, p, False))
        for h in range(2):
            for stage in [functools.partial(gate_half, h)] + finish_stages(h):
                stage()

    @pl.when(pl.program_id(1) == pl.num_programs(1) - 1)
    def _():
        for bi in range(bb):
            for p in range(N_GLA_PAIRS):
                for h in range(2):
                    stg_ref[bi, 2 * p + h] = stg_s[bi, p, h * GLA_DK:(h + 1) * GLA_DK,
                                                   h * GLA_DV:(h + 1) * GLA_DV]


def _mixer_call(x, mods, stg0, sth0, w_in_parts, w_alpha, b_alpha, lb_logits, g_on, w_out, g_post, masks,
                *, bb, tt):
    nb, seq, _ = x.shape
    rows = bb * tt
    assert nb % bb == 0 and seq % tt == 0 and tt % CHUNK == 0
    n_chunks = rows // CHUNK
    per_seq4 = lambda b, j: (b, 0, 0, 0)

    def resident(shape):
        return pl.BlockSpec(shape, lambda b, j: (0,) * len(shape), pipeline_mode=pl.Buffered(1))

    def initial_state(st0):
        assert st0.shape[0] in (bb, nb)
        shared = st0.shape[0] == bb and nb != bb
        return pl.BlockSpec((bb,) + st0.shape[1:], (lambda b, j: (0, 0, 0, 0)) if shared else per_seq4)

    kernel = functools.partial(_mixer_kernel, bb=bb, tt=tt)
    stg_shape =(nb, GLA_HEADS, GLA_DK, GLA_DV)
    sth_shape = (nb, HGRN_HEADS, HGRN_D, HGRN_D)
    return pl.pallas_call(
        kernel,
        grid=(nb // bb, seq // tt),
        in_specs=[
            pl.BlockSpec((bb, tt, D_MODEL), lambda b, j: (b, j, 0)),
            pl.BlockSpec((bb, 3, D_MODEL), lambda b, j: (b, 0, 0)),
            initial_state(stg0),
            initial_state(sth0),
            *[resident(part.shape) for part in w_in_parts],
            resident((RANK_PAD, GLA_K)),
            resident((1, GLA_K)),
            resident(lb_logits.shape),
            resident((1, D_MODEL)),
            resident((D_MODEL, D_MODEL)),
            resident((1, D_MODEL)),
            resident(masks.shape),
        ],
        out_specs=[
            pl.BlockSpec((bb, tt, D_MODEL), lambda b, j: (b, j, 0)),
            pl.BlockSpec((bb,) + stg_shape[1:], per_seq4),
            pl.BlockSpec((bb,) + sth_shape[1:], per_seq4),
        ],
        out_shape=[
            jax.ShapeDtypeStruct(x.shape, F32),
            jax.ShapeDtypeStruct(stg_shape, F32),
            jax.ShapeDtypeStruct(sth_shape, F32),
        ],
        scratch_shapes=[
            pltpu.VMEM((rows, D_MODEL), BF16),
            pltpu.VMEM((rows, K_LANES), F32),
            pltpu.VMEM((rows, K_LANES), F32),
            pltpu.VMEM((rows, K_LANES), F32),
            pltpu.VMEM((N_PAIRS, n_chunks, 2 * CHUNK, PAIR_V), BF16),
            pltpu.VMEM((rows, D_MODEL), F32),
            pltpu.VMEM((rows, D_MODEL), F32),
            pltpu.VMEM((n_chunks, N_K_TILES, CHUNK, LANES), F32),
            pltpu.VMEM((rows, K_LANES), BF16),
            pltpu.VMEM((N_GLA_PAIRS, n_chunks, LANES, 2 * CHUNK), BF16),
            pltpu.VMEM((N_HGRN_PAIRS, n_chunks, PAIR_V, 2 * CHUNK), BF16),
            pltpu.VMEM((N_GLA_PAIRS, n_chunks, 2 * CHUNK, LANES), BF16),
            pltpu.VMEM((HGRN_HEADS, n_chunks, CHUNK, LANES), BF16),
            pltpu.VMEM((n_chunks, N_K_TILES, LANES, LANES), F32),
            pltpu.VMEM((n_chunks, N_GLA_PAIRS, LANES, PAIR_V), BF16),
            pltpu.VMEM((n_chunks, N_HGRN_PAIRS, PAIR_V, PAIR_V), BF16),
            pltpu.VMEM((N_LEVELS, rows, K_LANES), BF16),
            pltpu.VMEM((N_LEVELS, N_GLA_PAIRS, n_chunks, LANES, 2 * CHUNK), BF16),
            pltpu.VMEM((N_LEVELS, N_HGRN_PAIRS, n_chunks, PAIR_V, 2 * CHUNK), BF16),
            pltpu.VMEM((rows, D_MODEL), BF16),
            pltpu.VMEM((rows, D_MODEL), F32),
            pltpu.VMEM((bb, N_GLA_PAIRS, LANES, PAIR_V), F32),
        ],
        compiler_params=pltpu.CompilerParams(
            dimension_semantics=("arbitrary", "arbitrary"),
            vmem_limit_bytes=VMEM_LIMIT_BYTES,
            allow_input_fusion=[False] * 4 + [True] * len(w_in_parts) + [False] * 7,
        ),
        name="mixer_bb%d_tt%d" % (bb, tt),
    )(x, mods, stg0, sth0, *w_in_parts, w_alpha, b_alpha, lb_logits, g_on, w_out, g_post, masks)


def kernel(x_prompt, x_sample, c_prompt, c_sample, state_gla, state_hgrn, w_ada, b_ada, g_pre, w_in,
           w_alpha, b_alpha, g_onorm_gla, hgrn_lb_logits, g_onorm_hgrn, w_out, g_post):
    assert w_ada.shape[0] == 1, "single-layer problem"
    bp = x_prompt.shape[0]

    w_in_parts, offset = [], 0
    for width in IN_SPLITS:
        part = w_in[0][:, offset:offset + width].astype(BF16)
        if width < LANES:
            part = jnp.pad(part, ((0, 0), (0, RANK_PAD - width)))
        w_in_parts.append(part)
        offset += width
    w_alpha_p = jnp.concatenate(
        [w_alpha[0], jnp.zeros((RANK_PAD - GATE_RANK, GLA_K), w_alpha.dtype)], axis=0).astype(BF16)
    g_on = jnp.concatenate([jnp.tile(g_onorm_gla[0], GLA_HEADS),
                            jnp.tile(g_onorm_hgrn[0], HGRN_HEADS)])[None, :]
    masks = jnp.asarray(_level_masks())

    c_all = jnp.concatenate([c_prompt, c_sample], axis=0)
    mods = _ada_call(c_all, w_ada[0], b_ada, g_pre).reshape(c_all.shape[0], 3, D_MODEL)

    shared = (tuple(w_in_parts), w_alpha_p, b_alpha, hgrn_lb_logits, g_on, w_out[0].astype(BF16), g_post, masks)
    stg0_p = jnp.zeros((1, GLA_HEADS, GLA_DK, GLA_DV), F32)
    sth0_p = jnp.zeros((1, HGRN_HEADS, HGRN_D, HGRN_D), F32)
    yp, gla_p, hgrn_p = _mixer_call(x_prompt, mods[:bp], stg0_p, sth0_p, *shared, bb=1, tt=ROWS_PER_STEP)
    ys, gla_s, hgrn_s = _mixer_call(x_sample, mods[bp:], state_gla[0], state_hgrn[0],
                                    *shared, bb=ROWS_PER_STEP // x_sample.shape[1], tt=x_sample.shape[1])
    return (yp, ys, gla_p[None], hgrn_p[None],
            gla_s[None].astype(state_gla.dtype), hgrn_s[None].astype(state_hgrn.dtype))
```

```python
import functools
import math

import numpy as np
import jax
import jax.numpy as jnp
from jax import lax
from jax.experimental import pallas as pl
from jax.experimental.pallas import tpu as pltpu

D_MODEL = 1024
CHUNK = 64
GLA_HEADS = 4
GLA_V = D_MODEL // 2
GLA_K = GLA_V // 2
GLA_DK = GLA_K // GLA_HEADS
GLA_DV = GLA_V // GLA_HEADS
GATE_RANK = 16
GATE_TAU = 16.0
HGRN_HEADS = 4
HGRN_W = D_MODEL - GLA_V
HGRN_D = HGRN_W // HGRN_HEADS
IN_SPLITS = (GLA_K, GLA_K, GLA_V, GLA_V, GATE_RANK, HGRN_W, HGRN_W, HGRN_W, HGRN_W)
EPS = 1e-6
LOG2E = math.log2(math.e)

LANES = 128
SUBLANES = 8
K_LANES = GLA_K + HGRN_W
N_K_TILES = K_LANES // LANES
N_GLA_PAIRS = GLA_HEADS // 2
N_HGRN_PAIRS = HGRN_HEADS // 2
N_PAIRS = N_GLA_PAIRS + N_HGRN_PAIRS
PAIR_V = 2 * LANES
N_COL_BLOCKS = D_MODEL // PAIR_V
RANK_PAD = LANES
HALF_SIZES = (32, 16, 8, 4, 2, 1)
N_LEVELS = len(HALF_SIZES) + 1
MASK_CAUSAL = N_LEVELS
SINGLE_ANCHOR_MAX_LOG2 = 90.0
ROWS_PER_STEP = 256
VMEM_LIMIT_BYTES = 56 * 1024 * 1024

F32 = jnp.float32
BF16 = jnp.bfloat16
TN = (((0,), (0,)), ((), ()))


def _level_masks():
    t = np.arange(CHUNK)[:, None]
    s = np.arange(CHUNK)[None, :]
    masks = [(t // (2 * m)) == (s // (2 * m)) for m in HALF_SIZES]
    masks.append(t == s)
    masks.append(t >= s)
    masks = np.stack(masks).astype(np.float32)
    return np.concatenate([masks, masks], axis=2)


def _rms(x):
    return x * lax.rsqrt(jnp.mean(x * x, axis=-1, keepdims=True) + EPS)


def _sigmoid(x):
    return 1.0 / (1.0 + jnp.exp(-x))


def _log2_sigmoid(x):
    return jnp.minimum(x, 0.0) * LOG2E - jnp.log2(1.0 + jnp.exp(-jnp.abs(x)))


def _ada_kernel(c_ref, w_ref, b_ref, gpre_ref, out_ref):
    n = pl.program_id(0)
    val = jnp.dot(c_ref[...].astype(BF16), w_ref[...].astype(BF16),
                  preferred_element_type=F32) + b_ref[...]
    out_ref[...] = jnp.where(n == 1, gpre_ref[...] * (1.0 + val), val)


def _ada_call(c_all, w_ada, b_ada, g_pre):
    nb = c_all.shape[0]
    return pl.pallas_call(
        _ada_kernel,
        grid=(3,),
        in_specs=[
            pl.BlockSpec((nb, D_MODEL), lambda n: (0, 0)),
            pl.BlockSpec((D_MODEL, D_MODEL), lambda n: (0, n)),
            pl.BlockSpec((1, D_MODEL), lambda n: (0, n)),
            pl.BlockSpec((1, D_MODEL), lambda n: (0, 0)),
        ],
        out_specs=pl.BlockSpec((nb, D_MODEL), lambda n: (0, n)),
        out_shape=jax.ShapeDtypeStruct((nb, 3 * D_MODEL), F32),
        name="ada_mod",
    )(c_all, w_ada, b_ada, g_pre)


def _ref_rows(b_ref, block, off):
    pieces = []
    if block >= SUBLANES:
        for j in range(CHUNK // block):
            row = b_ref[j * block + off:j * block + off + 1, :]
            pieces.append(jnp.broadcast_to(row, (block, LANES)))
    else:
        assert block == SUBLANES // 2
        sub = lax.broadcasted_iota(jnp.int32, (SUBLANES, LANES), 0)
        for i in range(CHUNK // SUBLANES):
            lo = b_ref[SUBLANES * i + off:SUBLANES * i + off + 1, :]
            hi = b_ref[SUBLANES * i + block + off:SUBLANES * i + block + off + 1, :]
            pieces.append(jnp.where(sub < block,
                                    jnp.broadcast_to(lo, (SUBLANES, LANES)),
                                    jnp.broadcast_to(hi, (SUBLANES, LANES))))
    return jnp.concatenate(pieces, axis=0)


def _mixer_kernel(x_ref, mods_ref, stg0_ref, sth0_ref,
                  wqa_ref, wka_ref, wva_ref, wza_ref, wal_ref, wqh_ref, wfh_ref, wih_ref, wzh_ref,
                  walpha_ref, balpha_ref, lbl_ref,
                  gon_ref, wout_ref, gpost_ref, masks_ref,
                  y_ref, stg_ref, sth_ref,
                  hb_s, q_s, k_s, g_s, vbd_s, zs_s, o_s, b_s, qst_s, kag_s, kah_s, ksg_s, ksh_s, dcol_s, sbg_s, sbh_s,
                  lql_s, klg_s, klh_s, gb_s, po_s, stg_s,
                  *, bb, tt):
    rows = bb * tt
    n_chunks = rows // CHUNK
    chunks_per_seq = tt // CHUNK

    @pl.when((pl.program_id(0) == 0) & (pl.program_id(1) == 0))
    def _():
        for ref in (vbd_s, kah_s, klh_s, sbh_s):
            ref[...] = jnp.zeros(ref.shape, BF16)

    @pl.when(pl.program_id(1) == 0)
    def _():
        zero = jnp.zeros((GLA_DK, GLA_DV), F32)
        for bi in range(bb):
            for p in range(N_GLA_PAIRS):
                first, second = stg0_ref[bi, 2 * p], stg0_ref[bi, 2 * p + 1]
                stg_s[bi, p] = jnp.concatenate([jnp.concatenate([first, zero], axis=1),
                                                jnp.concatenate([zero, second], axis=1)], axis=0)
        sth_ref[...] = sth0_ref[...]

    def mod_rows(i):
        if bb == 1:
            return mods_ref[0, i:i + 1, :]
        return jnp.concatenate(
            [jnp.broadcast_to(mods_ref[bi, i:i + 1, :], (tt, D_MODEL)) for bi in range(bb)], axis=0)

    half = rows // 2
    x = x_ref[...].reshape(rows, D_MODEL)
    hb_s[...] = (_rms(x) * mod_rows(1) + mod_rows(0)).astype(BF16)

    def proj(w_ref, n=0):
        width = min(PAIR_V, w_ref.shape[1])
        return jnp.dot(hb_s[...], w_ref[:, n * width:(n + 1) * width], preferred_element_type=F32)

    def head_group_block(gla_ref, hgrn_ref, n):
        n_gla = GLA_V // PAIR_V
        return proj(gla_ref, n) if n < n_gla else proj(hgrn_ref, n - n_gla)

    a_lr = proj(wal_ref)
    ga = jnp.dot(a_lr.astype(BF16), walpha_ref[...], preferred_element_type=F32) + balpha_ref[...]
    g_s[:, :GLA_K] = _log2_sigmoid(ga) * (1.0 / GATE_TAU)

    k_s[:, :GLA_K] = proj(wka_ref)
    lbl = lbl_ref[...]
    lmax = jnp.max(lbl, axis=0, keepdims=True)
    lexp = jnp.exp(lbl - lmax)
    lb = lexp[0:1, :] / jnp.sum(lexp, axis=0, keepdims=True)
    for n in range(HGRN_W // PAIR_V):
        cols = slice(GLA_K + n * PAIR_V, GLA_K + (n + 1) * PAIR_V)
        lb_n = lb[:, n * PAIR_V:(n + 1) * PAIR_V]
        f = lb_n + (1.0 - lb_n) * _sigmoid(proj(wfh_ref, n))
        k_s[:, cols] = 1.0 - f
        g_s[:, cols] = jnp.log2(f)

    q_s[:, :GLA_K] = proj(wqa_ref) * (GLA_DK ** -0.5)
    for n in range(HGRN_W // PAIR_V):
        cols = slice(GLA_K + n * PAIR_V, GLA_K + (n + 1) * PAIR_V)
        qh = proj(wqh_ref, n)
        q_s[:, cols] = qh * _sigmoid(qh)

    def project_values():
        for p in range(N_PAIRS):
            pv = head_group_block(wva_ref, wih_ref, p).astype(BF16)
            for ci in range(n_chunks):
                r = slice(ci * CHUNK, (ci + 1) * CHUNK)
                vbd_s[p, ci, :CHUNK, :LANES] = pv[r, :LANES]
                vbd_s[p, ci, CHUNK:, LANES:] = pv[r, LANES:]

    def project_gates(n):
        cols = slice(n * PAIR_V, (n + 1) * PAIR_V)
        pz = head_group_block(wza_ref, wzh_ref, n)
        zs_s[:, cols] = pz * _sigmoid(pz)

    row_i = lax.broadcasted_iota(jnp.int32, (CHUNK, LANES), 0)
    lane_i = lax.broadcasted_iota(jnp.int32, (CHUNK, LANES), 1)
    head0 = lane_i < GLA_DK

    def tile_base(ci, lt):
        r = slice(ci * CHUNK, (ci + 1) * CHUNK)
        lanes = slice(lt * LANES, (lt + 1) * LANES)
        b_t = b_s.at[ci, lt]
        g = g_s[r, lanes]
        q = q_s[r, lanes]
        k = k_s[r, lanes]
        b = g
        for sh in (1, 2, 4):
            b = b + jnp.where(row_i >= sh, pltpu.roll(b, sh, 0), 0.0)
        for sh in (8, 16, 32):
            b = b + jnp.concatenate([jnp.zeros((sh, LANES), F32), b[:CHUNK - sh]], axis=0)
        b_t[...] = b
        b_last = b_t[CHUNK - 1:CHUNK, :]
        dcol_s[ci, lt] = jnp.exp2(jnp.broadcast_to(b_last, (LANES, LANES)).T)
        qst_s[r, lanes] = (q * jnp.exp2(b)).astype(BF16)
        k_state = k * jnp.exp2(b_last - b)
        if lt < N_GLA_PAIRS:
            ksg_s[lt, ci, :CHUNK, :] = jnp.where(head0, k_state, 0.0).astype(BF16)
            ksg_s[lt, ci, CHUNK:, :] = jnp.where(head0, 0.0, k_state).astype(BF16)
        else:
            ksh_s[lt - N_GLA_PAIRS, ci] = k_state.astype(BF16)
        return k * jnp.exp2(-b)

    def chunk_rows(ci):
        if isinstance(ci, int):
            return slice(ci * CHUNK, (ci + 1) * CHUNK)
        return pl.ds(pl.multiple_of(ci * CHUNK, CHUNK), CHUNK)

    def tile_reload(ci, lt):
        r = chunk_rows(ci)
        lanes = slice(lt * LANES, (lt + 1) * LANES)
        b_t = b_s.at[ci, lt]
        return q_s[r, lanes], k_s[r, lanes], g_s[r, lanes], b_t[...], b_t

    def tile_level(base, lv):
        q, k, g, b, b_t = base
        if lv == N_LEVELS - 1:
            return q, k
        m = HALF_SIZES[lv]
        upper = (row_i & (2 * m - 1)) >= m
        if m == 1:
            lhs, rhs = q * jnp.exp2(g), k
        else:
            e = jnp.exp2(-jnp.abs(b - _ref_rows(b_t, 2 * m, m - 1)))
            lhs, rhs = q * e, k * e
        return jnp.where(upper, lhs, 0.0), jnp.where(upper, 0.0, rhs)

    def store_keys_t(dst, p, rhs_tiles):
        if p < N_GLA_PAIRS:
            rhs = rhs_tiles[0]
            stacked = jnp.concatenate([jnp.where(head0, rhs, 0.0), jnp.where(head0, 0.0, rhs)], axis=0)
            dst[...] = stacked.astype(BF16).T
        else:
            kt = jnp.concatenate(rhs_tiles, axis=0).astype(BF16).T
            dst[:LANES, :CHUNK] = kt[:, :CHUNK]
            dst[LANES:, CHUNK:] = kt[:, CHUNK:]

    def tiles_of_pair(p):
        if p < N_GLA_PAIRS:
            return (p,)
        return tuple(N_GLA_PAIRS + 2 * (p - N_GLA_PAIRS) + pos for pos in range(2))

    def pair_lanes(p):
        if p < N_GLA_PAIRS:
            return slice(p * LANES, (p + 1) * LANES)
        return slice(GLA_K + (p - N_GLA_PAIRS) * PAIR_V, GLA_K + (p - N_GLA_PAIRS + 1) * PAIR_V)

    def anchor_keys(ci, p):
        return kag_s.at[p, ci] if p < N_GLA_PAIRS else kah_s.at[p - N_GLA_PAIRS, ci]

    def level_keys(lv, p, ci):
        return klg_s.at[lv, p, ci] if p < N_GLA_PAIRS else klh_s.at[lv, p - N_GLA_PAIRS, ci]

    def prep_pair(ci, p):
        store_keys_t(anchor_keys(ci, p), p, [tile_base(ci, lt) for lt in tiles_of_pair(p)])

    def prep_pair_levels(ci, p):
        tiles = tiles_of_pair(p)
        bases = [tile_reload(ci, lt) for lt in tiles]
        for lv in range(N_LEVELS):
            rhs_tiles = []
            for lt, base in zip(tiles, bases):
                lhs, rhs = tile_level(base, lv)
                lql_s[lv, chunk_rows(ci), lt * LANES:(lt + 1) * LANES] = lhs.astype(BF16)
                rhs_tiles.append(rhs)
            store_keys_t(level_keys(lv, p, ci), p, rhs_tiles)

    def pair_attention(ci, p, single_anchor):
        lanes = pair_lanes(p)
        if single_anchor:
            prod = jnp.dot(qst_s[chunk_rows(ci), lanes], anchor_keys(ci, p)[...],
                           preferred_element_type=F32)
            return jnp.where(masks_ref[MASK_CAUSAL] > 0.0, prod, 0.0).astype(BF16)
        att = None
        for lv in range(N_LEVELS):
            prod = jnp.dot(lql_s[lv, chunk_rows(ci), lanes], level_keys(lv, p, ci)[...],
                           preferred_element_type=F32)
            if lv > 0:
                prod = prod * masks_ref[lv]
            att = prod if att is None else att + prod
        return att.astype(BF16)

    def snapshot_states(ci):
        bi = ci // chunks_per_seq
        for p in range(N_GLA_PAIRS):
            sbg_s[ci, p] = stg_s[bi, p].astype(BF16)
        for hh in range(HGRN_HEADS):
            pos = hh % 2
            sbh_s[ci, hh // 2, pos * LANES:(pos + 1) * LANES, pos * LANES:(pos + 1) * LANES] = (
                sth_ref[bi, hh].astype(BF16))

    def update_states(ci):
        bi = ci // chunks_per_seq
        for p in range(N_GLA_PAIRS):
            dcol = dcol_s[ci, p]
            stg_s[bi, p] = (stg_s[bi, p] * jnp.concatenate([dcol, dcol], axis=1)
                              + lax.dot_general(ksg_s[p, ci], vbd_s[p, ci], TN, preferred_element_type=F32))
        for hh in range(HGRN_HEADS):
            pg, pos = N_GLA_PAIRS + hh // 2, hh % 2
            v_head = vbd_s[pg, ci, pos * CHUNK:(pos + 1) * CHUNK, pos * LANES:(pos + 1) * LANES]
            sth_ref[bi, hh] = sth_ref[bi, hh] * dcol_s[ci, N_GLA_PAIRS + hh] + lax.dot_general(
                ksh_s[hh, ci], v_head, TN, preferred_element_type=F32)

    def pair_output(ci, p, att):
        r = chunk_rows(ci)
        state = sbg_s[ci, p] if p < N_GLA_PAIRS else sbh_s[ci, p - N_GLA_PAIRS]
        lhs = jnp.concatenate([att, qst_s[r, pair_lanes(p)]], axis=1)
        rhs = jnp.concatenate([vbd_s[p, ci], state], axis=0)
        o_s[r, p * PAIR_V:(p + 1) * PAIR_V] = jnp.dot(lhs, rhs, preferred_element_type=F32)

    gate_rows = mod_rows(2)

    def gate_half(h):
        rr = slice(h * half, (h + 1) * half)
        for ht in range(GLA_HEADS + HGRN_HEADS):
            lanes = slice(ht * LANES, (ht + 1) * LANES)
            gb_s[rr, lanes] = (_rms(o_s[rr, lanes]) * gon_ref[:, lanes] * zs_s[rr, lanes]).astype(BF16)

    def project_half(h, n):
        rr = slice(h * half, (h + 1) * half)
        cols = slice(n * PAIR_V, (n + 1) * PAIR_V)
        po_s[rr, cols] = jnp.dot(gb_s[rr, :], wout_ref[:, cols], preferred_element_type=F32)

    def post_half(h):
        rr = slice(h * half, (h + 1) * half)
        gate_h = gate_rows if bb == 1 else gate_rows[rr]
        res = gate_h * (_rms(po_s[rr, :]) * gpost_ref[...])
        if bb == 1:
            y_ref[0, rr, :] = x_ref[0, rr, :] + res
        else:
            seqs = slice(h * (bb // 2), (h + 1) * (bb // 2))
            y_ref[seqs] = x_ref[seqs] + res.reshape(bb // 2, tt, D_MODEL)

    def finish_stages(h):
        return ([functools.partial(project_half, h, n) for n in range(N_COL_BLOCKS)]
                + [functools.partial(post_half, h)])

    fillers = [functools.partial(project_gates, n) for n in range(N_COL_BLOCKS)]
    atts = {}
    for step in range(n_chunks + 2):
        if step < n_chunks:
            for p in range(N_PAIRS):
                prep_pair(step, p)
        if step == 0:
            project_values()
        if 1 <= step <= n_chunks:
            snapshot_states(step - 1)
            if step < n_chunks:
                update_states(step - 1)
            atts[step - 1] = [pair_attention(step - 1, p, True) for p in range(N_PAIRS)]
        if fillers:
            fillers.pop(0)()
        if step >= 2:
            for p in range(N_PAIRS):
                pair_output(step - 2, p, atts[step - 2][p])
            if (step - 1) * CHUNK == half:
                while fillers:
                    fillers.pop(0)()
                gate_half(0)
                fillers = finish_stages(0)
    last_stages = finish_stages(1)
    for stage in ([functools.partial(gate_half, 1)] + fillers + last_stages[:-1]
                  + [functools.partial(update_states, n_chunks - 1)] + last_stages[-1:]):
        stage()

    total = None
    for ci in range(n_chunks):
        t = jnp.sum(g_s[ci * CHUNK:(ci + 1) * CHUNK, :], axis=0, keepdims=True)
        total = t if total is None else jnp.minimum(total, t)
    mild = jnp.min(total) >= -SINGLE_ANCHOR_MAX_LOG2

    @pl.when(jnp.logical_not(mild))
    def _():
        def fix_chunk(ci, carry):
            for p in range(N_PAIRS):
                prep_pair_levels(ci, p)
            for p in range(N_PAIRS):
                pair_output(ci, p, pair_attention(ci, p, False))
            return carry

        lax.fori_loop(0, n_chunks, fix_chunk, 0)
        for h in range(2):
            for stage in [functools.partial(gate_half, h)] + finish_stages(h):
                stage()

    @pl.when(pl.program_id(1) == pl.num_programs(1) - 1)
    def _():
        for bi in range(bb):
            for p in range(N_GLA_PAIRS):
                for h in range(2):
                    stg_ref[bi, 2 * p + h] = stg_s[bi, p, h * GLA_DK:(h + 1) * GLA_DK,
                                                   h * GLA_DV:(h + 1) * GLA_DV]


def _mixer_call(x, mods, stg0, sth0, w_in_parts, w_alpha, b_alpha, lb_logits, g_on, w_out, g_post, masks,
                *, bb, tt):
    nb, seq, _ = x.shape
    rows = bb * tt
    assert nb % bb == 0 and seq % tt == 0 and tt % CHUNK == 0
    n_chunks = rows // CHUNK
    per_seq4 = lambda b, j: (b, 0, 0, 0)

    def resident(shape):
        return pl.BlockSpec(shape, lambda b, j: (0,) * len(shape), pipeline_mode=pl.Buffered(1))

    def initial_state(st0):
        assert st0.shape[0] in (bb, nb)
        shared = st0.shape[0] == bb and nb != bb
        return pl.BlockSpec((bb,) + st0.shape[1:], (lambda b, j: (0, 0, 0, 0)) if shared else per_seq4)

    kernel = functools.partial(_mixer_kernel, bb=bb, tt=tt)
    stg_shape =(nb, GLA_HEADS, GLA_DK, GLA_DV)
    sth_shape = (nb, HGRN_HEADS, HGRN_D, HGRN_D)
    return pl.pallas_call(
        kernel,
        grid=(nb // bb, seq // tt),
        in_specs=[
            pl.BlockSpec((bb, tt, D_MODEL), lambda b, j: (b, j, 0)),
            pl.BlockSpec((bb, 3, D_MODEL), lambda b, j: (b, 0, 0)),
            initial_state(stg0),
            initial_state(sth0),
            *[resident(part.shape) for part in w_in_parts],
            resident((RANK_PAD, GLA_K)),
            resident((1, GLA_K)),
            resident(lb_logits.shape),
            resident((1, D_MODEL)),
            resident((D_MODEL, D_MODEL)),
            resident((1, D_MODEL)),
            resident(masks.shape),
        ],
        out_specs=[
            pl.BlockSpec((bb, tt, D_MODEL), lambda b, j: (b, j, 0)),
            pl.BlockSpec((bb,) + stg_shape[1:], per_seq4),
            pl.BlockSpec((bb,) + sth_shape[1:], per_seq4),
        ],
        out_shape=[
            jax.ShapeDtypeStruct(x.shape, F32),
            jax.ShapeDtypeStruct(stg_shape, F32),
            jax.ShapeDtypeStruct(sth_shape, F32),
        ],
        scratch_shapes=[
            pltpu.VMEM((rows, D_MODEL), BF16),
            pltpu.VMEM((rows, K_LANES), F32),
            pltpu.VMEM((rows, K_LANES), F32),
            pltpu.VMEM((rows, K_LANES), F32),
            pltpu.VMEM((N_PAIRS, n_chunks, 2 * CHUNK, PAIR_V), BF16),
            pltpu.VMEM((rows, D_MODEL), F32),
            pltpu.VMEM((rows, D_MODEL), F32),
            pltpu.VMEM((n_chunks, N_K_TILES, CHUNK, LANES), F32),
            pltpu.VMEM((rows, K_LANES), BF16),
            pltpu.VMEM((N_GLA_PAIRS, n_chunks, LANES, 2 * CHUNK), BF16),
            pltpu.VMEM((N_HGRN_PAIRS, n_chunks, PAIR_V, 2 * CHUNK), BF16),
            pltpu.VMEM((N_GLA_PAIRS, n_chunks, 2 * CHUNK, LANES), BF16),
            pltpu.VMEM((HGRN_HEADS, n_chunks, CHUNK, LANES), BF16),
            pltpu.VMEM((n_chunks, N_K_TILES, LANES, LANES), F32),
            pltpu.VMEM((n_chunks, N_GLA_PAIRS, LANES, PAIR_V), BF16),
            pltpu.VMEM((n_chunks, N_HGRN_PAIRS, PAIR_V, PAIR_V), BF16),
            pltpu.VMEM((N_LEVELS, rows, K_LANES), BF16),
            pltpu.VMEM((N_LEVELS, N_GLA_PAIRS, n_chunks, LANES, 2 * CHUNK), BF16),
            pltpu.VMEM((N_LEVELS, N_HGRN_PAIRS, n_chunks, PAIR_V, 2 * CHUNK), BF16),
            pltpu.VMEM((rows, D_MODEL), BF16),
            pltpu.VMEM((rows, D_MODEL), F32),
            pltpu.VMEM((bb, N_GLA_PAIRS, LANES, PAIR_V), F32),
        ],
        compiler_params=pltpu.CompilerParams(
            dimension_semantics=("arbitrary", "arbitrary"),
            vmem_limit_bytes=VMEM_LIMIT_BYTES,
        ),
        name="mixer_bb%d_tt%d" % (bb, tt),
    )(x, mods, stg0, sth0, *w_in_parts, w_alpha, b_alpha, lb_logits, g_on, w_out, g_post, masks)


def kernel(x_prompt, x_sample, c_prompt, c_sample, state_gla, state_hgrn, w_ada, b_ada, g_pre, w_in,
           w_alpha, b_alpha, g_onorm_gla, hgrn_lb_logits, g_onorm_hgrn, w_out, g_post):
    assert w_ada.shape[0] == 1, "single-layer problem"
    bp = x_prompt.shape[0]

    w_in_parts, offset = [], 0
    for width in IN_SPLITS:
        part = w_in[0][:, offset:offset + width].astype(BF16)
        if width < LANES:
            part = jnp.pad(part, ((0, 0), (0, RANK_PAD - width)))
        w_in_parts.append(part)
        offset += width
    w_alpha_p = jnp.concatenate(
        [w_alpha[0], jnp.zeros((RANK_PAD - GATE_RANK, GLA_K), w_alpha.dtype)], axis=0).astype(BF16)
    g_on = jnp.concatenate([jnp.tile(g_onorm_gla[0], GLA_HEADS),
                            jnp.tile(g_onorm_hgrn[0], HGRN_HEADS)])[None, :]
    masks = jnp.asarray(_level_masks())

    c_all = jnp.concatenate([c_prompt, c_sample], axis=0)
    mods = _ada_call(c_all, w_ada[0], b_ada, g_pre).reshape(c_all.shape[0], 3, D_MODEL)

    shared = (tuple(w_in_parts), w_alpha_p, b_alpha, hgrn_lb_logits, g_on, w_out[0].astype(BF16), g_post, masks)
    stg0_p = jnp.zeros((1, GLA_HEADS, GLA_DK, GLA_DV), F32)
    sth0_p = jnp.zeros((1, HGRN_HEADS, HGRN_D, HGRN_D), F32)
    yp, gla_p, hgrn_p = _mixer_call(x_prompt, mods[:bp], stg0_p, sth0_p, *shared, bb=1, tt=ROWS_PER_STEP)
    ys, gla_s, hgrn_s = _mixer_call(x_sample, mods[bp:], state_gla[0], state_hgrn[0],
                                    *shared, bb=ROWS_PER_STEP // x_sample.shape[1], tt=x_sample.shape[1])
    return (yp, ys, gla_p[None], hgrn_p[None],
            gla_s[None].astype(state_gla.dtype), hgrn_s[None].astype(state_hgrn.dtype))
```

```python
import functools
import math

import numpy as np
import jax
import jax.numpy as jnp
from jax import lax
from jax.experimental import pallas as pl
from jax.experimental.pallas import tpu as pltpu

D_MODEL = 1024
CHUNK = 64
GLA_HEADS = 4
GLA_V = D_MODEL // 2
GLA_K = GLA_V // 2
GLA_DK = GLA_K // GLA_HEADS
GLA_DV = GLA_V // GLA_HEADS
GATE_RANK = 16
GATE_TAU = 16.0
HGRN_HEADS = 4
HGRN_W = D_MODEL - GLA_V
HGRN_D = HGRN_W // HGRN_HEADS
IN_SPLITS = (GLA_K, GLA_K, GLA_V, GLA_V, GATE_RANK, HGRN_W, HGRN_W, HGRN_W, HGRN_W)
EPS = 1e-6
LOG2E = math.log2(math.e)

LANES = 128
SUBLANES = 8
K_LANES = GLA_K + HGRN_W
N_K_TILES = K_LANES // LANES
N_GLA_PAIRS = GLA_HEADS // 2
N_HGRN_PAIRS = HGRN_HEADS // 2
N_PAIRS = N_GLA_PAIRS + N_HGRN_PAIRS
PAIR_V = 2 * LANES
N_COL_BLOCKS = D_MODEL // PAIR_V
RANK_PAD = LANES
HALF_SIZES = (32, 16, 8, 4, 2, 1)
N_LEVELS = len(HALF_SIZES) + 1
MASK_CAUSAL = N_LEVELS
SINGLE_ANCHOR_MAX_LOG2 = 90.0
ROWS_PER_STEP = 256
VMEM_LIMIT_BYTES = 56 * 1024 * 1024

F32 = jnp.float32
BF16 = jnp.bfloat16
TN = (((0,), (0,)), ((), ()))


def _level_masks():
    t = np.arange(CHUNK)[:, None]
    s = np.arange(CHUNK)[None, :]
    masks = [(t // (2 * m)) == (s // (2 * m)) for m in HALF_SIZES]
    masks.append(t == s)
    masks.append(t >= s)
    masks = np.stack(masks).astype(np.float32)
    return np.concatenate([masks, masks], axis=2)


def _rms(x):
    return x * lax.rsqrt(jnp.mean(x * x, axis=-1, keepdims=True) + EPS)


def _sigmoid(x):
    return 1.0 / (1.0 + jnp.exp(-x))


def _log2_sigmoid(x):
    return jnp.minimum(x, 0.0) * LOG2E - jnp.log2(1.0 + jnp.exp(-jnp.abs(x)))


def _ada_kernel(c_ref, w_ref, b_ref, gpre_ref, out_ref):
    n = pl.program_id(0)
    val = jnp.dot(c_ref[...].astype(BF16), w_ref[...].astype(BF16),
                  preferred_element_type=F32) + b_ref[...]
    out_ref[...] = jnp.where(n == 1, gpre_ref[...] * (1.0 + val), val)


def _ada_call(c_all, w_ada, b_ada, g_pre):
    nb = c_all.shape[0]
    return pl.pallas_call(
        _ada_kernel,
        grid=(3,),
        in_specs=[
            pl.BlockSpec((nb, D_MODEL), lambda n: (0, 0)),
            pl.BlockSpec((D_MODEL, D_MODEL), lambda n: (0, n)),
            pl.BlockSpec((1, D_MODEL), lambda n: (0, n)),
            pl.BlockSpec((1, D_MODEL), lambda n: (0, 0)),
        ],
        out_specs=pl.BlockSpec((nb, D_MODEL), lambda n: (0, n)),
        out_shape=jax.ShapeDtypeStruct((nb, 3 * D_MODEL), F32),
        name="ada_mod",
    )(c_all, w_ada, b_ada, g_pre)


def _ref_rows(b_ref, block, off):
    pieces = []
    if block >= SUBLANES:
        for j in range(CHUNK // block):
            row = b_ref[j * block + off:j * block + off + 1, :]
            pieces.append(jnp.broadcast_to(row, (block, LANES)))
    else:
        assert block == SUBLANES // 2
        sub = lax.broadcasted_iota(jnp.int32, (SUBLANES, LANES), 0)
        for i in range(CHUNK // SUBLANES):
            lo = b_ref[SUBLANES * i + off:SUBLANES * i + off + 1, :]
            hi = b_ref[SUBLANES * i + block + off:SUBLANES * i + block + off + 1, :]
            pieces.append(jnp.where(sub < block,
                                    jnp.broadcast_to(lo, (SUBLANES, LANES)),
                                    jnp.broadcast_to(hi, (SUBLANES, LANES))))
    return jnp.concatenate(pieces, axis=0)


def _mixer_kernel(x_ref, mods_ref, stg0_ref, sth0_ref,
                  wqa_ref, wka_ref, wva_ref, wza_ref, wal_ref, wqh_ref, wfh_ref, wih_ref, wzh_ref,
                  walpha_ref, balpha_ref, lbl_ref,
                  gon_ref, wout_ref, gpost_ref, masks_ref,
                  y_ref, stg_ref, sth_ref,
                  hb_s, q_s, k_s, g_s, vbd_s, zs_s, o_s, b_s, qst_s, kag_s, kah_s, ksg_s, ksh_s, dcol_s, sbg_s, sbh_s,
                  lql_s, klg_s, klh_s, gb_s, po_s, stg_s, strong_s,
                  *, bb, tt):
    rows = bb * tt
    n_chunks = rows // CHUNK
    chunks_per_seq = tt // CHUNK

    @pl.when((pl.program_id(0) == 0) & (pl.program_id(1) == 0))
    def _():
        for ref in (vbd_s, kah_s, klh_s, sbh_s):
            ref[...] = jnp.zeros(ref.shape, BF16)

    @pl.when(pl.program_id(1) == 0)
    def _():
        zero = jnp.zeros((GLA_DK, GLA_DV), F32)
        for bi in range(bb):
            for p in range(N_GLA_PAIRS):
                first, second = stg0_ref[bi, 2 * p], stg0_ref[bi, 2 * p + 1]
                stg_s[bi, p] = jnp.concatenate([jnp.concatenate([first, zero], axis=1),
                                                jnp.concatenate([zero, second], axis=1)], axis=0)
        sth_ref[...] = sth0_ref[...]

    def mod_rows(i):
        if bb == 1:
            return mods_ref[0, i:i + 1, :]
        return jnp.concatenate(
            [jnp.broadcast_to(mods_ref[bi, i:i + 1, :], (tt, D_MODEL)) for bi in range(bb)], axis=0)

    half = rows // 2
    x = x_ref[...].reshape(rows, D_MODEL)
    hb_s[...] = (_rms(x) * mod_rows(1) + mod_rows(0)).astype(BF16)

    def proj(w_ref, n=0):
        width = min(PAIR_V, w_ref.shape[1])
        return jnp.dot(hb_s[...], w_ref[:, n * width:(n + 1) * width], preferred_element_type=F32)

    def head_group_block(gla_ref, hgrn_ref, n):
        n_gla = GLA_V // PAIR_V
        return proj(gla_ref, n) if n < n_gla else proj(hgrn_ref, n - n_gla)

    a_lr = proj(wal_ref)
    ga = jnp.dot(a_lr.astype(BF16), walpha_ref[...], preferred_element_type=F32) + balpha_ref[...]
    g_s[:, :GLA_K] = _log2_sigmoid(ga) * (1.0 / GATE_TAU)

    k_s[:, :GLA_K] = proj(wka_ref)
    lbl = lbl_ref[...]
    lmax = jnp.max(lbl, axis=0, keepdims=True)
    lexp = jnp.exp(lbl - lmax)
    lb = lexp[0:1, :] / jnp.sum(lexp, axis=0, keepdims=True)
    for n in range(HGRN_W // PAIR_V):
        cols = slice(GLA_K + n * PAIR_V, GLA_K + (n + 1) * PAIR_V)
        lb_n = lb[:, n * PAIR_V:(n + 1) * PAIR_V]
        f = lb_n + (1.0 - lb_n) * _sigmoid(proj(wfh_ref, n))
        k_s[:, cols] = 1.0 - f
        g_s[:, cols] = jnp.log2(f)

    q_s[:, :GLA_K] = proj(wqa_ref) * (GLA_DK ** -0.5)
    for n in range(HGRN_W // PAIR_V):
        cols = slice(GLA_K + n * PAIR_V, GLA_K + (n + 1) * PAIR_V)
        qh = proj(wqh_ref, n)
        q_s[:, cols] = qh * _sigmoid(qh)

    total = None
    for ci in range(n_chunks):
        t = jnp.sum(g_s[ci * CHUNK:(ci + 1) * CHUNK, :], axis=0, keepdims=True)
        total = t if total is None else jnp.minimum(total, t)
    strong_s[0] = jnp.where(jnp.min(total) >= -SINGLE_ANCHOR_MAX_LOG2, 0, 1).astype(jnp.int32)

    def project_values():
        for p in range(N_PAIRS):
            pv = head_group_block(wva_ref, wih_ref, p).astype(BF16)
            for ci in range(n_chunks):
                r = slice(ci * CHUNK, (ci + 1) * CHUNK)
                vbd_s[p, ci, :CHUNK, :LANES] = pv[r, :LANES]
                vbd_s[p, ci, CHUNK:, LANES:] = pv[r, LANES:]

    def project_gates(n):
        cols = slice(n * PAIR_V, (n + 1) * PAIR_V)
        pz = head_group_block(wza_ref, wzh_ref, n)
        zs_s[:, cols] = pz * _sigmoid(pz)

    row_i = lax.broadcasted_iota(jnp.int32, (CHUNK, LANES), 0)
    lane_i = lax.broadcasted_iota(jnp.int32, (CHUNK, LANES), 1)
    head0 = lane_i < GLA_DK

    def tile_base(ci, lt):
        r = slice(ci * CHUNK, (ci + 1) * CHUNK)
        lanes = slice(lt * LANES, (lt + 1) * LANES)
        b_t = b_s.at[ci, lt]
        g = g_s[r, lanes]
        q = q_s[r, lanes]
        k = k_s[r, lanes]
        b = g
        for sh in (1, 2, 4):
            b = b + jnp.where(row_i >= sh, pltpu.roll(b, sh, 0), 0.0)
        for sh in (8, 16, 32):
            b = b + jnp.concatenate([jnp.zeros((sh, LANES), F32), b[:CHUNK - sh]], axis=0)
        b_t[...] = b
        b_last = b_t[CHUNK - 1:CHUNK, :]
        dcol_s[ci, lt] = jnp.exp2(jnp.broadcast_to(b_last, (LANES, LANES)).T)
        qst_s[r, lanes] = (q * jnp.exp2(b)).astype(BF16)
        k_state = k * jnp.exp2(b_last - b)
        if lt < N_GLA_PAIRS:
            ksg_s[lt, ci, :CHUNK, :] = jnp.where(head0, k_state, 0.0).astype(BF16)
            ksg_s[lt, ci, CHUNK:, :] = jnp.where(head0, 0.0, k_state).astype(BF16)
        else:
            ksh_s[lt - N_GLA_PAIRS, ci] = k_state.astype(BF16)
        return k * jnp.exp2(-b)

    def tile_reload(ci, lt):
        r = slice(ci * CHUNK, (ci + 1) * CHUNK)
        lanes = slice(lt * LANES, (lt + 1) * LANES)
        b_t = b_s.at[ci, lt]
        return q_s[r, lanes], k_s[r, lanes], g_s[r, lanes], b_t[...], b_t

    def tile_level(base, lv):
        q, k, g, b, b_t = base
        if lv == N_LEVELS - 1:
            return q, k
        m = HALF_SIZES[lv]
        upper = (row_i & (2 * m - 1)) >= m
        if m == 1:
            lhs, rhs = q * jnp.exp2(g), k
        else:
            e = jnp.exp2(-jnp.abs(b - _ref_rows(b_t, 2 * m, m - 1)))
            lhs, rhs = q * e, k * e
        return jnp.where(upper, lhs, 0.0), jnp.where(upper, 0.0, rhs)

    def store_keys_t(dst, p, rhs_tiles):
        if p < N_GLA_PAIRS:
            rhs = rhs_tiles[0]
            stacked = jnp.concatenate([jnp.where(head0, rhs, 0.0), jnp.where(head0, 0.0, rhs)], axis=0)
            dst[...] = stacked.astype(BF16).T
        else:
            kt = jnp.concatenate(rhs_tiles, axis=0).astype(BF16).T
            dst[:LANES, :CHUNK] = kt[:, :CHUNK]
            dst[LANES:, CHUNK:] = kt[:, CHUNK:]

    def tiles_of_pair(p):
        if p < N_GLA_PAIRS:
            return (p,)
        return tuple(N_GLA_PAIRS + 2 * (p - N_GLA_PAIRS) + pos for pos in range(2))

    def pair_lanes(p):
        if p < N_GLA_PAIRS:
            return slice(p * LANES, (p + 1) * LANES)
        return slice(GLA_K + (p - N_GLA_PAIRS) * PAIR_V, GLA_K + (p - N_GLA_PAIRS + 1) * PAIR_V)

    def anchor_keys(ci, p):
        return kag_s.at[p, ci] if p < N_GLA_PAIRS else kah_s.at[p - N_GLA_PAIRS, ci]

    def level_keys(lv, p, ci):
        return klg_s.at[lv, p, ci] if p < N_GLA_PAIRS else klh_s.at[lv, p - N_GLA_PAIRS, ci]

    def prep_pair(ci, p):
        store_keys_t(anchor_keys(ci, p), p, [tile_base(ci, lt) for lt in tiles_of_pair(p)])

    def prep_pair_levels(ci, p):
        tiles = tiles_of_pair(p)
        bases = [tile_reload(ci, lt) for lt in tiles]
        for lv in range(N_LEVELS):
            rhs_tiles = []
            for lt, base in zip(tiles, bases):
                lhs, rhs = tile_level(base, lv)
                lql_s[lv, ci * CHUNK:(ci + 1) * CHUNK, lt * LANES:(lt + 1) * LANES] = lhs.astype(BF16)
                rhs_tiles.append(rhs)
            store_keys_t(level_keys(lv, p, ci), p, rhs_tiles)

    def pair_attention(ci, p, single_anchor):
        lanes = pair_lanes(p)
        if single_anchor:
            prod = jnp.dot(qst_s[ci * CHUNK:(ci + 1) * CHUNK, lanes], anchor_keys(ci, p)[...],
                           preferred_element_type=F32)
            return jnp.where(masks_ref[MASK_CAUSAL] > 0.0, prod, 0.0).astype(BF16)
        att = None
        for lv in range(N_LEVELS):
            prod = jnp.dot(lql_s[lv, ci * CHUNK:(ci + 1) * CHUNK, lanes], level_keys(lv, p, ci)[...],
                           preferred_element_type=F32)
            if lv > 0:
                prod = prod * masks_ref[lv]
            att = prod if att is None else att + prod
        return att.astype(BF16)

    def snapshot_states(ci):
        bi = ci // chunks_per_seq
        for p in range(N_GLA_PAIRS):
            sbg_s[ci, p] = stg_s[bi, p].astype(BF16)
        for hh in range(HGRN_HEADS):
            pos = hh % 2
            sbh_s[ci, hh // 2, pos * LANES:(pos + 1) * LANES, pos * LANES:(pos + 1) * LANES] = (
                sth_ref[bi, hh].astype(BF16))

    def update_states(ci):
        bi = ci // chunks_per_seq
        for p in range(N_GLA_PAIRS):
            dcol = dcol_s[ci, p]
            stg_s[bi, p] = (stg_s[bi, p] * jnp.concatenate([dcol, dcol], axis=1)
                              + lax.dot_general(ksg_s[p, ci], vbd_s[p, ci], TN, preferred_element_type=F32))
        for hh in range(HGRN_HEADS):
            pg, pos = N_GLA_PAIRS + hh // 2, hh % 2
            v_head = vbd_s[pg, ci, pos * CHUNK:(pos + 1) * CHUNK, pos * LANES:(pos + 1) * LANES]
            sth_ref[bi, hh] = sth_ref[bi, hh] * dcol_s[ci, N_GLA_PAIRS + hh] + lax.dot_general(
                ksh_s[hh, ci], v_head, TN, preferred_element_type=F32)

    def pair_output(ci, p, att):
        r = slice(ci * CHUNK, (ci + 1) * CHUNK)
        state = sbg_s[ci, p] if p < N_GLA_PAIRS else sbh_s[ci, p - N_GLA_PAIRS]
        lhs = jnp.concatenate([att, qst_s[r, pair_lanes(p)]], axis=1)
        rhs = jnp.concatenate([vbd_s[p, ci], state], axis=0)
        o_s[r, p * PAIR_V:(p + 1) * PAIR_V] = jnp.dot(lhs, rhs, preferred_element_type=F32)

    gate_rows = mod_rows(2)

    def gate_half(h):
        rr = slice(h * half, (h + 1) * half)
        for ht in range(GLA_HEADS + HGRN_HEADS):
            lanes = slice(ht * LANES, (ht + 1) * LANES)
            gb_s[rr, lanes] = (_rms(o_s[rr, lanes]) * gon_ref[:, lanes] * zs_s[rr, lanes]).astype(BF16)

    def project_half(h, n):
        rr = slice(h * half, (h + 1) * half)
        cols = slice(n * PAIR_V, (n + 1) * PAIR_V)
        po_s[rr, cols] = jnp.dot(gb_s[rr, :], wout_ref[:, cols], preferred_element_type=F32)

    def post_half(h):
        rr = slice(h * half, (h + 1) * half)
        gate_h = gate_rows if bb == 1 else gate_rows[rr]
        res = gate_h * (_rms(po_s[rr, :]) * gpost_ref[...])
        if bb == 1:
            y_ref[0, rr, :] = x_ref[0, rr, :] + res
        else:
            seqs = slice(h * (bb // 2), (h + 1) * (bb // 2))
            y_ref[seqs] = x_ref[seqs] + res.reshape(bb // 2, tt, D_MODEL)

    def finish_stages(h):
        return ([functools.partial(project_half, h, n) for n in range(N_COL_BLOCKS)]
                + [functools.partial(post_half, h)])

    fillers = [functools.partial(project_gates, n) for n in range(N_COL_BLOCKS)]
    atts = {}
    for step in range(n_chunks + 2):
        if step < n_chunks:
            for p in range(N_PAIRS):
                prep_pair(step, p)
        if step == 0:
            project_values()
        if 1 <= step <= n_chunks:
            snapshot_states(step - 1)
            if step < n_chunks:
                update_states(step - 1)
            atts[step - 1] = [pair_attention(step - 1, p, True) for p in range(N_PAIRS)]
        if fillers:
            fillers.pop(0)()
        if step >= 2:
            for p in range(N_PAIRS):
                pair_output(step - 2, p, atts[step - 2][p])
            if (step - 1) * CHUNK == half:
                while fillers:
                    fillers.pop(0)()
                gate_half(0)
                fillers = finish_stages(0)
    last_stages = finish_stages(1)
    for stage in ([functools.partial(gate_half, 1)] + fillers + last_stages[:-1]
                  + [functools.partial(update_states, n_chunks - 1)] + last_stages[-1:]):
        stage()

    @pl.when(strong_s[0] != 0)
    def _():
        for ci in range(n_chunks):
            for p in range(N_PAIRS):
                prep_pair_levels(ci, p)
            for p in range(N_PAIRS):
                pair_output(ci, p, pair_attention(ci, p, False))
        for h in range(2):
            for stage in [functools.partial(gate_half, h)] + finish_stages(h):
                stage()

    @pl.when(pl.program_id(1) == pl.num_programs(1) - 1)
    def _():
        for bi in range(bb):
            for p in range(N_GLA_PAIRS):
                for h in range(2):
                    stg_ref[bi, 2 * p + h] = stg_s[bi, p, h * GLA_DK:(h + 1) * GLA_DK,
                                                   h * GLA_DV:(h + 1) * GLA_DV]


def _mixer_call(x, mods, stg0, sth0, w_in_parts, w_alpha, b_alpha, lb_logits, g_on, w_out, g_post, masks,
                *, bb, tt):
    nb, seq, _ = x.shape
    rows = bb * tt
    assert nb % bb == 0 and seq % tt == 0 and tt % CHUNK == 0
    n_chunks = rows // CHUNK
    per_seq4 = lambda b, j: (b, 0, 0, 0)

    def resident(shape):
        return pl.BlockSpec(shape, lambda b, j: (0,) * len(shape), pipeline_mode=pl.Buffered(1))

    def initial_state(st0):
        assert st0.shape[0] in (bb, nb)
        shared = st0.shape[0] == bb and nb != bb
        return pl.BlockSpec((bb,) + st0.shape[1:], (lambda b, j: (0, 0, 0, 0)) if shared else per_seq4)

    kernel = functools.partial(_mixer_kernel, bb=bb, tt=tt)
    stg_shape =(nb, GLA_HEADS, GLA_DK, GLA_DV)
    sth_shape = (nb, HGRN_HEADS, HGRN_D, HGRN_D)
    return pl.pallas_call(
        kernel,
        grid=(nb // bb, seq // tt),
        in_specs=[
            pl.BlockSpec((bb, tt, D_MODEL), lambda b, j: (b, j, 0)),
            pl.BlockSpec((bb, 3, D_MODEL), lambda b, j: (b, 0, 0)),
            initial_state(stg0),
            initial_state(sth0),
            *[resident(part.shape) for part in w_in_parts],
            resident((RANK_PAD, GLA_K)),
            resident((1, GLA_K)),
            resident(lb_logits.shape),
            resident((1, D_MODEL)),
            resident((D_MODEL, D_MODEL)),
            resident((1, D_MODEL)),
            resident(masks.shape),
        ],
        out_specs=[
            pl.BlockSpec((bb, tt, D_MODEL), lambda b, j: (b, j, 0)),
            pl.BlockSpec((bb,) + stg_shape[1:], per_seq4),
            pl.BlockSpec((bb,) + sth_shape[1:], per_seq4),
        ],
        out_shape=[
            jax.ShapeDtypeStruct(x.shape, F32),
            jax.ShapeDtypeStruct(stg_shape, F32),
            jax.ShapeDtypeStruct(sth_shape, F32),
        ],
        scratch_shapes=[
            pltpu.VMEM((rows, D_MODEL), BF16),
            pltpu.VMEM((rows, K_LANES), F32),
            pltpu.VMEM((rows, K_LANES), F32),
            pltpu.VMEM((rows, K_LANES), F32),
            pltpu.VMEM((N_PAIRS, n_chunks, 2 * CHUNK, PAIR_V), BF16),
            pltpu.VMEM((rows, D_MODEL), F32),
            pltpu.VMEM((rows, D_MODEL), F32),
            pltpu.VMEM((n_chunks, N_K_TILES, CHUNK, LANES), F32),
            pltpu.VMEM((rows, K_LANES), BF16),
            pltpu.VMEM((N_GLA_PAIRS, n_chunks, LANES, 2 * CHUNK), BF16),
            pltpu.VMEM((N_HGRN_PAIRS, n_chunks, PAIR_V, 2 * CHUNK), BF16),
            pltpu.VMEM((N_GLA_PAIRS, n_chunks, 2 * CHUNK, LANES), BF16),
            pltpu.VMEM((HGRN_HEADS, n_chunks, CHUNK, LANES), BF16),
            pltpu.VMEM((n_chunks, N_K_TILES, LANES, LANES), F32),
            pltpu.VMEM((n_chunks, N_GLA_PAIRS, LANES, PAIR_V), BF16),
            pltpu.VMEM((n_chunks, N_HGRN_PAIRS, PAIR_V, PAIR_V), BF16),
            pltpu.VMEM((N_LEVELS, rows, K_LANES), BF16),
            pltpu.VMEM((N_LEVELS, N_GLA_PAIRS, n_chunks, LANES, 2 * CHUNK), BF16),
            pltpu.VMEM((N_LEVELS, N_HGRN_PAIRS, n_chunks, PAIR_V, 2 * CHUNK), BF16),
            pltpu.VMEM((rows, D_MODEL), BF16),
            pltpu.VMEM((rows, D_MODEL), F32),
            pltpu.VMEM((bb, N_GLA_PAIRS, LANES, PAIR_V), F32),
            pltpu.SMEM((1,), jnp.int32),
        ],
        compiler_params=pltpu.CompilerParams(
            dimension_semantics=("arbitrary", "arbitrary"),
            vmem_limit_bytes=VMEM_LIMIT_BYTES,
        ),
        name="mixer_bb%d_tt%d" % (bb, tt),
    )(x, mods, stg0, sth0, *w_in_parts, w_alpha, b_alpha, lb_logits, g_on, w_out, g_post, masks)


def kernel(x_prompt, x_sample, c_prompt, c_sample, state_gla, state_hgrn, w_ada, b_ada, g_pre, w_in,
           w_alpha, b_alpha, g_onorm_gla, hgrn_lb_logits, g_onorm_hgrn, w_out, g_post):
    assert w_ada.shape[0] == 1, "single-layer problem"
    bp = x_prompt.shape[0]

    w_in_parts, offset = [], 0
    for width in IN_SPLITS:
        part = w_in[0][:, offset:offset + width].astype(BF16)
        if width < LANES:
            part = jnp.pad(part, ((0, 0), (0, RANK_PAD - width)))
        w_in_parts.append(part)
        offset += width
    w_alpha_p = jnp.concatenate(
        [w_alpha[0], jnp.zeros((RANK_PAD - GATE_RANK, GLA_K), w_alpha.dtype)], axis=0).astype(BF16)
    g_on = jnp.concatenate([jnp.tile(g_onorm_gla[0], GLA_HEADS),
                            jnp.tile(g_onorm_hgrn[0], HGRN_HEADS)])[None, :]
    masks = jnp.asarray(_level_masks())

    c_all = jnp.concatenate([c_prompt, c_sample], axis=0)
    mods = _ada_call(c_all, w_ada[0], b_ada, g_pre).reshape(c_all.shape[0], 3, D_MODEL)

    shared = (tuple(w_in_parts), w_alpha_p, b_alpha, hgrn_lb_logits, g_on, w_out[0].astype(BF16), g_post, masks)
    stg0_p = jnp.zeros((1, GLA_HEADS, GLA_DK, GLA_DV), F32)
    sth0_p = jnp.zeros((1, HGRN_HEADS, HGRN_D, HGRN_D), F32)
    yp, gla_p, hgrn_p = _mixer_call(x_prompt, mods[:bp], stg0_p, sth0_p, *shared, bb=1, tt=ROWS_PER_STEP)
    ys, gla_s, hgrn_s = _mixer_call(x_sample, mods[bp:], state_gla[0], state_hgrn[0],
                                    *shared, bb=ROWS_PER_STEP // x_sample.shape[1], tt=x_sample.shape[1])
    return (yp, ys, gla_p[None], hgrn_p[None],
            gla_s[None].astype(state_gla.dtype), hgrn_s[None].astype(state_hgrn.dtype))
```
